```python
import math
import jax, jax.numpy as jnp
from jax import lax
import numpy as np

D_MODEL = 1024
BATCH = 2
SEQ = 8192
DEPTH = 1
DEC_BATCH = 32
DEC_SEQ = 2048
PAST_LEN = 128

N_ATT_HEADS = 4
ATT_HEAD_DIM = 64
ATT_V_DIM = 2 * ATT_HEAD_DIM
ROT_DIM = ATT_HEAD_DIM // 4
ROPE_THETA = 500000.0
Q_BLOCK = 128
ATT_QK_W = N_ATT_HEADS * 2 * ATT_HEAD_DIM
ATT_V_W = N_ATT_HEADS * ATT_V_DIM
HY_WIDTH = 512
FILTER_EMB = 33
FILTER_HIDDEN = 64
DECAY_TARGET = 1e-2
FAST_DECAY_PCT = 0.3
SLOW_DECAY_PCT = 1.5
N_BRANCH = 2
IN_W = 2 * ATT_QK_W + ATT_V_W + 3 * HY_WIDTH + N_BRANCH * D_MODEL
N_MEM = 256
X_HEADS = 4
X_HEAD_DIM = D_MODEL // X_HEADS
N_EXPERTS = 16
EC_CAPACITY_FACTOR = 2
EXPERT_FF = 1024
EPS = 1e-6

kernel_name = "hybrid_diffattn_hyena_ec_encoder"

F32 = jnp.float32


def rmsnorm(x, g):
    xf = x.astype(F32)
    y = xf * lax.rsqrt(jnp.mean(xf * xf, axis=-1, keepdims=True) + EPS)
    return (y * g.astype(F32)).astype(x.dtype)


def rope_partial(t, pos):
    inv = ROPE_THETA ** (-jnp.arange(0, ROT_DIM, 2, dtype=F32) / ROT_DIM)
    ang = pos[:, None] * inv[None, :]
    cos = jnp.cos(ang)[:, None, None, :]
    sin = jnp.sin(ang)[:, None, None, :]
    tr = t[..., :ROT_DIM].astype(F32)
    t1, t2 = tr[..., :ROT_DIM // 2], tr[..., ROT_DIM // 2:]
    rot = jnp.concatenate([t1 * cos - t2 * sin, t1 * sin + t2 * cos], axis=-1)
    return jnp.concatenate([rot.astype(t.dtype), t[..., ROT_DIM:]], axis=-1)


def diff_attention(q, k, v, lam):
    B, L, H, _, dh = q.shape
    nb = L // Q_BLOCK
    qb = q.reshape(B, nb, Q_BLOCK, H, 2, dh).transpose(1, 0, 2, 3, 4, 5)
    scale = dh ** -0.5

    def block(qi):
        s = jnp.einsum('bqhcd,bkhcd->bhcqk', qi, k).astype(F32) * scale
        p = jax.nn.softmax(s, axis=-1)
        w = p[:, :, 0] - lam * p[:, :, 1]
        return jnp.einsum('bhqk,bkhe->bqhe', w.astype(v.dtype), v)

    o = lax.map(block, qb)
    return o.transpose(1, 0, 2, 3, 4).reshape(B, L, H, v.shape[-1])


def short_conv(u, w, b):
    up = jnp.pad(u, ((0, 0), (1, 1), (0, 0)))
    return up[:, :-2] * w[0] + up[:, 1:-1] * w[1] + up[:, 2:] * w[2] + b


def hyena_filter(L, w1, b1, w2, b2, w3, freq):
    t = jnp.linspace(0.0, 1.0, L, dtype=F32)[:, None]
    bands = (FILTER_EMB - 1) // 2
    wpos = 2.0 * math.pi * jnp.arange(L, dtype=F32)[:, None] / L
    f = jnp.linspace(1e-4, bands - 1, bands, dtype=F32)[None, :]
    z = jnp.concatenate([t, jnp.cos(f * wpos), -jnp.sin(f * wpos)], axis=-1)
    fr = freq.astype(F32)
    h = jnp.sin(fr * (z @ w1.astype(F32) + b1.astype(F32)))
    h = jnp.sin(fr * (h @ w2.astype(F32) + b2.astype(F32)))
    h = (h @ w3.astype(F32)).reshape(L, 2, HY_WIDTH)
    max_decay = math.log(DECAY_TARGET) / FAST_DECAY_PCT
    min_decay = math.log(DECAY_TARGET) / SLOW_DECAY_PCT
    deltas = jnp.abs(jnp.linspace(min_decay, max_decay, HY_WIDTH, dtype=F32))
    h = h * jnp.exp(-t[:, :, None] * deltas)
    kern = jnp.concatenate([h[:, 0], jnp.zeros((1, HY_WIDTH), F32), h[1:, 1][::-1]], axis=0)
    return kern / jnp.sum(jnp.abs(kern), axis=0, keepdims=True)


def long_conv(z, kern, bias):
    L = z.shape[1]
    zf32 = z.astype(F32)
    zf = jnp.fft.rfft(zf32, n=2 * L, axis=1)
    kf = jnp.fft.rfft(kern, n=2 * L, axis=0)
    y = jnp.fft.irfft(zf * kf[None], n=2 * L, axis=1)[:, :L]
    return (y + zf32 * bias.astype(F32)).astype(z.dtype)


def cross_attention(h, mem, wq, wkv, wo):
    B, L, _ = h.shape
    M = mem.shape[1]
    q = (h @ wq).reshape(B, L, X_HEADS, X_HEAD_DIM)
    kv = (mem @ wkv).reshape(B, M, 2, X_HEADS, X_HEAD_DIM)
    s = jnp.einsum('bqhd,bkhd->bhqk', q, kv[:, :, 0]).astype(F32) * (X_HEAD_DIM ** -0.5)
    p = jax.nn.softmax(s, axis=-1)
    o = jnp.einsum('bhqk,bkhd->bqhd', p.astype(h.dtype), kv[:, :, 1]).reshape(B, L, D_MODEL)
    return o @ wo


def expert_choice_ffn(h, w_router, w_gate, w_up, w_down):
    B, L, D = h.shape
    T = B * L
    xt = h.reshape(T, D)
    cap = EC_CAPACITY_FACTOR * T // N_EXPERTS
    aff = jax.nn.softmax((xt @ w_router).astype(F32), axis=-1)
    g, idx = lax.top_k(aff.T, cap)
    xe = xt[idx]
    he = jax.nn.silu(jnp.einsum('ecd,edf->ecf', xe, w_gate)) * jnp.einsum('ecd,edf->ecf', xe, w_up)
    ye = jnp.einsum('ecf,efd->ecd', he, w_down) * g[..., None].astype(h.dtype)
    y = jnp.zeros((T, D), h.dtype).at[idx.reshape(-1)].add(ye.reshape(-1, D))
    return y.reshape(B, L, D)


def encoder_layer(x, mem, P, l):
    B, L, _ = x.shape
    pos = jnp.arange(L, dtype=F32)
    u = rmsnorm(x, P['norm_mix'][l])
    proj = u @ P['w_in'][l]
    o1 = ATT_QK_W
    o2 = o1 + ATT_QK_W
    o3 = o2 + ATT_V_W
    o4 = o3 + 3 * HY_WIDTH
    q, k, v, hy, gates = jnp.split(proj, [o1, o2, o3, o4], axis=-1)
    q = rope_partial(q.reshape(B, L, N_ATT_HEADS, 2, ATT_HEAD_DIM), pos)
    k = rope_partial(k.reshape(B, L, N_ATT_HEADS, 2, ATT_HEAD_DIM), pos)
    v = v.reshape(B, L, N_ATT_HEADS, ATT_V_DIM)
    lam_init = 0.8 - 0.6 * math.exp(-0.3 * l)
    lam = (jnp.exp(jnp.sum(P['lambda_q1'][l].astype(F32) * P['lambda_k1'][l].astype(F32)))
           - jnp.exp(jnp.sum(P['lambda_q2'][l].astype(F32) * P['lambda_k2'][l].astype(F32)))
           + lam_init)
    a = diff_attention(q, k, v, lam)
    a = rmsnorm(a, P['subln'][l]) * (1.0 - lam_init)
    a = a.reshape(B, L, ATT_V_W) @ P['w_br_attn'][l]
    hy = short_conv(hy, P['hy_conv_w'][l], P['hy_conv_b'][l])
    x0, x1, hv = jnp.split(hy, 3, axis=-1)
    kern = hyena_filter(L, P['filt_w1'][l], P['filt_b1'][l], P['filt_w2'][l],
                        P['filt_b2'][l], P['filt_w3'][l], P['filt_freq'][l])
    hz = long_conv(hv * x1, kern, P['filt_bias'][l]) * x0
    hz = hz @ P['w_br_hyena'][l]
    g = jax.nn.sigmoid(gates.astype(F32)).astype(x.dtype).reshape(B, L, N_BRANCH, D_MODEL)
    x = x + (g[:, :, 0] * a + g[:, :, 1] * hz) @ P['w_out'][l]
    x = x + cross_attention(rmsnorm(x, P['norm_x'][l]), rmsnorm(mem, P['norm_mem'][l]),
                            P['w_xq'][l], P['w_xkv'][l], P['w_xo'][l])
    x = x + expert_choice_ffn(rmsnorm(x, P['norm_ffn'][l]), P['w_router'][l],
                              P['w_exp_gate'][l], P['w_exp_up'][l], P['w_exp_down'][l])
    return x


def run_trunk(x, mem, P, norm_final):
    for l in range(DEPTH):
        x = encoder_layer(x, mem, P, l)
    return rmsnorm(x, norm_final)


def setup_inputs(seed: int = 0) -> dict:
    key = jax.random.key(seed)
    ks = iter(jax.random.split(key, 40))

    def nrm(shape, scale):
        return jax.random.normal(next(ks), shape, F32) * scale

    def gain(shape):
        return 1.0 + nrm(shape, 0.02)

    Dl = DEPTH
    D = D_MODEL
    return {
        'x_prompt': nrm((BATCH, SEQ, D), 1.0),
        'x_sample': nrm((DEC_BATCH, DEC_SEQ, D), 1.0),
        'mem_prompt': nrm((BATCH, N_MEM, D), 1.0),
        'mem_sample': nrm((DEC_BATCH, N_MEM, D), 1.0),
        'norm_mix': gain((Dl, D)),
        'w_in': nrm((Dl, D, IN_W), D ** -0.5),
        'lambda_q1': nrm((Dl, ATT_HEAD_DIM), 0.1),
        'lambda_k1': nrm((Dl, ATT_HEAD_DIM), 0.1),
        'lambda_q2': nrm((Dl, ATT_HEAD_DIM), 0.1),
        'lambda_k2': nrm((Dl, ATT_HEAD_DIM), 0.1),
        'subln': gain((Dl, ATT_V_DIM)),
        'w_br_attn': nrm((Dl, ATT_V_W, D), ATT_V_W ** -0.5),
        'hy_conv_w': nrm((Dl, 3, 3 * HY_WIDTH), 3 ** -0.5),
        'hy_conv_b': nrm((Dl, 3 * HY_WIDTH), 0.02),
        'filt_w1': nrm((Dl, FILTER_EMB, FILTER_HIDDEN), FILTER_EMB ** -0.5),
        'filt_b1': nrm((Dl, FILTER_HIDDEN), 0.1),
        'filt_w2': nrm((Dl, FILTER_HIDDEN, FILTER_HIDDEN), FILTER_HIDDEN ** -0.5),
        'filt_b2': nrm((Dl, FILTER_HIDDEN), 0.1),
        'filt_w3': nrm((Dl, FILTER_HIDDEN, 2 * HY_WIDTH), FILTER_HIDDEN ** -0.5),
        'filt_freq': gain((Dl, FILTER_HIDDEN)),
        'filt_bias': nrm((Dl, HY_WIDTH), 1.0),
        'w_br_hyena': nrm((Dl, HY_WIDTH, D), HY_WIDTH ** -0.5),
        'w_out': nrm((Dl, D, D), D ** -0.5),
        'norm_x': gain((Dl, D)),
        'norm_mem': gain((Dl, D)),
        'w_xq': nrm((Dl, D, D), D ** -0.5),
        'w_xkv': nrm((Dl, D, 2 * D), D ** -0.5),
        'w_xo': nrm((Dl, D, D), D ** -0.5),
        'norm_ffn': gain((Dl, D)),
        'w_router': nrm((Dl, D, N_EXPERTS), D ** -0.5),
        'w_exp_gate': nrm((Dl, N_EXPERTS, D, EXPERT_FF), D ** -0.5),
        'w_exp_up': nrm((Dl, N_EXPERTS, D, EXPERT_FF), D ** -0.5),
        'w_exp_down': nrm((Dl, N_EXPERTS, EXPERT_FF, D), EXPERT_FF ** -0.5),
        'norm_final': gain((D,)),
    }


def reference(x_prompt, x_sample, mem_prompt, mem_sample, norm_mix, w_in, lambda_q1, lambda_k1,
              lambda_q2, lambda_k2, subln, w_br_attn, hy_conv_w, hy_conv_b, filt_w1, filt_b1,
              filt_w2, filt_b2, filt_w3, filt_freq, filt_bias, w_br_hyena, w_out, norm_x, norm_mem,
              w_xq, w_xkv, w_xo, norm_ffn, w_router, w_exp_gate, w_exp_up, w_exp_down, norm_final):
    P = dict(norm_mix=norm_mix, w_in=w_in, lambda_q1=lambda_q1, lambda_k1=lambda_k1,
             lambda_q2=lambda_q2, lambda_k2=lambda_k2, subln=subln, w_br_attn=w_br_attn,
             hy_conv_w=hy_conv_w, hy_conv_b=hy_conv_b, filt_w1=filt_w1, filt_b1=filt_b1,
             filt_w2=filt_w2, filt_b2=filt_b2, filt_w3=filt_w3, filt_freq=filt_freq,
             filt_bias=filt_bias, w_br_hyena=w_br_hyena, w_out=w_out, norm_x=norm_x,
             norm_mem=norm_mem, w_xq=w_xq, w_xkv=w_xkv, w_xo=w_xo, norm_ffn=norm_ffn,
             w_router=w_router, w_exp_gate=w_exp_gate, w_exp_up=w_exp_up, w_exp_down=w_exp_down)
    y_prompt = run_trunk(x_prompt, mem_prompt, P, norm_final)
    y_sample = run_trunk(x_sample, mem_sample, P, norm_final)
    return (y_prompt, y_sample)
```

```python
import functools
import math

import jax
import jax.numpy as jnp
from jax import lax
from jax.experimental import pallas as pl
from jax.experimental.pallas import tpu as pltpu

F32 = jnp.float32
BF16 = jnp.bfloat16
I32 = jnp.int32

D_MODEL = 1024
N_ATT_HEADS = 4
ATT_HEAD_DIM = 64
ATT_V_DIM = 128
ROT_DIM = 16
ROPE_THETA = 500000.0
ATT_QK_W = 512
ATT_V_W = 512
HY_WIDTH = 512
FILTER_EMB = 33
FILTER_HIDDEN = 64
DECAY_TARGET = 1e-2
FAST_DECAY_PCT = 0.3
SLOW_DECAY_PCT = 1.5
IN_W = 5120
N_MEM = 256
X_HEADS = 4
X_HEAD_DIM = 256
N_EXPERTS = 16
EC_CAPACITY_FACTOR = 2
EXPERT_FF = 1024
EPS = 1e-6
LAM_INIT = 0.8 - 0.6 * math.exp(-0.3 * 0)

LANES = 128
VMEM_LIMIT_BYTES = 56 * 1024 * 1024


def _cparams(*sem):
    return pltpu.CompilerParams(dimension_semantics=sem, vmem_limit_bytes=VMEM_LIMIT_BYTES)


def _rms(x, g):
    return x * lax.rsqrt(jnp.mean(x * x, axis=-1, keepdims=True) + EPS) * g


def _inproj_kernel(x_ref, g_ref, w_ref, cos_ref, sa_ref, sb_ref, o_ref):
    u = _rms(x_ref[...], g_ref[...]).astype(BF16)
    n_col = IN_W // 512
    for j in range(n_col):
        acc = jnp.dot(u, w_ref[:, j * 512:(j + 1) * 512], preferred_element_type=F32)
        if j < 2:
            parts = []
            for h in range(4):
                t = acc[:, h * LANES:(h + 1) * LANES]
                r = (t * cos_ref[...] + pltpu.roll(t, 8, 1) * sa_ref[...]
                     + pltpu.roll(t, LANES - 8, 1) * sb_ref[...])
                parts.append(r)
            acc = jnp.concatenate(parts, axis=1)
        o_ref[:, j * 512:(j + 1) * 512] = acc.astype(BF16)


def _rope_tables(L):
    inv = ROPE_THETA ** (-jnp.arange(0, ROT_DIM, 2, dtype=F32) / ROT_DIM)
    pos = jnp.arange(L, dtype=F32)
    ang = pos[:, None] * inv[None, :]
    cos, sin = jnp.cos(ang), jnp.sin(ang)
    one = jnp.ones((L, 48), F32)
    zero8 = jnp.zeros((L, 8), F32)
    zero48 = jnp.zeros((L, 48), F32)
    c64 = jnp.concatenate([cos, cos, one], axis=1)
    sa64 = jnp.concatenate([zero8, sin, zero48], axis=1)
    sb64 = jnp.concatenate([-sin, zero8, zero48], axis=1)
    tile2 = lambda a: jnp.concatenate([a, a], axis=1)
    return tile2(c64), tile2(sa64), tile2(sb64)


def _inproj(xt, g, w_in_bf, L, tm=512):
    T = xt.shape[0]
    cos, sa, sb = _rope_tables(L)
    nl = L // tm
    tab = pl.BlockSpec((tm, LANES), lambda i: (i % nl, 0))
    return pl.pallas_call(
        _inproj_kernel,
        grid=(T // tm,),
        in_specs=[pl.BlockSpec((tm, D_MODEL), lambda i: (i, 0)),
                  pl.BlockSpec((1, D_MODEL), lambda i: (0, 0)),
                  pl.BlockSpec((D_MODEL, IN_W), lambda i: (0, 0)),
                  tab, tab, tab],
        out_specs=pl.BlockSpec((tm, IN_W), lambda i: (i, 0)),
        out_shape=jax.ShapeDtypeStruct((T, IN_W), BF16),
        compiler_params=_cparams("parallel"),
        name="inproj",
    )(xt, g.reshape(1, D_MODEL), w_in_bf, cos, sa, sb)


def _attn_kernel(lam_ref, q_ref, k_ref, v_ref, sg_ref, o_ref, *, seq, kc):
    tq = q_ref.shape[0]
    q = q_ref[...]
    lane = lax.broadcasted_iota(I32, q.shape, 1)
    zero = jnp.zeros_like(q)
    scale = jnp.asarray(ATT_HEAD_DIM ** -0.5, BF16)
    qq = jnp.concatenate([jnp.where(lane < ATT_HEAD_DIM, q, zero),
                          jnp.where(lane >= ATT_HEAD_DIM, q, zero)], axis=0) * scale

    def body(c, carry):
        m, l, acc = carry
        start = pl.multiple_of(c * kc, kc)
        kb = k_ref[pl.ds(start, kc), :]
        vb = v_ref[pl.ds(start, kc), :]
        s = lax.dot_general(qq, kb, (((1,), (1,)), ((), ())), preferred_element_type=F32)
        m_new = jnp.maximum(m, jnp.max(s, axis=1, keepdims=True))
        alpha = jnp.exp(m - m_new)
        p = jnp.exp(s - m_new)
        l = alpha * l + jnp.sum(p, axis=1, keepdims=True)
        acc = alpha * acc + jnp.dot(p.astype(BF16), vb, preferred_element_type=F32)
        return m_new, l, acc

    m0 = jnp.full((2 * tq, 1), -jnp.inf, F32)
    l0 = jnp.zeros((2 * tq, 1), F32)
    a0 = jnp.zeros((2 * tq, ATT_V_DIM), F32)
    _, l, acc = lax.fori_loop(0, seq // kc, body, (m0, l0, a0))
    o = acc / l
    a = o[:tq] - lam_ref[0] * o[tq:]
    a = _rms(a, sg_ref[...]) * (1.0 - LAM_INIT)
    o_ref[...] = a.astype(BF16)


def _diff_attention(proj, lam, subln, B, L, tq=128):
    T = B * L
    nq = L // tq
    kc = min(512, L)
    return pl.pallas_call(
        functools.partial(_attn_kernel, seq=L, kc=kc),
        grid=(B, N_ATT_HEADS, nq),
        in_specs=[pl.BlockSpec(memory_space=pltpu.SMEM),
                  pl.BlockSpec((tq, LANES), lambda b, h, i: (b * nq + i, h)),
                  pl.BlockSpec((L, LANES), lambda b, h, i: (b, 4 + h)),
                  pl.BlockSpec((L, LANES), lambda b, h, i: (b, 8 + h)),
                  pl.BlockSpec((1, ATT_V_DIM), lambda b, h, i: (0, 0))],
        out_specs=pl.BlockSpec((tq, ATT_V_DIM), lambda b, h, i: (b * nq + i, h)),
        out_shape=jax.ShapeDtypeStruct((T, ATT_V_W), BF16),
        compiler_params=_cparams("parallel", "parallel", "parallel"),
        name="diff_attn",
    )(lam.reshape(1), proj, proj, proj, subln.reshape(1, ATT_V_DIM))


def _hyprep_kernel(hy_ref, prev_ref, next_ref, w_ref, b_ref, fb_ref, zc_ref, x0_ref, zb_ref):
    i = pl.program_id(1)
    n = pl.num_programs(1)
    hy = hy_ref[...].astype(F32)
    tl = hy.shape[0]
    prev_row = jnp.where(i == 0, 0.0, prev_ref[15:16, :].astype(F32))
    next_row = jnp.where(i == n - 1, 0.0, next_ref[0:1, :].astype(F32))
    row = lax.broadcasted_iota(I32, hy.shape, 0)
    up = jnp.where(row == 0, prev_row, pltpu.roll(hy, 1, 0))
    dn = jnp.where(row == tl - 1, next_row, pltpu.roll(hy, tl - 1, 0))
    conv = up * w_ref[0:1, :] + hy * w_ref[1:2, :] + dn * w_ref[2:3, :] + b_ref[...]
    x0 = conv[:, :HY_WIDTH]
    x1 = conv[:, HY_WIDTH:2 * HY_WIDTH]
    hv = conv[:, 2 * HY_WIDTH:]
    z = hv * x1
    for c in range(HY_WIDTH // LANES):
        zc_ref[0, c] = z[:, c * LANES:(c + 1) * LANES].astype(zc_ref.dtype)
    x0_ref[...] = x0.astype(x0_ref.dtype)
    zb_ref[...] = (z * fb_ref[...] * x0).astype(zb_ref.dtype)


def _hyena_prep(proj, conv_w, conv_b, filt_bias, B, L, tl=512):
    T = B * L
    nl = L // tl
    hw = 3 * HY_WIDTH
    rb = tl // 16
    nrb = T // 16
    return pl.pallas_call(
        _hyprep_kernel,
        grid=(B, nl),
        in_specs=[pl.BlockSpec((tl, hw), lambda b, i: (b * nl + i, 1)),
                  pl.BlockSpec((16, hw), lambda b, i: (jnp.maximum((b * nl + i) * rb - 1, 0), 1)),
                  pl.BlockSpec((16, hw), lambda b, i: (jnp.minimum((b * nl + i + 1) * rb, nrb - 1), 1)),
                  pl.BlockSpec((3, hw), lambda b, i: (0, 0)),
                  pl.BlockSpec((1, hw), lambda b, i: (0, 0)),
                  pl.BlockSpec((1, HY_WIDTH), lambda b, i: (0, 0))],
        out_specs=[pl.BlockSpec((1, HY_WIDTH // LANES, tl, LANES), lambda b, i: (b, 0, i, 0)),
                   pl.BlockSpec((tl, HY_WIDTH), lambda b, i: (b * nl + i, 0)),
                   pl.BlockSpec((tl, HY_WIDTH), lambda b, i: (b * nl + i, 0))],
        out_shape=[jax.ShapeDtypeStruct((B, HY_WIDTH // LANES, L, LANES), F32),
                   jax.ShapeDtypeStruct((T, HY_WIDTH), BF16),
                   jax.ShapeDtypeStruct((T, HY_WIDTH), BF16)],
        compiler_params=_cparams("parallel", "parallel"),
        name="hyena_prep",
    )(proj, proj, proj, conv_w, conv_b.reshape(1, hw), filt_bias.reshape(1, HY_WIDTH))


def _filter_jnp(L, w1, b1, w2, b2, w3, freq):
    t = jnp.linspace(0.0, 1.0, L, dtype=F32)[:, None]
    bands = (FILTER_EMB - 1) // 2
    wpos = 2.0 * math.pi * jnp.arange(L, dtype=F32)[:, None] / L
    f = jnp.linspace(1e-4, bands - 1, bands, dtype=F32)[None, :]
    z = jnp.concatenate([t, jnp.cos(f * wpos), -jnp.sin(f * wpos)], axis=-1)
    h = jnp.sin(freq * (z @ w1 + b1))
    h = jnp.sin(freq * (h @ w2 + b2))
    h = (h @ w3).reshape(L, 2, HY_WIDTH)
    max_decay = math.log(DECAY_TARGET) / FAST_DECAY_PCT
    min_decay = math.log(DECAY_TARGET) / SLOW_DECAY_PCT
    deltas = jnp.abs(jnp.linspace(min_decay, max_decay, HY_WIDTH, dtype=F32))
    h = h * jnp.exp(-t[:, :, None] * deltas)
    kern = jnp.concatenate([h[:, 0], jnp.zeros((1, HY_WIDTH), F32), h[1:, 1][::-1]], axis=0)
    return kern / jnp.sum(jnp.abs(kern), axis=0, keepdims=True)


def _long_conv_jnp(zc, kern):
    B, nc, L, _ = zc.shape
    z = zc.transpose(0, 2, 1, 3).reshape(B, L, HY_WIDTH)
    zf = jnp.fft.rfft(z, n=2 * L, axis=1)
    kf = jnp.fft.rfft(kern, n=2 * L, axis=0)
    y = jnp.fft.irfft(zf * kf[None], n=2 * L, axis=1)[:, :L]
    return y.reshape(B, L, nc, LANES).transpose(0, 2, 1, 3)


def _merge_kernel(x_ref, a_ref, y_ref, x0_ref, zb_ref, g0_ref, g1_ref, wa_ref, wh_ref, wo_ref, o_ref):
    a = jnp.dot(a_ref[...], wa_ref[...], preferred_element_type=F32)
    x0 = x0_ref[...].astype(F32)
    y = jnp.concatenate([y_ref[0, c].astype(F32) for c in range(HY_WIDTH // LANES)], axis=1)
    hz = (y * x0 + zb_ref[...].astype(F32)).astype(BF16)
    hzp = jnp.dot(hz, wh_ref[...], preferred_element_type=F32)
    g0 = jax.nn.sigmoid(g0_ref[...].astype(F32))
    g1 = jax.nn.sigmoid(g1_ref[...].astype(F32))
    m = (g0 * a + g1 * hzp).astype(BF16)
    o_ref[...] = x_ref[...] + jnp.dot(m, wo_ref[...], preferred_element_type=F32)


def _merge(xt, a, yc, x0, zb, proj, wa, wh, wo, B, L, tm=512):
    T = B * L
    nl = L // tm
    full = lambda r, c: pl.BlockSpec((r, c), lambda b, i: (0, 0))
    tok = lambda c, cb=0: pl.BlockSpec((tm, c), lambda b, i, cb=cb: (b * nl + i, cb))
    return pl.pallas_call(
        _merge_kernel,
        grid=(B, nl),
        in_specs=[tok(D_MODEL), tok(ATT_V_W),
                  pl.BlockSpec((1, HY_WIDTH // LANES, tm, LANES), lambda b, i: (b, 0, i, 0)),
                  tok(HY_WIDTH), tok(HY_WIDTH), tok(D_MODEL, 3), tok(D_MODEL, 4),
                  full(ATT_V_W, D_MODEL), full(HY_WIDTH, D_MODEL), full(D_MODEL, D_MODEL)],
        out_specs=tok(D_MODEL),
        out_shape=jax.ShapeDtypeStruct((T, D_MODEL), F32),
        compiler_params=_cparams("parallel", "parallel"),
        name="merge",
    )(xt, a, yc, x0, zb, proj, proj, wa, wh, wo)


def _memkv_kernel(m_ref, g_ref, w_ref, o_ref):
    u = _rms(m_ref[...], g_ref[...]).astype(BF16)
    o_ref[...] = jnp.dot(u, w_ref[...], preferred_element_type=F32).astype(BF16)


def _mem_kv(mem_t, g, w_xkv_bf):
    R = mem_t.shape[0]
    return pl.pallas_call(
        _memkv_kernel,
        grid=(R // N_MEM,),
        in_specs=[pl.BlockSpec((N_MEM, D_MODEL), lambda i: (i, 0)),
                  pl.BlockSpec((1, D_MODEL), lambda i: (0, 0)),
                  pl.BlockSpec((D_MODEL, 2 * D_MODEL), lambda i: (0, 0))],
        out_specs=pl.BlockSpec((N_MEM, 2 * D_MODEL), lambda i: (i, 0)),
        out_shape=jax.ShapeDtypeStruct((R, 2 * D_MODEL), BF16),
        compiler_params=_cparams("parallel"),
        name="mem_kv",
    )(mem_t, g.reshape(1, D_MODEL), w_xkv_bf)


def _xattn_kernel(x_ref, kv_ref, gx_ref, wq_ref, wo_ref, gf_ref, wr_ref, x2_ref, h_ref, aff_ref):
    x = x_ref[...]
    u = _rms(x, gx_ref[...]).astype(BF16)
    q = jnp.dot(u, wq_ref[...], preferred_element_type=F32)
    q = (q * (X_HEAD_DIM ** -0.5)).astype(BF16)
    outs = []
    for h in range(X_HEADS):
        qh = q[:, h * X_HEAD_DIM:(h + 1) * X_HEAD_DIM]
        kh = kv_ref[:, h * X_HEAD_DIM:(h + 1) * X_HEAD_DIM]
        vh = kv_ref[:, D_MODEL + h * X_HEAD_DIM:D_MODEL + (h + 1) * X_HEAD_DIM]
        s = lax.dot_general(qh, kh, (((1,), (1,)), ((), ())), preferred_element_type=F32)
        s = s - jnp.max(s, axis=1, keepdims=True)
        p = jnp.exp(s)
        p = p / jnp.sum(p, axis=1, keepdims=True)
        outs.append(jnp.dot(p.astype(BF16), vh, preferred_element_type=F32))
    o = jnp.concatenate(outs, axis=1).astype(BF16)
    x2 = x + jnp.dot(o, wo_ref[...], preferred_element_type=F32)
    x2_ref[...] = x2
    hf = _rms(x2, gf_ref[...]).astype(BF16)
    h_ref[...] = hf
    lg = lax.dot_general(wr_ref[...], hf, (((1,), (1,)), ((), ())), preferred_element_type=F32)
    lg = lg - jnp.max(lg, axis=0, keepdims=True)
    e = jnp.exp(lg)
    aff_ref[...] = e / jnp.sum(e, axis=0, keepdims=True)


def _cross_attn(x1, kv, gx, wq, wo, gf, wr_t, B, L, tm=512):
    T = B * L
    nl = L // tm
    full = lambda r, c: pl.BlockSpec((r, c), lambda b, i: (0, 0))
    tok = lambda c: pl.BlockSpec((tm, c), lambda b, i: (b * nl + i, 0))
    return pl.pallas_call(
        _xattn_kernel,
        grid=(B, nl),
        in_specs=[tok(D_MODEL),
                  pl.BlockSpec((N_MEM, 2 * D_MODEL), lambda b, i: (b, 0)),
                  full(1, D_MODEL), full(D_MODEL, D_MODEL), full(D_MODEL, D_MODEL),
                  full(1, D_MODEL), full(N_EXPERTS, D_MODEL)],
        out_specs=[tok(D_MODEL), tok(D_MODEL),
                   pl.BlockSpec((N_EXPERTS, tm), lambda b, i: (0, b * nl + i))],
        out_shape=[jax.ShapeDtypeStruct((T, D_MODEL), F32),
                   jax.ShapeDtypeStruct((T, D_MODEL), BF16),
                   jax.ShapeDtypeStruct((N_EXPERTS, T), F32)],
        compiler_params=_cparams("parallel", "parallel"),
        name="cross_attn",
    )(x1, kv, gx.reshape(1, D_MODEL), wq, wo, gf.reshape(1, D_MODEL), wr_t)


def _select_kernel(aff_ref, gate_ref, *, cap):
    aff = aff_ref[...]
    bits = pltpu.bitcast(aff, I32)
    T = aff.shape[1]

    def count(mask):
        return jnp.sum(mask.astype(I32), axis=1, keepdims=True)

    def vbody(k, thr):
        cand = thr | (jnp.int32(1) << (30 - k))
        return jnp.where(count(bits >= cand) >= cap, cand, thr)

    thr = lax.fori_loop(0, 31, vbody, jnp.zeros((N_EXPERTS, 1), I32))
    gt = bits > thr
    eq = bits == thr
    need = cap - count(gt)
    tok = lax.broadcasted_iota(I32, aff.shape, 1)
    nbit = int(T).bit_length()

    def jbody(k, j):
        cand = j + (jnp.int32(1) << (nbit - 1 - k))
        return jnp.where(count(eq & (tok < cand)) <= need, cand, j)

    j = lax.fori_loop(0, nbit, jbody, jnp.zeros((N_EXPERTS, 1), I32))
    sel = gt | (eq & (tok < j))
    gate_ref[...] = jnp.where(sel, aff, 0.0)


def _select(aff_t, cap):
    E, T = aff_t.shape
    return pl.pallas_call(
        functools.partial(_select_kernel, cap=cap),
        grid=(1,),
        in_specs=[pl.BlockSpec((E, T), lambda i: (0, 0))],
        out_specs=pl.BlockSpec((E, T), lambda i: (0, 0)),
        out_shape=jax.ShapeDtypeStruct((E, T), F32),
        compiler_params=_cparams("arbitrary"),
        name="ec_select",
    )(aff_t)


def _ffn_kernel(h_ref, gate_ref, x_ref, wg_ref, wu_ref, wd_ref, gn_ref, o_ref, acc_ref):
    e = pl.program_id(1)

    @pl.when(e == 0)
    def _():
        acc_ref[...] = jnp.zeros_like(acc_ref)

    h = h_ref[...]
    lane = lax.broadcasted_iota(I32, gate_ref.shape, 1)
    g = jnp.sum(jnp.where(lane == e, gate_ref[...], 0.0), axis=1, keepdims=True)
    half = EXPERT_FF // 2
    y = jnp.zeros(acc_ref.shape, F32)
    for c in range(2):
        sl = slice(c * half, (c + 1) * half)
        a = jnp.dot(h, wg_ref[0, :, sl], preferred_element_type=F32)
        b = jnp.dot(h, wu_ref[0, :, sl], preferred_element_type=F32)
        he = (jax.nn.silu(a) * b).astype(BF16)
        y = y + jnp.dot(he, wd_ref[0, sl, :], preferred_element_type=F32)
    acc_ref[...] += y * g

    @pl.when(e == pl.num_programs(1) - 1)
    def _():
        o_ref[...] = _rms(x_ref[...] + acc_ref[...], gn_ref[...])


def _expert_ffn(h, gate_te, x2, wg, wu, wd, gn, tm=1024):
    T = h.shape[0]
    tok = lambda c: pl.BlockSpec((tm, c), lambda i, e: (i, 0))
    wspec = lambda r, c: pl.BlockSpec((1, r, c), lambda i, e: (e, 0, 0))
    return pl.pallas_call(
        _ffn_kernel,
        grid=(T // tm, N_EXPERTS),
        in_specs=[tok(D_MODEL), tok(N_EXPERTS), tok(D_MODEL),
                  wspec(D_MODEL, EXPERT_FF), wspec(D_MODEL, EXPERT_FF), wspec(EXPERT_FF, D_MODEL),
                  pl.BlockSpec((1, D_MODEL), lambda i, e: (0, 0))],
        out_specs=tok(D_MODEL),
        out_shape=jax.ShapeDtypeStruct((T, D_MODEL), F32),
        scratch_shapes=[pltpu.VMEM((tm, D_MODEL), F32)],
        compiler_params=_cparams("parallel", "arbitrary"),
        name="expert_ffn",
    )(h, gate_te, x2, wg, wu, wd, gn.reshape(1, D_MODEL))


def _run_group(x, mem, W):
    B, L, _ = x.shape
    T = B * L
    xt = x.reshape(T, D_MODEL)
    proj = _inproj(xt, W['norm_mix'], W['w_in'], L)
    a = _diff_attention(proj, W['lam'], W['subln'], B, L)
    zc, x0, zb = _hyena_prep(proj, W['hy_conv_w'], W['hy_conv_b'], W['filt_bias'], B, L)
    kern = _filter_jnp(L, W['filt_w1'], W['filt_b1'], W['filt_w2'], W['filt_b2'], W['filt_w3'],
                       W['filt_freq'])
    yc = _long_conv_jnp(zc, kern)
    x1 = _merge(xt, a, yc, x0, zb, proj, W['w_br_attn'], W['w_br_hyena'], W['w_out'], B, L)
    kv = _mem_kv(mem.reshape(B * N_MEM, D_MODEL), W['norm_mem'], W['w_xkv'])
    x2, hf, aff = _cross_attn(x1, kv, W['norm_x'], W['w_xq'], W['w_xo'], W['norm_ffn'],
                              W['w_router_t'], B, L)
    gate = _select(aff, EC_CAPACITY_FACTOR * T // N_EXPERTS)
    y = _expert_ffn(hf, gate.T, x2, W['w_exp_gate'], W['w_exp_up'], W['w_exp_down'], W['norm_final'])
    return y.reshape(B, L, D_MODEL)


def kernel(x_prompt, x_sample, mem_prompt, mem_sample, norm_mix, w_in, lambda_q1, lambda_k1, lambda_q2, lambda_k2, subln, w_br_attn, hy_conv_w, hy_conv_b, filt_w1, filt_b1, filt_w2, filt_b2, filt_w3, filt_freq, filt_bias, w_br_hyena, w_out, norm_x, norm_mem, w_xq, w_xkv, w_xo, norm_ffn, w_router, w_exp_gate, w_exp_up, w_exp_down, norm_final):
    l = 0
    lam = (jnp.exp(jnp.sum(lambda_q1[l] * lambda_k1[l])) - jnp.exp(jnp.sum(lambda_q2[l] * lambda_k2[l]))
           + LAM_INIT).astype(F32)
    W = dict(
        norm_mix=norm_mix[l], w_in=w_in[l].astype(BF16), lam=lam, subln=subln[l],
        w_br_attn=w_br_attn[l].astype(BF16), hy_conv_w=hy_conv_w[l], hy_conv_b=hy_conv_b[l],
        filt_w1=filt_w1[l], filt_b1=filt_b1[l], filt_w2=filt_w2[l], filt_b2=filt_b2[l],
        filt_w3=filt_w3[l], filt_freq=filt_freq[l], filt_bias=filt_bias[l],
        w_br_hyena=w_br_hyena[l].astype(BF16), w_out=w_out[l].astype(BF16),
        norm_x=norm_x[l], norm_mem=norm_mem[l], w_xq=w_xq[l].astype(BF16),
        w_xkv=w_xkv[l].astype(BF16), w_xo=w_xo[l].astype(BF16), norm_ffn=norm_ffn[l],
        w_router_t=w_router[l].T.astype(BF16), w_exp_gate=w_exp_gate[l].astype(BF16),
        w_exp_up=w_exp_up[l].astype(BF16), w_exp_down=w_exp_down[l].astype(BF16),
        norm_final=norm_final)
    return (_run_group(x_prompt, mem_prompt, W), _run_group(x_sample, mem_sample, W))
```

```python
import functools
import math

import jax
import jax.numpy as jnp
from jax import lax
from jax.experimental import pallas as pl
from jax.experimental.pallas import tpu as pltpu

F32 = jnp.float32
BF16 = jnp.bfloat16
I32 = jnp.int32

D_MODEL = 1024
N_ATT_HEADS = 4
ATT_HEAD_DIM = 64
ATT_V_DIM = 128
ROT_DIM = 16
ROPE_THETA = 500000.0
ATT_QK_W = 512
ATT_V_W = 512
HY_WIDTH = 512
FILTER_EMB = 33
FILTER_HIDDEN = 64
DECAY_TARGET = 1e-2
FAST_DECAY_PCT = 0.3
SLOW_DECAY_PCT = 1.5
IN_W = 5120
N_MEM = 256
X_HEADS = 4
X_HEAD_DIM = 256
N_EXPERTS = 16
EC_CAPACITY_FACTOR = 2
EXPERT_FF = 1024
EPS = 1e-6
LAM_INIT = 0.8 - 0.6 * math.exp(-0.3 * 0)

LANES = 128
VMEM_LIMIT_BYTES = 56 * 1024 * 1024


def _cparams(*sem):
    return pltpu.CompilerParams(dimension_semantics=sem, vmem_limit_bytes=VMEM_LIMIT_BYTES)


def _rms(x, g):
    return x * lax.rsqrt(jnp.mean(x * x, axis=-1, keepdims=True) + EPS) * g


def _inproj_kernel(x_ref, g_ref, w_ref, cos_ref, sa_ref, sb_ref, o_ref):
    u = _rms(x_ref[...], g_ref[...]).astype(BF16)
    n_col = IN_W // 512
    for j in range(n_col):
        acc = jnp.dot(u, w_ref[:, j * 512:(j + 1) * 512], preferred_element_type=F32)
        if j < 2:
            parts = []
            for h in range(4):
                t = acc[:, h * LANES:(h + 1) * LANES]
                r = (t * cos_ref[...] + pltpu.roll(t, 8, 1) * sa_ref[...]
                     + pltpu.roll(t, LANES - 8, 1) * sb_ref[...])
                parts.append(r)
            acc = jnp.concatenate(parts, axis=1)
        o_ref[:, j * 512:(j + 1) * 512] = acc.astype(BF16)


def _rope_tables(L):
    inv = ROPE_THETA ** (-jnp.arange(0, ROT_DIM, 2, dtype=F32) / ROT_DIM)
    pos = jnp.arange(L, dtype=F32)
    ang = pos[:, None] * inv[None, :]
    cos, sin = jnp.cos(ang), jnp.sin(ang)
    one = jnp.ones((L, 48), F32)
    zero8 = jnp.zeros((L, 8), F32)
    zero48 = jnp.zeros((L, 48), F32)
    c64 = jnp.concatenate([cos, cos, one], axis=1)
    sa64 = jnp.concatenate([zero8, sin, zero48], axis=1)
    sb64 = jnp.concatenate([-sin, zero8, zero48], axis=1)
    tile2 = lambda a: jnp.concatenate([a, a], axis=1)
    return tile2(c64), tile2(sa64), tile2(sb64)


def _inproj(xt, g, w_in_bf, L, tm=512):
    T = xt.shape[0]
    cos, sa, sb = _rope_tables(L)
    nl = L // tm
    tab = pl.BlockSpec((tm, LANES), lambda i: (i % nl, 0))
    return pl.pallas_call(
        _inproj_kernel,
        grid=(T // tm,),
        in_specs=[pl.BlockSpec((tm, D_MODEL), lambda i: (i, 0)),
                  pl.BlockSpec((1, D_MODEL), lambda i: (0, 0)),
                  pl.BlockSpec((D_MODEL, IN_W), lambda i: (0, 0)),
                  tab, tab, tab],
        out_specs=pl.BlockSpec((tm, IN_W), lambda i: (i, 0)),
        out_shape=jax.ShapeDtypeStruct((T, IN_W), BF16),
        compiler_params=_cparams("parallel"),
        name="inproj",
    )(xt, g.reshape(1, D_MODEL), w_in_bf, cos, sa, sb)


def _attn_kernel(lam_ref, q_ref, k_ref, v_ref, sg_ref, o_ref, *, seq, kc):
    tq = q_ref.shape[0]
    q = q_ref[...]
    lane = lax.broadcasted_iota(I32, q.shape, 1)
    zero = jnp.zeros_like(q)
    scale = jnp.asarray(ATT_HEAD_DIM ** -0.5, BF16)
    qq = jnp.concatenate([jnp.where(lane < ATT_HEAD_DIM, q, zero),
                          jnp.where(lane >= ATT_HEAD_DIM, q, zero)], axis=0) * scale

    def body(c, carry):
        m, l, acc = carry
        start = pl.multiple_of(c * kc, kc)
        kb = k_ref[pl.ds(start, kc), :]
        vb = v_ref[pl.ds(start, kc), :]
        s = lax.dot_general(qq, kb, (((1,), (1,)), ((), ())), preferred_element_type=F32)
        m_new = jnp.maximum(m, jnp.max(s, axis=1, keepdims=True))
        alpha = jnp.exp(m - m_new)
        p = jnp.exp(s - m_new)
        l = alpha * l + jnp.sum(p, axis=1, keepdims=True)
        acc = alpha * acc + jnp.dot(p.astype(BF16), vb, preferred_element_type=F32)
        return m_new, l, acc

    m0 = jnp.full((2 * tq, 1), -jnp.inf, F32)
    l0 = jnp.zeros((2 * tq, 1), F32)
    a0 = jnp.zeros((2 * tq, ATT_V_DIM), F32)
    _, l, acc = lax.fori_loop(0, seq // kc, body, (m0, l0, a0))
    o = acc / l
    a = o[:tq] - lam_ref[0] * o[tq:]
    a = _rms(a, sg_ref[...]) * (1.0 - LAM_INIT)
    o_ref[...] = a.astype(BF16)


def _diff_attention(proj, lam, subln, B, L, tq=128):
    T = B * L
    nq = L // tq
    kc = min(512, L)
    return pl.pallas_call(
        functools.partial(_attn_kernel, seq=L, kc=kc),
        grid=(B, N_ATT_HEADS, nq),
        in_specs=[pl.BlockSpec(memory_space=pltpu.SMEM),
                  pl.BlockSpec((tq, LANES), lambda b, h, i: (b * nq + i, h)),
                  pl.BlockSpec((L, LANES), lambda b, h, i: (b, 4 + h)),
                  pl.BlockSpec((L, LANES), lambda b, h, i: (b, 8 + h)),
                  pl.BlockSpec((1, ATT_V_DIM), lambda b, h, i: (0, 0))],
        out_specs=pl.BlockSpec((tq, ATT_V_DIM), lambda b, h, i: (b * nq + i, h)),
        out_shape=jax.ShapeDtypeStruct((T, ATT_V_W), BF16),
        compiler_params=_cparams("parallel", "parallel", "parallel"),
        name="diff_attn",
    )(lam.reshape(1), proj, proj, proj, subln.reshape(1, ATT_V_DIM))


def _hyprep_kernel(hy_ref, prev_ref, next_ref, w_ref, b_ref, fb_ref, zc_ref, x0_ref, zb_ref):
    i = pl.program_id(1)
    n = pl.num_programs(1)
    hy = hy_ref[...].astype(F32)
    tl = hy.shape[0]
    prev_row = jnp.where(i == 0, 0.0, prev_ref[15:16, :].astype(F32))
    next_row = jnp.where(i == n - 1, 0.0, next_ref[0:1, :].astype(F32))
    row = lax.broadcasted_iota(I32, hy.shape, 0)
    up = jnp.where(row == 0, prev_row, pltpu.roll(hy, 1, 0))
    dn = jnp.where(row == tl - 1, next_row, pltpu.roll(hy, tl - 1, 0))
    conv = up * w_ref[0:1, :] + hy * w_ref[1:2, :] + dn * w_ref[2:3, :] + b_ref[...]
    x0 = conv[:, :HY_WIDTH]
    x1 = conv[:, HY_WIDTH:2 * HY_WIDTH]
    hv = conv[:, 2 * HY_WIDTH:]
    z = hv * x1
    for c in range(HY_WIDTH // LANES):
        zc_ref[0, c] = z[:, c * LANES:(c + 1) * LANES].astype(zc_ref.dtype)
    x0_ref[...] = x0.astype(x0_ref.dtype)
    zb_ref[...] = (z * fb_ref[...] * x0).astype(zb_ref.dtype)


def _hyena_prep(proj, conv_w, conv_b, filt_bias, B, L, tl=512):
    T = B * L
    nl = L // tl
    hw = 3 * HY_WIDTH
    rb = tl // 16
    nrb = T // 16
    return pl.pallas_call(
        _hyprep_kernel,
        grid=(B, nl),
        in_specs=[pl.BlockSpec((tl, hw), lambda b, i: (b * nl + i, 1)),
                  pl.BlockSpec((16, hw), lambda b, i: (jnp.maximum((b * nl + i) * rb - 1, 0), 1)),
                  pl.BlockSpec((16, hw), lambda b, i: (jnp.minimum((b * nl + i + 1) * rb, nrb - 1), 1)),
                  pl.BlockSpec((3, hw), lambda b, i: (0, 0)),
                  pl.BlockSpec((1, hw), lambda b, i: (0, 0)),
                  pl.BlockSpec((1, HY_WIDTH), lambda b, i: (0, 0))],
        out_specs=[pl.BlockSpec((1, HY_WIDTH // LANES, tl, LANES), lambda b, i: (b, 0, i, 0)),
                   pl.BlockSpec((tl, HY_WIDTH), lambda b, i: (b * nl + i, 0)),
                   pl.BlockSpec((tl, HY_WIDTH), lambda b, i: (b * nl + i, 0))],
        out_shape=[jax.ShapeDtypeStruct((B, HY_WIDTH // LANES, L, LANES), BF16),
                   jax.ShapeDtypeStruct((T, HY_WIDTH), BF16),
                   jax.ShapeDtypeStruct((T, HY_WIDTH), BF16)],
        compiler_params=_cparams("parallel", "parallel"),
        name="hyena_prep",
    )(proj, proj, proj, conv_w, conv_b.reshape(1, hw), filt_bias.reshape(1, HY_WIDTH))


def _fft_dims(L):
    n = 2 * L
    n1 = 1 << ((n.bit_length() - 1 + 1) // 2)
    n2 = n // n1
    A = n1 // 2
    K1 = n1 // 2 + 1
    K1p = -(-K1 // 8) * 8
    KP = -(-2 * K1p // LANES) * LANES
    assert n1 * n2 == n and A * n2 == L and n2 % 8 == 0 and A % 16 == 0
    return n, n1, n2, A, K1, K1p, KP


def _dft_tables(L):
    n, n1, n2, A, K1, K1p, KP = _fft_dims(L)
    b = jnp.arange(n2, dtype=I32)
    a = jnp.arange(A, dtype=I32)
    k1 = jnp.arange(K1p, dtype=I32)
    t = a[None, :] * n2 + b[:, None]
    m = (k1[None, :, None] * t[:, None, :]) % n
    th = m.astype(F32) * (2.0 * math.pi / n)
    valid = (k1 < K1).astype(F32)[None, :, None]
    c, s = jnp.cos(th) * valid, jnp.sin(th) * valid
    g_fwd = jnp.concatenate([c, -s], axis=1).astype(BF16)
    w = jnp.where((k1 == 0) | (k1 == n1 // 2), 1.0, 2.0)[None, :, None] / n
    ci = jnp.transpose(c * w, (0, 2, 1))
    si = jnp.transpose(-s * w, (0, 2, 1))
    pad = jnp.zeros((n2, A, KP - 2 * K1p), F32)
    g_inv = jnp.concatenate([ci, si, pad], axis=2).astype(BF16)
    k2 = jnp.arange(n2, dtype=I32)
    ph = ((k2[:, None] * b[None, :]) % n2).astype(F32) * (2.0 * math.pi / n2)
    c2, s2 = jnp.cos(ph), jnp.sin(ph)
    f2 = jnp.concatenate([jnp.concatenate([c2, s2], 1), jnp.concatenate([-s2, c2], 1)], 0).astype(BF16)
    f2i = jnp.concatenate([jnp.concatenate([c2, -s2], 1), jnp.concatenate([s2, c2], 1)], 0).astype(BF16)
    return g_fwd, g_inv, f2, f2i


def _fft_rows_fwd(x_ref, g_ref, yf_ref, *, nb, n2, K1p):
    def body(b, _):
        off = pl.multiple_of(b * LANES, LANES)
        xb = jnp.concatenate([x_ref[i, :, pl.ds(off, LANES)] for i in range(nb)], axis=1)
        y = jnp.dot(g_ref[b], xb, preferred_element_type=F32)
        for i in range(nb):
            for r in range(2 * K1p // 8):
                row = pl.multiple_of((r * n2 + b) * 8, 8)
                yf_ref[i, pl.ds(row, 8), :] = y[r * 8:(r + 1) * 8, i * LANES:(i + 1) * LANES]
        return 0
    lax.fori_loop(0, n2, body, 0)


def _fft_cols_fwd(yf_ref, f2_ref, k1, *, nb, n2, K1p):
    kt, ks = k1 // 8, k1 % 8
    parts = []
    for i in range(nb):
        re = yf_ref[i, pl.ds(kt * (n2 * 8) + ks, n2, stride=8), :]
        im = yf_ref[i, pl.ds((K1p // 8 + kt) * (n2 * 8) + ks, n2, stride=8), :]
        parts.append(jnp.concatenate([re, im], axis=0))
    yk = jnp.concatenate(parts, axis=1).astype(BF16)
    return jnp.dot(f2_ref[...], yk, preferred_element_type=F32)


def _kspec_kernel(fb_ref, l1_ref, g_ref, f2_ref, kf_ref, yf_ref, *, n2, K1p):
    _fft_rows_fwd(fb_ref, g_ref, yf_ref, nb=2, n2=n2, K1p=K1p)
    inv = 1.0 / l1_ref[...]

    def body(k1, _):
        z = _fft_cols_fwd(yf_ref, f2_ref, k1, nb=2, n2=n2, K1p=K1p)
        zf, zb = z[:, :LANES], z[:, LANES:]
        kr = (zf[:n2] + zb[:n2]) * inv
        ki = (zf[n2:] - zb[n2:]) * inv
        kf_ref[k1] = jnp.concatenate([kr, ki], axis=0).astype(kf_ref.dtype)
        return 0
    lax.fori_loop(0, K1p, body, 0)


def _lconv_kernel(x_ref, g_ref, f2_ref, f2i_ref, kf_ref, gi_ref, o_ref, yf_ref, wf_ref, *,
                  nb, n2, K1p, KP):
    _fft_rows_fwd(x_ref, g_ref, yf_ref, nb=nb, n2=n2, K1p=K1p)

    def kbody(k1, _):
        z = _fft_cols_fwd(yf_ref, f2_ref, k1, nb=nb, n2=n2, K1p=K1p)
        kf = kf_ref[k1].astype(F32)
        kr = jnp.concatenate([kf[:n2]] * nb, axis=1)
        ki = jnp.concatenate([kf[n2:]] * nb, axis=1)
        zr, zi = z[:n2], z[n2:]
        p = jnp.concatenate([zr * kr - zi * ki, zr * ki + zi * kr], axis=0).astype(BF16)
        v = jnp.dot(f2i_ref[...], p, preferred_element_type=F32)
        for i in range(nb):
            for r in range(2 * n2 // 8):
                row = pl.multiple_of((r * K1p + k1) * 8, 8)
                wf_ref[i, pl.ds(row, 8), :] = v[r * 8:(r + 1) * 8, i * LANES:(i + 1) * LANES]
        return 0
    lax.fori_loop(0, K1p, kbody, 0)

    def bbody(b, _):
        bt, bs = b // 8, b % 8
        parts = []
        for i in range(nb):
            re = wf_ref[i, pl.ds(bt * (K1p * 8) + bs, K1p, stride=8), :]
            im = wf_ref[i, pl.ds((n2 // 8 + bt) * (K1p * 8) + bs, K1p, stride=8), :]
            parts.append(jnp.concatenate([re, im, jnp.zeros((KP - 2 * K1p, LANES), F32)], axis=0))
        vb = jnp.concatenate(parts, axis=1).astype(BF16)
        ob = jnp.dot(gi_ref[b], vb, preferred_element_type=F32)
        off = pl.multiple_of(b * LANES, LANES)
        for i in range(nb):
            o_ref[i, :, pl.ds(off, LANES)] = ob[:, i * LANES:(i + 1) * LANES].astype(o_ref.dtype)
        return 0
    lax.fori_loop(0, n2, bbody, 0)


def _filter_kernel(w1_ref, b1_ref, w2_ref, b2_ref, w3_ref, fr_ref, fb_ref, l1_ref, *, seq):
    i = pl.program_id(0)
    tl = fb_ref.shape[2]
    bands = (FILTER_EMB - 1) // 2
    pos = (i * tl + lax.broadcasted_iota(I32, (tl, 1), 0)).astype(F32)
    t = pos * (1.0 / (seq - 1))
    wpos = pos * (2.0 * math.pi / seq)
    lane = lax.broadcasted_iota(I32, (1, LANES), 1)
    band = jnp.where(lane <= bands, lane - 1, lane - 1 - bands).astype(F32)
    fvec = 1e-4 + band * ((bands - 1 - 1e-4) / (bands - 1))
    arg = wpos * fvec
    z = jnp.where(lane == 0, t, jnp.where(lane <= bands, jnp.cos(arg),
                                          jnp.where(lane <= 2 * bands, -jnp.sin(arg), 0.0)))
    fr = fr_ref[...]
    h = jnp.sin(fr * (jnp.dot(z, w1_ref[...], preferred_element_type=F32) + b1_ref[...]))
    h = jnp.sin(fr * (jnp.dot(h, w2_ref[...], preferred_element_type=F32) + b2_ref[...]))
    h = jnp.dot(h, w3_ref[...], preferred_element_type=F32)
    max_decay = math.log(DECAY_TARGET) / FAST_DECAY_PCT
    min_decay = math.log(DECAY_TARGET) / SLOW_DECAY_PCT
    ch = lax.broadcasted_iota(I32, (1, HY_WIDTH), 1).astype(F32)
    deltas = jnp.abs(min_decay + ch * ((max_decay - min_decay) / (HY_WIDTH - 1)))
    decay = jnp.exp(-t * deltas)
    fwd = h[:, :HY_WIDTH] * decay
    bwd = jnp.where(pos == 0.0, 0.0, h[:, HY_WIDTH:] * decay)
    for c in range(HY_WIDTH // LANES):
        fb_ref[0, c] = fwd[:, c * LANES:(c + 1) * LANES].astype(fb_ref.dtype)
        fb_ref[1, c] = bwd[:, c * LANES:(c + 1) * LANES].astype(fb_ref.dtype)
    part = jnp.sum(jnp.abs(fwd) + jnp.abs(bwd), axis=0, keepdims=True)

    @pl.when(i == 0)
    def _():
        l1_ref[...] = jnp.zeros_like(l1_ref)
    l1_ref[...] += part


def _filter_spectrum(L, W, tables, tl=512):
    n, n1, n2, A, K1, K1p, KP = _fft_dims(L)
    g_fwd, _, f2, _ = tables
    nc = HY_WIDTH // LANES
    w1 = jnp.pad(W['filt_w1'], ((0, LANES - FILTER_EMB), (0, 0)))
    full = lambda r, c: pl.BlockSpec((r, c), lambda i: (0, 0))
    fb, l1 = pl.pallas_call(
        functools.partial(_filter_kernel, seq=L),
        grid=(L // tl,),
        in_specs=[full(LANES, FILTER_HIDDEN), full(1, FILTER_HIDDEN),
                  full(FILTER_HIDDEN, FILTER_HIDDEN), full(1, FILTER_HIDDEN),
                  full(FILTER_HIDDEN, 2 * HY_WIDTH), full(1, FILTER_HIDDEN)],
        out_specs=[pl.BlockSpec((2, nc, tl, LANES), lambda i: (0, 0, i, 0)),
                   pl.BlockSpec((1, HY_WIDTH), lambda i: (0, 0))],
        out_shape=[jax.ShapeDtypeStruct((2, nc, L, LANES), BF16),
                   jax.ShapeDtypeStruct((1, HY_WIDTH), F32)],
        compiler_params=_cparams("arbitrary"),
        name="hyena_filter",
    )(w1, W['filt_b1'].reshape(1, -1), W['filt_w2'], W['filt_b2'].reshape(1, -1), W['filt_w3'],
      W['filt_freq'].reshape(1, -1))
    fbr = fb.reshape(2, nc, A, n2 * LANES)
    rows = (2 * K1p // 8) * n2 * 8
    return pl.pallas_call(
        functools.partial(_kspec_kernel, n2=n2, K1p=K1p),
        grid=(nc,),
        in_specs=[pl.BlockSpec((2, None, A, n2 * LANES), lambda c: (0, c, 0, 0)),
                  pl.BlockSpec((1, LANES), lambda c: (0, c)),
                  pl.BlockSpec((n2, 2 * K1p, A), lambda c: (0, 0, 0)),
                  pl.BlockSpec((2 * n2, 2 * n2), lambda c: (0, 0))],
        out_specs=pl.BlockSpec((None, K1p, 2 * n2, LANES), lambda c: (c, 0, 0, 0)),
        out_shape=jax.ShapeDtypeStruct((nc, K1p, 2 * n2, LANES), BF16),
        scratch_shapes=[pltpu.VMEM((2, rows, LANES), F32)],
        compiler_params=_cparams("arbitrary"),
        name="filter_spectrum",
    )(fbr, l1, g_fwd, f2)


def _long_conv(zc, kf, tables, nb):
    B, nc, L, _ = zc.shape
    n, n1, n2, A, K1, K1p, KP = _fft_dims(L)
    g_fwd, g_inv, f2, f2i = tables
    x = zc.reshape(B, nc, A, n2 * LANES)
    rows_y = (2 * K1p // 8) * n2 * 8
    rows_w = (2 * n2 // 8) * K1p * 8
    const = lambda shape: pl.BlockSpec(shape, lambda c, j: (0,) * len(shape),
                                       pipeline_mode=pl.Buffered(1))
    y = pl.pallas_call(
        functools.partial(_lconv_kernel, nb=nb, n2=n2, K1p=K1p, KP=KP),
        grid=(nc, B // nb),
        in_specs=[pl.BlockSpec((nb, None, A, n2 * LANES), lambda c, j: (j, c, 0, 0)),
                  const((n2, 2 * K1p, A)), const((2 * n2, 2 * n2)), const((2 * n2, 2 * n2)),
                  pl.BlockSpec((None, K1p, 2 * n2, LANES), lambda c, j: (c, 0, 0, 0)),
                  const((n2, A, KP))],
        out_specs=pl.BlockSpec((nb, None, A, n2 * LANES), lambda c, j: (j, c, 0, 0)),
        out_shape=jax.ShapeDtypeStruct((B, nc, A, n2 * LANES), BF16),
        scratch_shapes=[pltpu.VMEM((nb, rows_y, LANES), F32), pltpu.VMEM((nb, rows_w, LANES), F32)],
        compiler_params=_cparams("arbitrary", "arbitrary"),
        name="long_conv",
    )(x, g_fwd, f2, f2i, kf, g_inv)
    return y.reshape(B, nc, L, LANES)


def _merge_kernel(x_ref, a_ref, y_ref, x0_ref, zb_ref, g0_ref, g1_ref, wa_ref, wh_ref, wo_ref, o_ref):
    a = jnp.dot(a_ref[...], wa_ref[...], preferred_element_type=F32)
    x0 = x0_ref[...].astype(F32)
    y = jnp.concatenate([y_ref[0, c].astype(F32) for c in range(HY_WIDTH // LANES)], axis=1)
    hz = (y * x0 + zb_ref[...].astype(F32)).astype(BF16)
    hzp = jnp.dot(hz, wh_ref[...], preferred_element_type=F32)
    g0 = jax.nn.sigmoid(g0_ref[...].astype(F32))
    g1 = jax.nn.sigmoid(g1_ref[...].astype(F32))
    m = (g0 * a + g1 * hzp).astype(BF16)
    o_ref[...] = x_ref[...] + jnp.dot(m, wo_ref[...], preferred_element_type=F32)


def _merge(xt, a, yc, x0, zb, proj, wa, wh, wo, B, L, tm=512):
    T = B * L
    nl = L // tm
    full = lambda r, c: pl.BlockSpec((r, c), lambda b, i: (0, 0))
    tok = lambda c, cb=0: pl.BlockSpec((tm, c), lambda b, i, cb=cb: (b * nl + i, cb))
    return pl.pallas_call(
        _merge_kernel,
        grid=(B, nl),
        in_specs=[tok(D_MODEL), tok(ATT_V_W),
                  pl.BlockSpec((1, HY_WIDTH // LANES, tm, LANES), lambda b, i: (b, 0, i, 0)),
                  tok(HY_WIDTH), tok(HY_WIDTH), tok(D_MODEL, 3), tok(D_MODEL, 4),
                  full(ATT_V_W, D_MODEL), full(HY_WIDTH, D_MODEL), full(D_MODEL, D_MODEL)],
        out_specs=tok(D_MODEL),
        out_shape=jax.ShapeDtypeStruct((T, D_MODEL), F32),
        compiler_params=_cparams("parallel", "parallel"),
        name="merge",
    )(xt, a, yc, x0, zb, proj, proj, wa, wh, wo)


def _memkv_kernel(m_ref, g_ref, w_ref, o_ref):
    u = _rms(m_ref[...], g_ref[...]).astype(BF16)
    o_ref[...] = jnp.dot(u, w_ref[...], preferred_element_type=F32).astype(BF16)


def _mem_kv(mem_t, g, w_xkv_bf):
    R = mem_t.shape[0]
    return pl.pallas_call(
        _memkv_kernel,
        grid=(R // N_MEM,),
        in_specs=[pl.BlockSpec((N_MEM, D_MODEL), lambda i: (i, 0)),
                  pl.BlockSpec((1, D_MODEL), lambda i: (0, 0)),
                  pl.BlockSpec((D_MODEL, 2 * D_MODEL), lambda i: (0, 0))],
        out_specs=pl.BlockSpec((N_MEM, 2 * D_MODEL), lambda i: (i, 0)),
        out_shape=jax.ShapeDtypeStruct((R, 2 * D_MODEL), BF16),
        compiler_params=_cparams("parallel"),
        name="mem_kv",
    )(mem_t, g.reshape(1, D_MODEL), w_xkv_bf)


def _xattn_kernel(x_ref, kv_ref, gx_ref, wq_ref, wo_ref, gf_ref, wr_ref, x2_ref, h_ref, aff_ref):
    x = x_ref[...]
    u = _rms(x, gx_ref[...]).astype(BF16)
    q = jnp.dot(u, wq_ref[...], preferred_element_type=F32)
    q = (q * (X_HEAD_DIM ** -0.5)).astype(BF16)
    outs = []
    for h in range(X_HEADS):
        qh = q[:, h * X_HEAD_DIM:(h + 1) * X_HEAD_DIM]
        kh = kv_ref[:, h * X_HEAD_DIM:(h + 1) * X_HEAD_DIM]
        vh = kv_ref[:, D_MODEL + h * X_HEAD_DIM:D_MODEL + (h + 1) * X_HEAD_DIM]
        s = lax.dot_general(qh, kh, (((1,), (1,)), ((), ())), preferred_element_type=F32)
        s = s - jnp.max(s, axis=1, keepdims=True)
        p = jnp.exp(s)
        p = p / jnp.sum(p, axis=1, keepdims=True)
        outs.append(jnp.dot(p.astype(BF16), vh, preferred_element_type=F32))
    o = jnp.concatenate(outs, axis=1).astype(BF16)
    x2 = x + jnp.dot(o, wo_ref[...], preferred_element_type=F32)
    x2_ref[...] = x2
    hf = _rms(x2, gf_ref[...]).astype(BF16)
    h_ref[...] = hf
    lg = lax.dot_general(wr_ref[...], hf, (((1,), (1,)), ((), ())), preferred_element_type=F32)
    lg = lg - jnp.max(lg, axis=0, keepdims=True)
    e = jnp.exp(lg)
    aff_ref[...] = e / jnp.sum(e, axis=0, keepdims=True)


def _cross_attn(x1, kv, gx, wq, wo, gf, wr_t, B, L, tm=512):
    T = B * L
    nl = L // tm
    full = lambda r, c: pl.BlockSpec((r, c), lambda b, i: (0, 0))
    tok = lambda c: pl.BlockSpec((tm, c), lambda b, i: (b * nl + i, 0))
    return pl.pallas_call(
        _xattn_kernel,
        grid=(B, nl),
        in_specs=[tok(D_MODEL),
                  pl.BlockSpec((N_MEM, 2 * D_MODEL), lambda b, i: (b, 0)),
                  full(1, D_MODEL), full(D_MODEL, D_MODEL), full(D_MODEL, D_MODEL),
                  full(1, D_MODEL), full(N_EXPERTS, D_MODEL)],
        out_specs=[tok(D_MODEL), tok(D_MODEL),
                   pl.BlockSpec((N_EXPERTS, tm), lambda b, i: (0, b * nl + i))],
        out_shape=[jax.ShapeDtypeStruct((T, D_MODEL), F32),
                   jax.ShapeDtypeStruct((T, D_MODEL), BF16),
                   jax.ShapeDtypeStruct((N_EXPERTS, T), F32)],
        compiler_params=_cparams("parallel", "parallel"),
        name="cross_attn",
    )(x1, kv, gx.reshape(1, D_MODEL), wq, wo, gf.reshape(1, D_MODEL), wr_t)


def _select_kernel(aff_ref, gate_ref, *, cap):
    aff = aff_ref[...]
    bits = pltpu.bitcast(aff, I32)
    T = aff.shape[1]

    def count(mask):
        return jnp.sum(mask.astype(I32), axis=1, keepdims=True)

    def vbody(k, thr):
        cand = thr | (jnp.int32(1) << (30 - k))
        return jnp.where(count(bits >= cand) >= cap, cand, thr)

    thr = lax.fori_loop(0, 31, vbody, jnp.zeros((N_EXPERTS, 1), I32))
    gt = bits > thr
    eq = bits == thr
    need = cap - count(gt)
    tok = lax.broadcasted_iota(I32, aff.shape, 1)
    nbit = int(T).bit_length()

    def jbody(k, j):
        cand = j + (jnp.int32(1) << (nbit - 1 - k))
        return jnp.where(count(eq & (tok < cand)) <= need, cand, j)

    j = lax.fori_loop(0, nbit, jbody, jnp.zeros((N_EXPERTS, 1), I32))
    sel = gt | (eq & (tok < j))
    gate_ref[...] = jnp.where(sel, aff, 0.0)


def _select(aff_t, cap):
    E, T = aff_t.shape
    return pl.pallas_call(
        functools.partial(_select_kernel, cap=cap),
        grid=(1,),
        in_specs=[pl.BlockSpec((E, T), lambda i: (0, 0))],
        out_specs=pl.BlockSpec((E, T), lambda i: (0, 0)),
        out_shape=jax.ShapeDtypeStruct((E, T), F32),
        compiler_params=_cparams("arbitrary"),
        name="ec_select",
    )(aff_t)


def _ffn_kernel(h_ref, gate_ref, x_ref, wg_ref, wu_ref, wd_ref, gn_ref, o_ref, acc_ref):
    e = pl.program_id(1)

    @pl.when(e == 0)
    def _():
        acc_ref[...] = jnp.zeros_like(acc_ref)

    h = h_ref[...]
    lane = lax.broadcasted_iota(I32, gate_ref.shape, 1)
    g = jnp.sum(jnp.where(lane == e, gate_ref[...], 0.0), axis=1, keepdims=True)
    half = EXPERT_FF // 2
    y = jnp.zeros(acc_ref.shape, F32)
    for c in range(2):
        sl = slice(c * half, (c + 1) * half)
        a = jnp.dot(h, wg_ref[0, :, sl], preferred_element_type=F32)
        b = jnp.dot(h, wu_ref[0, :, sl], preferred_element_type=F32)
        he = (jax.nn.silu(a) * b).astype(BF16)
        y = y + jnp.dot(he, wd_ref[0, sl, :], preferred_element_type=F32)
    acc_ref[...] += y * g

    @pl.when(e == pl.num_programs(1) - 1)
    def _():
        o_ref[...] = _rms(x_ref[...] + acc_ref[...], gn_ref[...])


def _expert_ffn(h, gate_te, x2, wg, wu, wd, gn, tm=1024):
    T = h.shape[0]
    tok = lambda c: pl.BlockSpec((tm, c), lambda i, e: (i, 0))
    wspec = lambda r, c: pl.BlockSpec((1, r, c), lambda i, e: (e, 0, 0))
    return pl.pallas_call(
        _ffn_kernel,
        grid=(T // tm, N_EXPERTS),
        in_specs=[tok(D_MODEL), tok(N_EXPERTS), tok(D_MODEL),
                  wspec(D_MODEL, EXPERT_FF), wspec(D_MODEL, EXPERT_FF), wspec(EXPERT_FF, D_MODEL),
                  pl.BlockSpec((1, D_MODEL), lambda i, e: (0, 0))],
        out_specs=tok(D_MODEL),
        out_shape=jax.ShapeDtypeStruct((T, D_MODEL), F32),
        scratch_shapes=[pltpu.VMEM((tm, D_MODEL), F32)],
        compiler_params=_cparams("parallel", "arbitrary"),
        name="expert_ffn",
    )(h, gate_te, x2, wg, wu, wd, gn.reshape(1, D_MODEL))


def _run_group(x, mem, W):
    B, L, _ = x.shape
    T = B * L
    xt = x.reshape(T, D_MODEL)
    proj = _inproj(xt, W['norm_mix'], W['w_in'], L)
    a = _diff_attention(proj, W['lam'], W['subln'], B, L)
    zc, x0, zb = _hyena_prep(proj, W['hy_conv_w'], W['hy_conv_b'], W['filt_bias'], B, L)
    tables = _dft_tables(L)
    kf = _filter_spectrum(L, W, tables)
    yc = _long_conv(zc, kf, tables, nb=1 if L >= 4096 else 4)
    x1 = _merge(xt, a, yc, x0, zb, proj, W['w_br_attn'], W['w_br_hyena'], W['w_out'], B, L)
    kv = _mem_kv(mem.reshape(B * N_MEM, D_MODEL), W['norm_mem'], W['w_xkv'])
    x2, hf, aff = _cross_attn(x1, kv, W['norm_x'], W['w_xq'], W['w_xo'], W['norm_ffn'],
                              W['w_router_t'], B, L)
    gate = _select(aff, EC_CAPACITY_FACTOR * T // N_EXPERTS)
    y = _expert_ffn(hf, gate.T, x2, W['w_exp_gate'], W['w_exp_up'], W['w_exp_down'], W['norm_final'])
    return y.reshape(B, L, D_MODEL)


def kernel(x_prompt, x_sample, mem_prompt, mem_sample, norm_mix, w_in, lambda_q1, lambda_k1, lambda_q2, lambda_k2, subln, w_br_attn, hy_conv_w, hy_conv_b, filt_w1, filt_b1, filt_w2, filt_b2, filt_w3, filt_freq, filt_bias, w_br_hyena, w_out, norm_x, norm_mem, w_xq, w_xkv, w_xo, norm_ffn, w_router, w_exp_gate, w_exp_up, w_exp_down, norm_final):
    l = 0
    lam = (jnp.exp(jnp.sum(lambda_q1[l] * lambda_k1[l])) - jnp.exp(jnp.sum(lambda_q2[l] * lambda_k2[l]))
           + LAM_INIT).astype(F32)
    W = dict(
        norm_mix=norm_mix[l], w_in=w_in[l].astype(BF16), lam=lam, subln=subln[l],
        w_br_attn=w_br_attn[l].astype(BF16), hy_conv_w=hy_conv_w[l], hy_conv_b=hy_conv_b[l],
        filt_w1=filt_w1[l], filt_b1=filt_b1[l], filt_w2=filt_w2[l], filt_b2=filt_b2[l],
        filt_w3=filt_w3[l], filt_freq=filt_freq[l], filt_bias=filt_bias[l],
        w_br_hyena=w_br_hyena[l].astype(BF16), w_out=w_out[l].astype(BF16),
        norm_x=norm_x[l], norm_mem=norm_mem[l], w_xq=w_xq[l].astype(BF16),
        w_xkv=w_xkv[l].astype(BF16), w_xo=w_xo[l].astype(BF16), norm_ffn=norm_ffn[l],
        w_router_t=w_router[l].T.astype(BF16), w_exp_gate=w_exp_gate[l].astype(BF16),
        w_exp_up=w_exp_up[l].astype(BF16), w_exp_down=w_exp_down[l].astype(BF16),
        norm_final=norm_final)
    return (_run_group(x_prompt, mem_prompt, W), _run_group(x_sample, mem_sample, W))
```

```python
import functools
import math

import jax
import jax.numpy as jnp
from jax import lax
from jax.experimental import pallas as pl
from jax.experimental.pallas import tpu as pltpu

F32 = jnp.float32
BF16 = jnp.bfloat16
I32 = jnp.int32

D_MODEL = 1024
N_ATT_HEADS = 4
ATT_HEAD_DIM = 64
ATT_V_DIM = 128
ROT_DIM = 16
ROPE_THETA = 500000.0
ATT_QK_W = 512
ATT_V_W = 512
HY_WIDTH = 512
FILTER_EMB = 33
FILTER_HIDDEN = 64
DECAY_TARGET = 1e-2
FAST_DECAY_PCT = 0.3
SLOW_DECAY_PCT = 1.5
IN_W = 5120
N_MEM = 256
X_HEADS = 4
X_HEAD_DIM = 256
N_EXPERTS = 16
EC_CAPACITY_FACTOR = 2
EXPERT_FF = 1024
EPS = 1e-6
LAM_INIT = 0.8 - 0.6 * math.exp(-0.3 * 0)

LANES = 128
ATT_ROW_GROUPS = 8
FFT_UNROLL = 4
VMEM_LIMIT_BYTES = 56 * 1024 * 1024


def _cparams(*sem):
    return pltpu.CompilerParams(dimension_semantics=sem, vmem_limit_bytes=VMEM_LIMIT_BYTES)


def _rms(x, g):
    return x * lax.rsqrt(jnp.mean(x * x, axis=-1, keepdims=True) + EPS) * g


def _inproj_kernel(x_ref, g_ref, w_ref, cos_ref, sa_ref, sb_ref, o_ref):
    u = _rms(x_ref[...], g_ref[...]).astype(BF16)
    n_col = IN_W // 512
    for j in range(n_col):
        acc = jnp.dot(u, w_ref[:, j * 512:(j + 1) * 512], preferred_element_type=F32)
        if j < 2:
            parts = []
            for h in range(4):
                t = acc[:, h * LANES:(h + 1) * LANES]
                r = (t * cos_ref[...] + pltpu.roll(t, 8, 1) * sa_ref[...]
                     + pltpu.roll(t, LANES - 8, 1) * sb_ref[...])
                parts.append(r)
            acc = jnp.concatenate(parts, axis=1)
        if j == 0:
            acc = acc * (ATT_HEAD_DIM ** -0.5 * math.log2(math.e))
        o_ref[:, j * 512:(j + 1) * 512] = acc.astype(BF16)


def _rope_tables(L):
    inv = ROPE_THETA ** (-jnp.arange(0, ROT_DIM, 2, dtype=F32) / ROT_DIM)
    pos = jnp.arange(L, dtype=F32)
    ang = pos[:, None] * inv[None, :]
    cos, sin = jnp.cos(ang), jnp.sin(ang)
    one = jnp.ones((L, 48), F32)
    zero8 = jnp.zeros((L, 8), F32)
    zero48 = jnp.zeros((L, 48), F32)
    c64 = jnp.concatenate([cos, cos, one], axis=1)
    sa64 = jnp.concatenate([zero8, sin, zero48], axis=1)
    sb64 = jnp.concatenate([-sin, zero8, zero48], axis=1)
    tile2 = lambda a: jnp.concatenate([a, a], axis=1)
    return tile2(c64), tile2(sa64), tile2(sb64)


def _inproj(xt, g, w_in_bf, L, tm=512):
    T = xt.shape[0]
    cos, sa, sb = _rope_tables(L)
    nl = L // tm
    tab = pl.BlockSpec((tm, LANES), lambda i: (i % nl, 0))
    return pl.pallas_call(
        _inproj_kernel,
        grid=(T // tm,),
        in_specs=[pl.BlockSpec((tm, D_MODEL), lambda i: (i, 0)),
                  pl.BlockSpec((1, D_MODEL), lambda i: (0, 0)),
                  pl.BlockSpec((D_MODEL, IN_W), lambda i: (0, 0)),
                  tab, tab, tab],
        out_specs=pl.BlockSpec((tm, IN_W), lambda i: (i, 0)),
        out_shape=jax.ShapeDtypeStruct((T, IN_W), BF16),
        compiler_params=_cparams("parallel"),
        name="inproj",
    )(xt, g.reshape(1, D_MODEL), w_in_bf, cos, sa, sb)


def _attn_kernel(lam_ref, q_ref, k_ref, v_ref, sg_ref, o_ref, vx_ref, *, seq, kc):
    tq = q_ref.shape[0]

    @pl.when(pl.program_id(2) == 0)
    def _():
        lane = lax.broadcasted_iota(I32, (seq, LANES), 1)
        vx_ref[:, :LANES] = v_ref[...]
        vx_ref[:, LANES:] = jnp.where(lane == 0, 1.0, 0.0).astype(BF16)

    q = q_ref[...]
    lane = lax.broadcasted_iota(I32, q.shape, 1)
    zero = jnp.zeros_like(q)
    qq = jnp.concatenate([jnp.where(lane < ATT_HEAD_DIM, q, zero),
                          jnp.where(lane >= ATT_HEAD_DIM, q, zero)], axis=0)

    rows = 2 * tq // ATT_ROW_GROUPS
    groups = [qq[g * rows:(g + 1) * rows] for g in range(ATT_ROW_GROUPS)]

    def scores(qg, c):
        start = pl.multiple_of(c * kc, kc)
        s = lax.dot_general(qg, k_ref[pl.ds(start, kc), :], (((1,), (1,)), ((), ())),
                            preferred_element_type=F32)
        return s, vx_ref[pl.ds(start, kc), :]

    if seq == kc:
        accs = []
        for qg in groups:
            s, vx = scores(qg, 0)
            p = jnp.exp2(s - jnp.max(s, axis=1, keepdims=True))
            accs.append(jnp.dot(p.astype(BF16), vx, preferred_element_type=F32))
    else:
        def body(c, carry):
            out = []
            for qg, (m, acc) in zip(groups, carry):
                s, vx = scores(qg, c)
                m_new = jnp.maximum(m, jnp.max(s, axis=1, keepdims=True))
                p = jnp.exp2(s - m_new)
                pv = jnp.dot(p.astype(BF16), vx, preferred_element_type=F32)
                out.append((m_new, jnp.exp2(m - m_new) * acc + pv))
            return tuple(out)

        init = tuple((jnp.full((rows, 1), -jnp.inf, F32), jnp.zeros((rows, 2 * LANES), F32))
                     for _ in groups)
        accs = [a for _, a in lax.fori_loop(0, seq // kc, body, init)]
    acc = jnp.concatenate(accs, axis=0)
    o = acc[:, :ATT_V_DIM] / acc[:, ATT_V_DIM:ATT_V_DIM + 1]
    a = o[:tq] - lam_ref[0] * o[tq:]
    a = _rms(a, sg_ref[...]) * (1.0 - LAM_INIT)
    o_ref[...] = a.astype(BF16)


def _diff_attention(proj, lam, subln, B, L):
    T = B * L
    tq = min(1024, L)
    nq = L // tq
    kc = min(2048, L)
    return pl.pallas_call(
        functools.partial(_attn_kernel, seq=L, kc=kc),
        grid=(B, N_ATT_HEADS, nq),
        in_specs=[pl.BlockSpec(memory_space=pltpu.SMEM),
                  pl.BlockSpec((tq, LANES), lambda b, h, i: (b * nq + i, h)),
                  pl.BlockSpec((L, LANES), lambda b, h, i: (b, 4 + h)),
                  pl.BlockSpec((L, LANES), lambda b, h, i: (b, 8 + h)),
                  pl.BlockSpec((1, ATT_V_DIM), lambda b, h, i: (0, 0))],
        out_specs=pl.BlockSpec((tq, ATT_V_DIM), lambda b, h, i: (b * nq + i, h)),
        out_shape=jax.ShapeDtypeStruct((T, ATT_V_W), BF16),
        scratch_shapes=[pltpu.VMEM((L, 2 * LANES), BF16)],
        compiler_params=_cparams("parallel", "parallel", "arbitrary"),
        name="diff_attn",
    )(lam.reshape(1), proj, proj, proj, subln.reshape(1, ATT_V_DIM))


def _hyprep_kernel(hy_ref, prev_ref, next_ref, w_ref, b_ref, fb_ref, zc_ref, x0_ref, zb_ref):
    i = pl.program_id(1)
    n = pl.num_programs(1)
    hy = hy_ref[...].astype(F32)
    tl = hy.shape[0]
    prev_row = jnp.where(i == 0, 0.0, prev_ref[15:16, :].astype(F32))
    next_row = jnp.where(i == n - 1, 0.0, next_ref[0:1, :].astype(F32))
    row = lax.broadcasted_iota(I32, hy.shape, 0)
    up = jnp.where(row == 0, prev_row, pltpu.roll(hy, 1, 0))
    dn = jnp.where(row == tl - 1, next_row, pltpu.roll(hy, tl - 1, 0))
    conv = up * w_ref[0:1, :] + hy * w_ref[1:2, :] + dn * w_ref[2:3, :] + b_ref[...]
    x0 = conv[:, :HY_WIDTH]
    x1 = conv[:, HY_WIDTH:2 * HY_WIDTH]
    hv = conv[:, 2 * HY_WIDTH:]
    z = hv * x1
    for c in range(HY_WIDTH // LANES):
        zc_ref[0, c] = z[:, c * LANES:(c + 1) * LANES].astype(zc_ref.dtype)
    x0_ref[...] = x0.astype(x0_ref.dtype)
    zb_ref[...] = (z * fb_ref[...] * x0).astype(zb_ref.dtype)


def _hyena_prep(proj, conv_w, conv_b, filt_bias, B, L, tl=512):
    T = B * L
    nl = L // tl
    hw = 3 * HY_WIDTH
    rb = tl // 16
    nrb = T // 16
    return pl.pallas_call(
        _hyprep_kernel,
        grid=(B, nl),
        in_specs=[pl.BlockSpec((tl, hw), lambda b, i: (b * nl + i, 1)),
                  pl.BlockSpec((16, hw), lambda b, i: (jnp.maximum((b * nl + i) * rb - 1, 0), 1)),
                  pl.BlockSpec((16, hw), lambda b, i: (jnp.minimum((b * nl + i + 1) * rb, nrb - 1), 1)),
                  pl.BlockSpec((3, hw), lambda b, i: (0, 0)),
                  pl.BlockSpec((1, hw), lambda b, i: (0, 0)),
                  pl.BlockSpec((1, HY_WIDTH), lambda b, i: (0, 0))],
        out_specs=[pl.BlockSpec((1, HY_WIDTH // LANES, tl, LANES), lambda b, i: (b, 0, i, 0)),
                   pl.BlockSpec((tl, HY_WIDTH), lambda b, i: (b * nl + i, 0)),
                   pl.BlockSpec((tl, HY_WIDTH), lambda b, i: (b * nl + i, 0))],
        out_shape=[jax.ShapeDtypeStruct((B, HY_WIDTH // LANES, L, LANES), BF16),
                   jax.ShapeDtypeStruct((T, HY_WIDTH), BF16),
                   jax.ShapeDtypeStruct((T, HY_WIDTH), BF16)],
        compiler_params=_cparams("parallel", "parallel"),
        name="hyena_prep",
    )(proj, proj, proj, conv_w, conv_b.reshape(1, hw), filt_bias.reshape(1, HY_WIDTH))


def _fft_dims(L):
    n = 2 * L
    n1 = 1 << ((n.bit_length() - 1 + 1) // 2)
    n2 = n // n1
    A = n1 // 2
    K1 = n1 // 2 + 1
    K1p = -(-K1 // 8) * 8
    KP = -(-2 * K1p // LANES) * LANES
    assert n1 * n2 == n and A * n2 == L and n2 % 8 == 0 and A % 16 == 0
    return n, n1, n2, A, K1, K1p, KP


def _dft_tables(L):
    n, n1, n2, A, K1, K1p, KP = _fft_dims(L)
    b = jnp.arange(n2, dtype=I32)
    a = jnp.arange(A, dtype=I32)
    k1 = jnp.arange(K1p, dtype=I32)
    t = a[None, :] * n2 + b[:, None]
    m = (k1[None, :, None] * t[:, None, :]) % n
    th = m.astype(F32) * (2.0 * math.pi / n)
    valid = (k1 < K1).astype(F32)[None, :, None]
    c, s = jnp.cos(th) * valid, jnp.sin(th) * valid
    g_fwd = jnp.concatenate([c, -s], axis=1).astype(BF16)
    w = jnp.where((k1 == 0) | (k1 == n1 // 2), 1.0, 2.0)[None, :, None] / n
    ci = jnp.transpose(c * w, (0, 2, 1))
    si = jnp.transpose(-s * w, (0, 2, 1))
    pad = jnp.zeros((n2, A, KP - 2 * K1p), F32)
    g_inv = jnp.concatenate([ci, si, pad], axis=2).astype(BF16)
    k2 = jnp.arange(n2, dtype=I32)
    ph = ((k2[:, None] * b[None, :]) % n2).astype(F32) * (2.0 * math.pi / n2)
    c2, s2 = jnp.cos(ph), jnp.sin(ph)
    f2 = jnp.concatenate([jnp.concatenate([c2, s2], 1), jnp.concatenate([-s2, c2], 1)], 0).astype(BF16)
    f2i = jnp.concatenate([jnp.concatenate([c2, -s2], 1), jnp.concatenate([s2, c2], 1)], 0).astype(BF16)
    return g_fwd, g_inv, f2, f2i


def _fft_rows_fwd(x_ref, g_ref, yf_ref, *, nb, n2, K1p):
    def body(b, _):
        off = pl.multiple_of(b * LANES, LANES)
        xb = jnp.concatenate([x_ref[i, :, pl.ds(off, LANES)] for i in range(nb)], axis=1)
        y = jnp.dot(g_ref[b], xb, preferred_element_type=F32)
        for i in range(nb):
            for r in range(2 * K1p // 8):
                row = pl.multiple_of((r * n2 + b) * 8, 8)
                yf_ref[i, pl.ds(row, 8), :] = y[r * 8:(r + 1) * 8, i * LANES:(i + 1) * LANES]
        return 0
    lax.fori_loop(0, n2, body, 0, unroll=FFT_UNROLL)


def _fft_cols_fwd(yf_ref, f2_ref, k1, *, nb, n2, K1p):
    kt, ks = k1 // 8, k1 % 8
    parts = []
    for i in range(nb):
        re = yf_ref[i, pl.ds(kt * (n2 * 8) + ks, n2, stride=8), :]
        im = yf_ref[i, pl.ds((K1p // 8 + kt) * (n2 * 8) + ks, n2, stride=8), :]
        parts.append(jnp.concatenate([re, im], axis=0))
    yk = jnp.concatenate(parts, axis=1).astype(BF16)
    return jnp.dot(f2_ref[...], yk, preferred_element_type=F32)


def _kspec_kernel(fb_ref, l1_ref, g_ref, f2_ref, kf_ref, yf_ref, *, n2, K1p):
    _fft_rows_fwd(fb_ref, g_ref, yf_ref, nb=2, n2=n2, K1p=K1p)
    inv = 1.0 / l1_ref[...]

    def body(k1, _):
        z = _fft_cols_fwd(yf_ref, f2_ref, k1, nb=2, n2=n2, K1p=K1p)
        zf, zb = z[:, :LANES], z[:, LANES:]
        kr = (zf[:n2] + zb[:n2]) * inv
        ki = (zf[n2:] - zb[n2:]) * inv
        kf_ref[k1] = jnp.concatenate([kr, ki], axis=0).astype(kf_ref.dtype)
        return 0
    lax.fori_loop(0, K1p, body, 0, unroll=FFT_UNROLL)


def _lconv_kernel(x_ref, g_ref, f2_ref, f2i_ref, kf_ref, gi_ref, o_ref, yf_ref, wf_ref, *,
                  nb, n2, K1p, KP):
    _fft_rows_fwd(x_ref, g_ref, yf_ref, nb=nb, n2=n2, K1p=K1p)

    def kbody(k1, _):
        z = _fft_cols_fwd(yf_ref, f2_ref, k1, nb=nb, n2=n2, K1p=K1p)
        kf = kf_ref[k1].astype(F32)
        kr = jnp.concatenate([kf[:n2]] * nb, axis=1)
        ki = jnp.concatenate([kf[n2:]] * nb, axis=1)
        zr, zi = z[:n2], z[n2:]
        p = jnp.concatenate([zr * kr - zi * ki, zr * ki + zi * kr], axis=0).astype(BF16)
        v = jnp.dot(f2i_ref[...], p, preferred_element_type=F32)
        for i in range(nb):
            for r in range(2 * n2 // 8):
                row = pl.multiple_of((r * K1p + k1) * 8, 8)
                wf_ref[i, pl.ds(row, 8), :] = v[r * 8:(r + 1) * 8, i * LANES:(i + 1) * LANES]
        return 0
    lax.fori_loop(0, K1p, kbody, 0, unroll=FFT_UNROLL)

    def bbody(b, _):
        bt, bs = b // 8, b % 8
        parts = []
        for i in range(nb):
            re = wf_ref[i, pl.ds(bt * (K1p * 8) + bs, K1p, stride=8), :]
            im = wf_ref[i, pl.ds((n2 // 8 + bt) * (K1p * 8) + bs, K1p, stride=8), :]
            parts.append(jnp.concatenate([re, im, jnp.zeros((KP - 2 * K1p, LANES), F32)], axis=0))
        vb = jnp.concatenate(parts, axis=1).astype(BF16)
        ob = jnp.dot(gi_ref[b], vb, preferred_element_type=F32)
        off = pl.multiple_of(b * LANES, LANES)
        for i in range(nb):
            o_ref[i, :, pl.ds(off, LANES)] = ob[:, i * LANES:(i + 1) * LANES].astype(o_ref.dtype)
        return 0
    lax.fori_loop(0, n2, bbody, 0, unroll=FFT_UNROLL)


def _filter_kernel(w1_ref, b1_ref, w2_ref, b2_ref, w3_ref, fr_ref, fb_ref, l1_ref, *, seq):
    i = pl.program_id(0)
    tl = fb_ref.shape[2]
    bands = (FILTER_EMB - 1) // 2
    pos = (i * tl + lax.broadcasted_iota(I32, (tl, 1), 0)).astype(F32)
    t = pos * (1.0 / (seq - 1))
    wpos = pos * (2.0 * math.pi / seq)
    lane = lax.broadcasted_iota(I32, (1, LANES), 1)
    band = jnp.where(lane <= bands, lane - 1, lane - 1 - bands).astype(F32)
    fvec = 1e-4 + band * ((bands - 1 - 1e-4) / (bands - 1))
    arg = wpos * fvec
    z = jnp.where(lane == 0, t, jnp.where(lane <= bands, jnp.cos(arg),
                                          jnp.where(lane <= 2 * bands, -jnp.sin(arg), 0.0)))
    fr = fr_ref[...]
    h = jnp.sin(fr * (jnp.dot(z, w1_ref[...], preferred_element_type=F32) + b1_ref[...]))
    h = jnp.sin(fr * (jnp.dot(h, w2_ref[...], preferred_element_type=F32) + b2_ref[...]))
    h = jnp.dot(h, w3_ref[...], preferred_element_type=F32)
    max_decay = math.log(DECAY_TARGET) / FAST_DECAY_PCT
    min_decay = math.log(DECAY_TARGET) / SLOW_DECAY_PCT
    ch = lax.broadcasted_iota(I32, (1, HY_WIDTH), 1).astype(F32)
    deltas = jnp.abs(min_decay + ch * ((max_decay - min_decay) / (HY_WIDTH - 1)))
    decay = jnp.exp(-t * deltas)
    fwd = h[:, :HY_WIDTH] * decay
    bwd = jnp.where(pos == 0.0, 0.0, h[:, HY_WIDTH:] * decay)
    for c in range(HY_WIDTH // LANES):
        fb_ref[0, c] = fwd[:, c * LANES:(c + 1) * LANES].astype(fb_ref.dtype)
        fb_ref[1, c] = bwd[:, c * LANES:(c + 1) * LANES].astype(fb_ref.dtype)
    part = jnp.sum(jnp.abs(fwd) + jnp.abs(bwd), axis=0, keepdims=True)

    @pl.when(i == 0)
    def _():
        l1_ref[...] = jnp.zeros_like(l1_ref)
    l1_ref[...] += part


def _filter_spectrum(L, W, tables, tl=512):
    n, n1, n2, A, K1, K1p, KP = _fft_dims(L)
    g_fwd, _, f2, _ = tables
    nc = HY_WIDTH // LANES
    w1 = jnp.pad(W['filt_w1'], ((0, LANES - FILTER_EMB), (0, 0)))
    full = lambda r, c: pl.BlockSpec((r, c), lambda i: (0, 0))
    fb, l1 = pl.pallas_call(
        functools.partial(_filter_kernel, seq=L),
        grid=(L // tl,),
        in_specs=[full(LANES, FILTER_HIDDEN), full(1, FILTER_HIDDEN),
                  full(FILTER_HIDDEN, FILTER_HIDDEN), full(1, FILTER_HIDDEN),
                  full(FILTER_HIDDEN, 2 * HY_WIDTH), full(1, FILTER_HIDDEN)],
        out_specs=[pl.BlockSpec((2, nc, tl, LANES), lambda i: (0, 0, i, 0)),
                   pl.BlockSpec((1, HY_WIDTH), lambda i: (0, 0))],
        out_shape=[jax.ShapeDtypeStruct((2, nc, L, LANES), BF16),
                   jax.ShapeDtypeStruct((1, HY_WIDTH), F32)],
        compiler_params=_cparams("arbitrary"),
        name="hyena_filter",
    )(w1, W['filt_b1'].reshape(1, -1), W['filt_w2'], W['filt_b2'].reshape(1, -1), W['filt_w3'],
      W['filt_freq'].reshape(1, -1))
    fbr = fb.reshape(2, nc, A, n2 * LANES)
    rows = (2 * K1p // 8) * n2 * 8
    return pl.pallas_call(
        functools.partial(_kspec_kernel, n2=n2, K1p=K1p),
        grid=(nc,),
        in_specs=[pl.BlockSpec((2, None, A, n2 * LANES), lambda c: (0, c, 0, 0)),
                  pl.BlockSpec((1, LANES), lambda c: (0, c)),
                  pl.BlockSpec((n2, 2 * K1p, A), lambda c: (0, 0, 0)),
                  pl.BlockSpec((2 * n2, 2 * n2), lambda c: (0, 0))],
        out_specs=pl.BlockSpec((None, K1p, 2 * n2, LANES), lambda c: (c, 0, 0, 0)),
        out_shape=jax.ShapeDtypeStruct((nc, K1p, 2 * n2, LANES), BF16),
        scratch_shapes=[pltpu.VMEM((2, rows, LANES), F32)],
        compiler_params=_cparams("arbitrary"),
        name="filter_spectrum",
    )(fbr, l1, g_fwd, f2)


def _long_conv(zc, kf, tables, nb):
    B, nc, L, _ = zc.shape
    n, n1, n2, A, K1, K1p, KP = _fft_dims(L)
    g_fwd, g_inv, f2, f2i = tables
    x = zc.reshape(B, nc, A, n2 * LANES)
    rows_y = (2 * K1p // 8) * n2 * 8
    rows_w = (2 * n2 // 8) * K1p * 8
    const = lambda shape: pl.BlockSpec(shape, lambda c, j: (0,) * len(shape),
                                       pipeline_mode=pl.Buffered(1))
    y = pl.pallas_call(
        functools.partial(_lconv_kernel, nb=nb, n2=n2, K1p=K1p, KP=KP),
        grid=(nc, B // nb),
        in_specs=[pl.BlockSpec((nb, None, A, n2 * LANES), lambda c, j: (j, c, 0, 0)),
                  const((n2, 2 * K1p, A)), const((2 * n2, 2 * n2)), const((2 * n2, 2 * n2)),
                  pl.BlockSpec((None, K1p, 2 * n2, LANES), lambda c, j: (c, 0, 0, 0)),
                  const((n2, A, KP))],
        out_specs=pl.BlockSpec((nb, None, A, n2 * LANES), lambda c, j: (j, c, 0, 0)),
        out_shape=jax.ShapeDtypeStruct((B, nc, A, n2 * LANES), BF16),
        scratch_shapes=[pltpu.VMEM((nb, rows_y, LANES), F32), pltpu.VMEM((nb, rows_w, LANES), F32)],
        compiler_params=_cparams("arbitrary", "arbitrary"),
        name="long_conv",
    )(x, g_fwd, f2, f2i, kf, g_inv)
    return y.reshape(B, nc, L, LANES)


def _merge_kernel(x_ref, a_ref, y_ref, x0_ref, zb_ref, g0_ref, g1_ref, wa_ref, wh_ref, wo_ref, o_ref):
    a = jnp.dot(a_ref[...], wa_ref[...], preferred_element_type=F32)
    x0 = x0_ref[...].astype(F32)
    y = jnp.concatenate([y_ref[0, c].astype(F32) for c in range(HY_WIDTH // LANES)], axis=1)
    hz = (y * x0 + zb_ref[...].astype(F32)).astype(BF16)
    hzp = jnp.dot(hz, wh_ref[...], preferred_element_type=F32)
    g0 = jax.nn.sigmoid(g0_ref[...].astype(F32))
    g1 = jax.nn.sigmoid(g1_ref[...].astype(F32))
    m = (g0 * a + g1 * hzp).astype(BF16)
    o_ref[...] = x_ref[...] + jnp.dot(m, wo_ref[...], preferred_element_type=F32)


def _merge(xt, a, yc, x0, zb, proj, wa, wh, wo, B, L, tm=512):
    T = B * L
    nl = L // tm
    full = lambda r, c: pl.BlockSpec((r, c), lambda b, i: (0, 0))
    tok = lambda c, cb=0: pl.BlockSpec((tm, c), lambda b, i, cb=cb: (b * nl + i, cb))
    return pl.pallas_call(
        _merge_kernel,
        grid=(B, nl),
        in_specs=[tok(D_MODEL), tok(ATT_V_W),
                  pl.BlockSpec((1, HY_WIDTH // LANES, tm, LANES), lambda b, i: (b, 0, i, 0)),
                  tok(HY_WIDTH), tok(HY_WIDTH), tok(D_MODEL, 3), tok(D_MODEL, 4),
                  full(ATT_V_W, D_MODEL), full(HY_WIDTH, D_MODEL), full(D_MODEL, D_MODEL)],
        out_specs=tok(D_MODEL),
        out_shape=jax.ShapeDtypeStruct((T, D_MODEL), F32),
        compiler_params=_cparams("parallel", "parallel"),
        name="merge",
    )(xt, a, yc, x0, zb, proj, proj, wa, wh, wo)


def _memkv_kernel(m_ref, g_ref, w_ref, o_ref):
    u = _rms(m_ref[...], g_ref[...]).astype(BF16)
    o_ref[...] = jnp.dot(u, w_ref[...], preferred_element_type=F32).astype(BF16)


def _mem_kv(mem_t, g, w_xkv_bf):
    R = mem_t.shape[0]
    return pl.pallas_call(
        _memkv_kernel,
        grid=(R // N_MEM,),
        in_specs=[pl.BlockSpec((N_MEM, D_MODEL), lambda i: (i, 0)),
                  pl.BlockSpec((1, D_MODEL), lambda i: (0, 0)),
                  pl.BlockSpec((D_MODEL, 2 * D_MODEL), lambda i: (0, 0))],
        out_specs=pl.BlockSpec((N_MEM, 2 * D_MODEL), lambda i: (i, 0)),
        out_shape=jax.ShapeDtypeStruct((R, 2 * D_MODEL), BF16),
        compiler_params=_cparams("parallel"),
        name="mem_kv",
    )(mem_t, g.reshape(1, D_MODEL), w_xkv_bf)


def _xattn_kernel(x_ref, kv_ref, gx_ref, wq_ref, wo_ref, gf_ref, wr_ref, x2_ref, h_ref, aff_ref):
    x = x_ref[...]
    u = _rms(x, gx_ref[...]).astype(BF16)
    q = jnp.dot(u, wq_ref[...], preferred_element_type=F32)
    q = (q * (X_HEAD_DIM ** -0.5)).astype(BF16)
    outs = []
    for h in range(X_HEADS):
        qh = q[:, h * X_HEAD_DIM:(h + 1) * X_HEAD_DIM]
        kh = kv_ref[:, h * X_HEAD_DIM:(h + 1) * X_HEAD_DIM]
        vh = kv_ref[:, D_MODEL + h * X_HEAD_DIM:D_MODEL + (h + 1) * X_HEAD_DIM]
        s = lax.dot_general(qh, kh, (((1,), (1,)), ((), ())), preferred_element_type=F32)
        s = s - jnp.max(s, axis=1, keepdims=True)
        p = jnp.exp(s)
        p = p / jnp.sum(p, axis=1, keepdims=True)
        outs.append(jnp.dot(p.astype(BF16), vh, preferred_element_type=F32))
    o = jnp.concatenate(outs, axis=1).astype(BF16)
    x2 = x + jnp.dot(o, wo_ref[...], preferred_element_type=F32)
    x2_ref[...] = x2
    hf = _rms(x2, gf_ref[...]).astype(BF16)
    h_ref[...] = hf
    lg = lax.dot_general(wr_ref[...], hf, (((1,), (1,)), ((), ())), preferred_element_type=F32)
    lg = lg - jnp.max(lg, axis=0, keepdims=True)
    e = jnp.exp(lg)
    aff_ref[...] = e / jnp.sum(e, axis=0, keepdims=True)


def _cross_attn(x1, kv, gx, wq, wo, gf, wr_t, B, L, tm=512):
    T = B * L
    nl = L // tm
    full = lambda r, c: pl.BlockSpec((r, c), lambda b, i: (0, 0))
    tok = lambda c: pl.BlockSpec((tm, c), lambda b, i: (b * nl + i, 0))
    return pl.pallas_call(
        _xattn_kernel,
        grid=(B, nl),
        in_specs=[tok(D_MODEL),
                  pl.BlockSpec((N_MEM, 2 * D_MODEL), lambda b, i: (b, 0)),
                  full(1, D_MODEL), full(D_MODEL, D_MODEL), full(D_MODEL, D_MODEL),
                  full(1, D_MODEL), full(N_EXPERTS, D_MODEL)],
        out_specs=[tok(D_MODEL), tok(D_MODEL),
                   pl.BlockSpec((N_EXPERTS, tm), lambda b, i: (0, b * nl + i))],
        out_shape=[jax.ShapeDtypeStruct((T, D_MODEL), F32),
                   jax.ShapeDtypeStruct((T, D_MODEL), BF16),
                   jax.ShapeDtypeStruct((N_EXPERTS, T), F32)],
        compiler_params=_cparams("parallel", "parallel"),
        name="cross_attn",
    )(x1, kv, gx.reshape(1, D_MODEL), wq, wo, gf.reshape(1, D_MODEL), wr_t)


def _select_kernel(aff_ref, gate_ref, *, cap):
    aff = aff_ref[...]
    bits = pltpu.bitcast(aff, I32)
    T = aff.shape[1]

    def count(mask):
        return jnp.sum(mask.astype(I32), axis=1, keepdims=True)

    def vbody(k, thr):
        cand = thr | (jnp.int32(1) << (30 - k))
        return jnp.where(count(bits >= cand) >= cap, cand, thr)

    thr = lax.fori_loop(0, 31, vbody, jnp.zeros((N_EXPERTS, 1), I32))
    gt = bits > thr
    eq = bits == thr
    need = cap - count(gt)
    tok = lax.broadcasted_iota(I32, aff.shape, 1)
    nbit = int(T).bit_length()

    def jbody(k, j):
        cand = j + (jnp.int32(1) << (nbit - 1 - k))
        return jnp.where(count(eq & (tok < cand)) <= need, cand, j)

    j = lax.fori_loop(0, nbit, jbody, jnp.zeros((N_EXPERTS, 1), I32))
    sel = gt | (eq & (tok < j))
    gate_ref[...] = jnp.where(sel, aff, 0.0)


def _select(aff_t, cap):
    E, T = aff_t.shape
    return pl.pallas_call(
        functools.partial(_select_kernel, cap=cap),
        grid=(1,),
        in_specs=[pl.BlockSpec((E, T), lambda i: (0, 0))],
        out_specs=pl.BlockSpec((E, T), lambda i: (0, 0)),
        out_shape=jax.ShapeDtypeStruct((E, T), F32),
        compiler_params=_cparams("arbitrary"),
        name="ec_select",
    )(aff_t)


def _ffn_kernel(h_ref, gate_ref, x_ref, wg_ref, wu_ref, wd_ref, gn_ref, o_ref, acc_ref):
    e = pl.program_id(1)

    @pl.when(e == 0)
    def _():
        acc_ref[...] = jnp.zeros_like(acc_ref)

    h = h_ref[...]
    lane = lax.broadcasted_iota(I32, gate_ref.shape, 1)
    g = jnp.sum(jnp.where(lane == e, gate_ref[...], 0.0), axis=1, keepdims=True)
    half = EXPERT_FF // 2
    y = jnp.zeros(acc_ref.shape, F32)
    for c in range(2):
        sl = slice(c * half, (c + 1) * half)
        a = jnp.dot(h, wg_ref[0, :, sl], preferred_element_type=F32)
        b = jnp.dot(h, wu_ref[0, :, sl], preferred_element_type=F32)
        he = (jax.nn.silu(a) * b).astype(BF16)
        y = y + jnp.dot(he, wd_ref[0, sl, :], preferred_element_type=F32)
    acc_ref[...] += y * g

    @pl.when(e == pl.num_programs(1) - 1)
    def _():
        o_ref[...] = _rms(x_ref[...] + acc_ref[...], gn_ref[...])


def _expert_ffn(h, gate_te, x2, wg, wu, wd, gn, tm=1024):
    T = h.shape[0]
    tok = lambda c: pl.BlockSpec((tm, c), lambda i, e: (i, 0))
    wspec = lambda r, c: pl.BlockSpec((1, r, c), lambda i, e: (e, 0, 0))
    return pl.pallas_call(
        _ffn_kernel,
        grid=(T // tm, N_EXPERTS),
        in_specs=[tok(D_MODEL), tok(N_EXPERTS), tok(D_MODEL),
                  wspec(D_MODEL, EXPERT_FF), wspec(D_MODEL, EXPERT_FF), wspec(EXPERT_FF, D_MODEL),
                  pl.BlockSpec((1, D_MODEL), lambda i, e: (0, 0))],
        out_specs=tok(D_MODEL),
        out_shape=jax.ShapeDtypeStruct((T, D_MODEL), F32),
        scratch_shapes=[pltpu.VMEM((tm, D_MODEL), F32)],
        compiler_params=_cparams("parallel", "arbitrary"),
        name="expert_ffn",
    )(h, gate_te, x2, wg, wu, wd, gn.reshape(1, D_MODEL))


def _run_group(x, mem, W):
    B, L, _ = x.shape
    T = B * L
    xt = x.reshape(T, D_MODEL)
    proj = _inproj(xt, W['norm_mix'], W['w_in'], L)
    a = _diff_attention(proj, W['lam'], W['subln'], B, L)
    zc, x0, zb = _hyena_prep(proj, W['hy_conv_w'], W['hy_conv_b'], W['filt_bias'], B, L)
    tables = _dft_tables(L)
    kf = _filter_spectrum(L, W, tables)
    yc = _long_conv(zc, kf, tables, nb=1 if L >= 4096 else 4)
    x1 = _merge(xt, a, yc, x0, zb, proj, W['w_br_attn'], W['w_br_hyena'], W['w_out'], B, L)
    kv = _mem_kv(mem.reshape(B * N_MEM, D_MODEL), W['norm_mem'], W['w_xkv'])
    x2, hf, aff = _cross_attn(x1, kv, W['norm_x'], W['w_xq'], W['w_xo'], W['norm_ffn'],
                              W['w_router_t'], B, L)
    gate = _select(aff, EC_CAPACITY_FACTOR * T // N_EXPERTS)
    y = _expert_ffn(hf, gate.T, x2, W['w_exp_gate'], W['w_exp_up'], W['w_exp_down'], W['norm_final'])
    return y.reshape(B, L, D_MODEL)


def kernel(x_prompt, x_sample, mem_prompt, mem_sample, norm_mix, w_in, lambda_q1, lambda_k1, lambda_q2, lambda_k2, subln, w_br_attn, hy_conv_w, hy_conv_b, filt_w1, filt_b1, filt_w2, filt_b2, filt_w3, filt_freq, filt_bias, w_br_hyena, w_out, norm_x, norm_mem, w_xq, w_xkv, w_xo, norm_ffn, w_router, w_exp_gate, w_exp_up, w_exp_down, norm_final):
    l = 0
    lam = (jnp.exp(jnp.sum(lambda_q1[l] * lambda_k1[l])) - jnp.exp(jnp.sum(lambda_q2[l] * lambda_k2[l]))
           + LAM_INIT).astype(F32)
    W = dict(
        norm_mix=norm_mix[l], w_in=w_in[l].astype(BF16), lam=lam, subln=subln[l],
        w_br_attn=w_br_attn[l].astype(BF16), hy_conv_w=hy_conv_w[l], hy_conv_b=hy_conv_b[l],
        filt_w1=filt_w1[l], filt_b1=filt_b1[l], filt_w2=filt_w2[l], filt_b2=filt_b2[l],
        filt_w3=filt_w3[l], filt_freq=filt_freq[l], filt_bias=filt_bias[l],
        w_br_hyena=w_br_hyena[l].astype(BF16), w_out=w_out[l].astype(BF16),
        norm_x=norm_x[l], norm_mem=norm_mem[l], w_xq=w_xq[l].astype(BF16),
        w_xkv=w_xkv[l].astype(BF16), w_xo=w_xo[l].astype(BF16), norm_ffn=norm_ffn[l],
        w_router_t=w_router[l].T.astype(BF16), w_exp_gate=w_exp_gate[l].astype(BF16),
        w_exp_up=w_exp_up[l].astype(BF16), w_exp_down=w_exp_down[l].astype(BF16),
        norm_final=norm_final)
    return (_run_group(x_prompt, mem_prompt, W), _run_group(x_sample, mem_sample, W))
```

```python
import functools
import math

import jax
import jax.numpy as jnp
from jax import lax
from jax.experimental import pallas as pl
from jax.experimental.pallas import tpu as pltpu

F32 = jnp.float32
BF16 = jnp.bfloat16
I32 = jnp.int32

D_MODEL = 1024
N_ATT_HEADS = 4
ATT_HEAD_DIM = 64
ATT_V_DIM = 128
ROT_DIM = 16
ROPE_THETA = 500000.0
ATT_QK_W = 512
ATT_V_W = 512
HY_WIDTH = 512
FILTER_EMB = 33
FILTER_HIDDEN = 64
DECAY_TARGET = 1e-2
FAST_DECAY_PCT = 0.3
SLOW_DECAY_PCT = 1.5
IN_W = 5120
N_MEM = 256
X_HEADS = 4
X_HEAD_DIM = 256
N_EXPERTS = 16
EC_CAPACITY_FACTOR = 2
EXPERT_FF = 1024
EPS = 1e-6
LAM_INIT = 0.8 - 0.6 * math.exp(-0.3 * 0)

LANES = 128
ATT_ROW_GROUPS = 8
FFT_UNROLL = 4
VMEM_LIMIT_BYTES = 56 * 1024 * 1024


def _cparams(*sem):
    return pltpu.CompilerParams(dimension_semantics=sem, vmem_limit_bytes=VMEM_LIMIT_BYTES)


def _rms(x, g):
    return x * lax.rsqrt(jnp.mean(x * x, axis=-1, keepdims=True) + EPS) * g


def _inproj_kernel(x_ref, g_ref, w_ref, cos_ref, sa_ref, sb_ref, o_ref):
    u = _rms(x_ref[...], g_ref[...]).astype(BF16)
    n_col = IN_W // 512
    for j in range(n_col):
        acc = jnp.dot(u, w_ref[:, j * 512:(j + 1) * 512], preferred_element_type=F32)
        if j < 2:
            parts = []
            for h in range(4):
                t = acc[:, h * LANES:(h + 1) * LANES]
                r = (t * cos_ref[...] + pltpu.roll(t, 8, 1) * sa_ref[...]
                     + pltpu.roll(t, LANES - 8, 1) * sb_ref[...])
                parts.append(r)
            acc = jnp.concatenate(parts, axis=1)
        if j == 0:
            acc = acc * (ATT_HEAD_DIM ** -0.5 * math.log2(math.e))
        o_ref[:, j * 512:(j + 1) * 512] = acc.astype(BF16)


def _rope_tables(L):
    inv = ROPE_THETA ** (-jnp.arange(0, ROT_DIM, 2, dtype=F32) / ROT_DIM)
    pos = jnp.arange(L, dtype=F32)
    ang = pos[:, None] * inv[None, :]
    cos, sin = jnp.cos(ang), jnp.sin(ang)
    one = jnp.ones((L, 48), F32)
    zero8 = jnp.zeros((L, 8), F32)
    zero48 = jnp.zeros((L, 48), F32)
    c64 = jnp.concatenate([cos, cos, one], axis=1)
    sa64 = jnp.concatenate([zero8, sin, zero48], axis=1)
    sb64 = jnp.concatenate([-sin, zero8, zero48], axis=1)
    tile2 = lambda a: jnp.concatenate([a, a], axis=1)
    return tile2(c64), tile2(sa64), tile2(sb64)


def _inproj(xt, g, w_in_bf, L, tm=512):
    T = xt.shape[0]
    cos, sa, sb = _rope_tables(L)
    nl = L // tm
    tab = pl.BlockSpec((tm, LANES), lambda i: (i % nl, 0))
    return pl.pallas_call(
        _inproj_kernel,
        grid=(T // tm,),
        in_specs=[pl.BlockSpec((tm, D_MODEL), lambda i: (i, 0)),
                  pl.BlockSpec((1, D_MODEL), lambda i: (0, 0)),
                  pl.BlockSpec((D_MODEL, IN_W), lambda i: (0, 0)),
                  tab, tab, tab],
        out_specs=pl.BlockSpec((tm, IN_W), lambda i: (i, 0)),
        out_shape=jax.ShapeDtypeStruct((T, IN_W), BF16),
        compiler_params=_cparams("parallel"),
        name="inproj",
    )(xt, g.reshape(1, D_MODEL), w_in_bf, cos, sa, sb)


def _attn_kernel(lam_ref, q_ref, k_ref, v_ref, sg_ref, o_ref, vx_ref, *, seq, kc):
    tq = q_ref.shape[0]

    @pl.when(pl.program_id(2) == 0)
    def _():
        lane = lax.broadcasted_iota(I32, (seq, LANES), 1)
        vx_ref[:, :LANES] = v_ref[...]
        vx_ref[:, LANES:] = jnp.where(lane == 0, 1.0, 0.0).astype(BF16)

    q = q_ref[...]
    lane = lax.broadcasted_iota(I32, q.shape, 1)
    zero = jnp.zeros_like(q)
    qq = jnp.concatenate([jnp.where(lane < ATT_HEAD_DIM, q, zero),
                          jnp.where(lane >= ATT_HEAD_DIM, q, zero)], axis=0)

    rows = 2 * tq // ATT_ROW_GROUPS
    groups = [qq[g * rows:(g + 1) * rows] for g in range(ATT_ROW_GROUPS)]

    def scores(qg, c):
        start = pl.multiple_of(c * kc, kc)
        s = lax.dot_general(qg, k_ref[pl.ds(start, kc), :], (((1,), (1,)), ((), ())),
                            preferred_element_type=F32)
        return s, vx_ref[pl.ds(start, kc), :]

    if seq == kc:
        accs = []
        for qg in groups:
            s, vx = scores(qg, 0)
            p = jnp.exp2(s - jnp.max(s, axis=1, keepdims=True))
            accs.append(jnp.dot(p.astype(BF16), vx, preferred_element_type=F32))
    else:
        def body(c, carry):
            out = []
            for qg, (m, acc) in zip(groups, carry):
                s, vx = scores(qg, c)
                m_new = jnp.maximum(m, jnp.max(s, axis=1, keepdims=True))
                p = jnp.exp2(s - m_new)
                pv = jnp.dot(p.astype(BF16), vx, preferred_element_type=F32)
                out.append((m_new, jnp.exp2(m - m_new) * acc + pv))
            return tuple(out)

        init = tuple((jnp.full((rows, 1), -jnp.inf, F32), jnp.zeros((rows, 2 * LANES), F32))
                     for _ in groups)
        accs = [a for _, a in lax.fori_loop(0, seq // kc, body, init)]
    acc = jnp.concatenate(accs, axis=0)
    o = acc[:, :ATT_V_DIM] / acc[:, ATT_V_DIM:ATT_V_DIM + 1]
    a = o[:tq] - lam_ref[0] * o[tq:]
    a = _rms(a, sg_ref[...]) * (1.0 - LAM_INIT)
    o_ref[...] = a.astype(BF16)


def _diff_attention(proj, lam, subln, B, L):
    T = B * L
    tq = min(1024, L)
    nq = L // tq
    kc = min(2048, L)
    return pl.pallas_call(
        functools.partial(_attn_kernel, seq=L, kc=kc),
        grid=(B, N_ATT_HEADS, nq),
        in_specs=[pl.BlockSpec(memory_space=pltpu.SMEM),
                  pl.BlockSpec((tq, LANES), lambda b, h, i: (b * nq + i, h)),
                  pl.BlockSpec((L, LANES), lambda b, h, i: (b, 4 + h)),
                  pl.BlockSpec((L, LANES), lambda b, h, i: (b, 8 + h)),
                  pl.BlockSpec((1, ATT_V_DIM), lambda b, h, i: (0, 0))],
        out_specs=pl.BlockSpec((tq, ATT_V_DIM), lambda b, h, i: (b * nq + i, h)),
        out_shape=jax.ShapeDtypeStruct((T, ATT_V_W), BF16),
        scratch_shapes=[pltpu.VMEM((L, 2 * LANES), BF16)],
        compiler_params=_cparams("parallel", "parallel", "arbitrary"),
        name="diff_attn",
    )(lam.reshape(1), proj, proj, proj, subln.reshape(1, ATT_V_DIM))


def _hyprep_kernel(hy_ref, prev_ref, next_ref, w_ref, b_ref, fb_ref, zc_ref, x0_ref, zb_ref):
    i = pl.program_id(1)
    n = pl.num_programs(1)
    hy = hy_ref[...].astype(F32)
    tl = hy.shape[0]
    prev_row = jnp.where(i == 0, 0.0, prev_ref[15:16, :].astype(F32))
    next_row = jnp.where(i == n - 1, 0.0, next_ref[0:1, :].astype(F32))
    row = lax.broadcasted_iota(I32, hy.shape, 0)
    up = jnp.where(row == 0, prev_row, pltpu.roll(hy, 1, 0))
    dn = jnp.where(row == tl - 1, next_row, pltpu.roll(hy, tl - 1, 0))
    conv = up * w_ref[0:1, :] + hy * w_ref[1:2, :] + dn * w_ref[2:3, :] + b_ref[...]
    x0 = conv[:, :HY_WIDTH]
    x1 = conv[:, HY_WIDTH:2 * HY_WIDTH]
    hv = conv[:, 2 * HY_WIDTH:]
    z = hv * x1
    for c in range(HY_WIDTH // LANES):
        zc_ref[0, c] = z[:, c * LANES:(c + 1) * LANES].astype(zc_ref.dtype)
    x0_ref[...] = x0.astype(x0_ref.dtype)
    zb_ref[...] = (z * fb_ref[...] * x0).astype(zb_ref.dtype)


def _hyena_prep(proj, conv_w, conv_b, filt_bias, B, L, tl=512):
    T = B * L
    nl = L // tl
    hw = 3 * HY_WIDTH
    rb = tl // 16
    nrb = T // 16
    return pl.pallas_call(
        _hyprep_kernel,
        grid=(B, nl),
        in_specs=[pl.BlockSpec((tl, hw), lambda b, i: (b * nl + i, 1)),
                  pl.BlockSpec((16, hw), lambda b, i: (jnp.maximum((b * nl + i) * rb - 1, 0), 1)),
                  pl.BlockSpec((16, hw), lambda b, i: (jnp.minimum((b * nl + i + 1) * rb, nrb - 1), 1)),
                  pl.BlockSpec((3, hw), lambda b, i: (0, 0)),
                  pl.BlockSpec((1, hw), lambda b, i: (0, 0)),
                  pl.BlockSpec((1, HY_WIDTH), lambda b, i: (0, 0))],
        out_specs=[pl.BlockSpec((1, HY_WIDTH // LANES, tl, LANES), lambda b, i: (b, 0, i, 0)),
                   pl.BlockSpec((tl, HY_WIDTH), lambda b, i: (b * nl + i, 0)),
                   pl.BlockSpec((tl, HY_WIDTH), lambda b, i: (b * nl + i, 0))],
        out_shape=[jax.ShapeDtypeStruct((B, HY_WIDTH // LANES, L, LANES), BF16),
                   jax.ShapeDtypeStruct((T, HY_WIDTH), BF16),
                   jax.ShapeDtypeStruct((T, HY_WIDTH), BF16)],
        compiler_params=_cparams("parallel", "parallel"),
        name="hyena_prep",
    )(proj, proj, proj, conv_w, conv_b.reshape(1, hw), filt_bias.reshape(1, HY_WIDTH))


def _fft_dims(L):
    n = 2 * L
    n1 = 1 << ((n.bit_length() - 1 + 1) // 2)
    n2 = n // n1
    A = n1 // 2
    K1 = n1 // 2 + 1
    K1p = -(-K1 // 8) * 8
    KP = -(-2 * K1p // LANES) * LANES
    assert n1 * n2 == n and A * n2 == L and n2 % 8 == 0 and A % 16 == 0
    return n, n1, n2, A, K1, K1p, KP


def _dft_tables(L):
    n, n1, n2, A, K1, K1p, KP = _fft_dims(L)
    b = jnp.arange(n2, dtype=I32)
    a = jnp.arange(A, dtype=I32)
    k1 = jnp.arange(K1p, dtype=I32)
    t = a[None, :] * n2 + b[:, None]
    m = (k1[None, :, None] * t[:, None, :]) % n
    th = m.astype(F32) * (2.0 * math.pi / n)
    valid = (k1 < K1).astype(F32)[None, :, None]
    c, s = jnp.cos(th) * valid, jnp.sin(th) * valid
    g_fwd = jnp.concatenate([c, -s], axis=1).astype(BF16)
    w = jnp.where((k1 == 0) | (k1 == n1 // 2), 1.0, 2.0)[None, :, None] / n
    ci = jnp.transpose(c * w, (0, 2, 1))
    si = jnp.transpose(-s * w, (0, 2, 1))
    pad = jnp.zeros((n2, A, KP - 2 * K1p), F32)
    g_inv = jnp.concatenate([ci, si, pad], axis=2).astype(BF16)
    k2 = jnp.arange(n2, dtype=I32)
    ph = ((k2[:, None] * b[None, :]) % n2).astype(F32) * (2.0 * math.pi / n2)
    c2, s2 = jnp.cos(ph), jnp.sin(ph)
    f2 = jnp.concatenate([jnp.concatenate([c2, s2], 1), jnp.concatenate([-s2, c2], 1)], 0).astype(BF16)
    f2i = jnp.concatenate([jnp.concatenate([c2, -s2], 1), jnp.concatenate([s2, c2], 1)], 0).astype(BF16)
    return g_fwd, g_inv, f2, f2i


def _fft_rows_fwd(x_ref, g_ref, yf_ref, *, nb, n2, K1p):
    def body(b, _):
        off = pl.multiple_of(b * LANES, LANES)
        xb = jnp.concatenate([x_ref[i, :, pl.ds(off, LANES)] for i in range(nb)], axis=1)
        y = jnp.dot(g_ref[b], xb, preferred_element_type=F32)
        for i in range(nb):
            for r in range(2 * K1p // 8):
                row = pl.multiple_of((r * n2 + b) * 8, 8)
                yf_ref[i, pl.ds(row, 8), :] = y[r * 8:(r + 1) * 8, i * LANES:(i + 1) * LANES]
        return 0
    lax.fori_loop(0, n2, body, 0, unroll=FFT_UNROLL)


def _fft_cols_fwd(yf_ref, f2_ref, k1, *, nb, n2, K1p):
    kt, ks = k1 // 8, k1 % 8
    parts = []
    for i in range(nb):
        re = yf_ref[i, pl.ds(kt * (n2 * 8) + ks, n2, stride=8), :]
        im = yf_ref[i, pl.ds((K1p // 8 + kt) * (n2 * 8) + ks, n2, stride=8), :]
        parts.append(jnp.concatenate([re, im], axis=0))
    yk = jnp.concatenate(parts, axis=1).astype(BF16)
    return jnp.dot(f2_ref[...], yk, preferred_element_type=F32)


def _kspec_kernel(fb_ref, l1_ref, g_ref, f2_ref, kf_ref, yf_ref, *, n2, K1p):
    _fft_rows_fwd(fb_ref, g_ref, yf_ref, nb=2, n2=n2, K1p=K1p)
    inv = 1.0 / l1_ref[...]

    def body(k1, _):
        z = _fft_cols_fwd(yf_ref, f2_ref, k1, nb=2, n2=n2, K1p=K1p)
        zf, zb = z[:, :LANES], z[:, LANES:]
        kr = (zf[:n2] + zb[:n2]) * inv
        ki = (zf[n2:] - zb[n2:]) * inv
        kf_ref[k1] = jnp.concatenate([kr, ki], axis=0).astype(kf_ref.dtype)
        return 0
    lax.fori_loop(0, K1p, body, 0, unroll=FFT_UNROLL)


def _lconv_kernel(x_ref, g_ref, f2_ref, f2i_ref, kf_ref, gi_ref, o_ref, yf_ref, wf_ref, *,
                  nb, n2, K1p, KP):
    _fft_rows_fwd(x_ref, g_ref, yf_ref, nb=nb, n2=n2, K1p=K1p)

    def kbody(k1, _):
        z = _fft_cols_fwd(yf_ref, f2_ref, k1, nb=nb, n2=n2, K1p=K1p)
        kf = kf_ref[k1].astype(F32)
        kr = jnp.concatenate([kf[:n2]] * nb, axis=1)
        ki = jnp.concatenate([kf[n2:]] * nb, axis=1)
        zr, zi = z[:n2], z[n2:]
        p = jnp.concatenate([zr * kr - zi * ki, zr * ki + zi * kr], axis=0).astype(BF16)
        v = jnp.dot(f2i_ref[...], p, preferred_element_type=F32)
        for i in range(nb):
            for r in range(2 * n2 // 8):
                row = pl.multiple_of((r * K1p + k1) * 8, 8)
                wf_ref[i, pl.ds(row, 8), :] = v[r * 8:(r + 1) * 8, i * LANES:(i + 1) * LANES]
        return 0
    lax.fori_loop(0, K1p, kbody, 0, unroll=FFT_UNROLL)

    def bbody(b, _):
        bt, bs = b // 8, b % 8
        parts = []
        for i in range(nb):
            re = wf_ref[i, pl.ds(bt * (K1p * 8) + bs, K1p, stride=8), :]
            im = wf_ref[i, pl.ds((n2 // 8 + bt) * (K1p * 8) + bs, K1p, stride=8), :]
            parts.append(jnp.concatenate([re, im, jnp.zeros((KP - 2 * K1p, LANES), F32)], axis=0))
        vb = jnp.concatenate(parts, axis=1).astype(BF16)
        ob = jnp.dot(gi_ref[b], vb, preferred_element_type=F32)
        off = pl.multiple_of(b * LANES, LANES)
        for i in range(nb):
            o_ref[i, :, pl.ds(off, LANES)] = ob[:, i * LANES:(i + 1) * LANES].astype(o_ref.dtype)
        return 0
    lax.fori_loop(0, n2, bbody, 0, unroll=FFT_UNROLL)


def _filter_kernel(w1_ref, b1_ref, w2_ref, b2_ref, w3_ref, fr_ref, fb_ref, l1_ref, *, seq):
    i = pl.program_id(0)
    tl = fb_ref.shape[2]
    bands = (FILTER_EMB - 1) // 2
    pos = (i * tl + lax.broadcasted_iota(I32, (tl, 1), 0)).astype(F32)
    t = pos * (1.0 / (seq - 1))
    wpos = pos * (2.0 * math.pi / seq)
    lane = lax.broadcasted_iota(I32, (1, LANES), 1)
    band = jnp.where(lane <= bands, lane - 1, lane - 1 - bands).astype(F32)
    fvec = 1e-4 + band * ((bands - 1 - 1e-4) / (bands - 1))
    arg = wpos * fvec
    z = jnp.where(lane == 0, t, jnp.where(lane <= bands, jnp.cos(arg),
                                          jnp.where(lane <= 2 * bands, -jnp.sin(arg), 0.0)))
    fr = fr_ref[...]
    h = jnp.sin(fr * (jnp.dot(z, w1_ref[...], preferred_element_type=F32) + b1_ref[...]))
    h = jnp.sin(fr * (jnp.dot(h, w2_ref[...], preferred_element_type=F32) + b2_ref[...]))
    h = jnp.dot(h, w3_ref[...], preferred_element_type=F32)
    max_decay = math.log(DECAY_TARGET) / FAST_DECAY_PCT
    min_decay = math.log(DECAY_TARGET) / SLOW_DECAY_PCT
    ch = lax.broadcasted_iota(I32, (1, HY_WIDTH), 1).astype(F32)
    deltas = jnp.abs(min_decay + ch * ((max_decay - min_decay) / (HY_WIDTH - 1)))
    decay = jnp.exp(-t * deltas)
    fwd = h[:, :HY_WIDTH] * decay
    bwd = jnp.where(pos == 0.0, 0.0, h[:, HY_WIDTH:] * decay)
    for c in range(HY_WIDTH // LANES):
        fb_ref[0, c] = fwd[:, c * LANES:(c + 1) * LANES].astype(fb_ref.dtype)
        fb_ref[1, c] = bwd[:, c * LANES:(c + 1) * LANES].astype(fb_ref.dtype)
    part = jnp.sum(jnp.abs(fwd) + jnp.abs(bwd), axis=0, keepdims=True)

    @pl.when(i == 0)
    def _():
        l1_ref[...] = jnp.zeros_like(l1_ref)
    l1_ref[...] += part


def _filter_spectrum(L, W, tables, tl=512):
    n, n1, n2, A, K1, K1p, KP = _fft_dims(L)
    g_fwd, _, f2, _ = tables
    nc = HY_WIDTH // LANES
    w1 = jnp.pad(W['filt_w1'], ((0, LANES - FILTER_EMB), (0, 0)))
    full = lambda r, c: pl.BlockSpec((r, c), lambda i: (0, 0))
    fb, l1 = pl.pallas_call(
        functools.partial(_filter_kernel, seq=L),
        grid=(L // tl,),
        in_specs=[full(LANES, FILTER_HIDDEN), full(1, FILTER_HIDDEN),
                  full(FILTER_HIDDEN, FILTER_HIDDEN), full(1, FILTER_HIDDEN),
                  full(FILTER_HIDDEN, 2 * HY_WIDTH), full(1, FILTER_HIDDEN)],
        out_specs=[pl.BlockSpec((2, nc, tl, LANES), lambda i: (0, 0, i, 0)),
                   pl.BlockSpec((1, HY_WIDTH), lambda i: (0, 0))],
        out_shape=[jax.ShapeDtypeStruct((2, nc, L, LANES), BF16),
                   jax.ShapeDtypeStruct((1, HY_WIDTH), F32)],
        compiler_params=_cparams("arbitrary"),
        name="hyena_filter",
    )(w1, W['filt_b1'].reshape(1, -1), W['filt_w2'], W['filt_b2'].reshape(1, -1), W['filt_w3'],
      W['filt_freq'].reshape(1, -1))
    fbr = fb.reshape(2, nc, A, n2 * LANES)
    rows = (2 * K1p // 8) * n2 * 8
    return pl.pallas_call(
        functools.partial(_kspec_kernel, n2=n2, K1p=K1p),
        grid=(nc,),
        in_specs=[pl.BlockSpec((2, None, A, n2 * LANES), lambda c: (0, c, 0, 0)),
                  pl.BlockSpec((1, LANES), lambda c: (0, c)),
                  pl.BlockSpec((n2, 2 * K1p, A), lambda c: (0, 0, 0)),
                  pl.BlockSpec((2 * n2, 2 * n2), lambda c: (0, 0))],
        out_specs=pl.BlockSpec((None, K1p, 2 * n2, LANES), lambda c: (c, 0, 0, 0)),
        out_shape=jax.ShapeDtypeStruct((nc, K1p, 2 * n2, LANES), BF16),
        scratch_shapes=[pltpu.VMEM((2, rows, LANES), F32)],
        compiler_params=_cparams("arbitrary"),
        name="filter_spectrum",
    )(fbr, l1, g_fwd, f2)


def _long_conv(zc, kf, tables, nb):
    B, nc, L, _ = zc.shape
    n, n1, n2, A, K1, K1p, KP = _fft_dims(L)
    g_fwd, g_inv, f2, f2i = tables
    x = zc.reshape(B, nc, A, n2 * LANES)
    rows_y = (2 * K1p // 8) * n2 * 8
    rows_w = (2 * n2 // 8) * K1p * 8
    const = lambda shape: pl.BlockSpec(shape, lambda c, j: (0,) * len(shape),
                                       pipeline_mode=pl.Buffered(1))
    y = pl.pallas_call(
        functools.partial(_lconv_kernel, nb=nb, n2=n2, K1p=K1p, KP=KP),
        grid=(nc, B // nb),
        in_specs=[pl.BlockSpec((nb, None, A, n2 * LANES), lambda c, j: (j, c, 0, 0)),
                  const((n2, 2 * K1p, A)), const((2 * n2, 2 * n2)), const((2 * n2, 2 * n2)),
                  pl.BlockSpec((None, K1p, 2 * n2, LANES), lambda c, j: (c, 0, 0, 0)),
                  const((n2, A, KP))],
        out_specs=pl.BlockSpec((nb, None, A, n2 * LANES), lambda c, j: (j, c, 0, 0)),
        out_shape=jax.ShapeDtypeStruct((B, nc, A, n2 * LANES), BF16),
        scratch_shapes=[pltpu.VMEM((nb, rows_y, LANES), F32), pltpu.VMEM((nb, rows_w, LANES), F32)],
        compiler_params=_cparams("arbitrary", "arbitrary"),
        name="long_conv",
    )(x, g_fwd, f2, f2i, kf, g_inv)
    return y.reshape(B, nc, L, LANES)


def _merge_kernel(x_ref, a_ref, y_ref, x0_ref, zb_ref, g0_ref, g1_ref, wa_ref, wh_ref, wo_ref, o_ref):
    a = jnp.dot(a_ref[...], wa_ref[...], preferred_element_type=F32)
    x0 = x0_ref[...].astype(F32)
    y = jnp.concatenate([y_ref[0, c].astype(F32) for c in range(HY_WIDTH // LANES)], axis=1)
    hz = (y * x0 + zb_ref[...].astype(F32)).astype(BF16)
    hzp = jnp.dot(hz, wh_ref[...], preferred_element_type=F32)
    g0 = jax.nn.sigmoid(g0_ref[...].astype(F32))
    g1 = jax.nn.sigmoid(g1_ref[...].astype(F32))
    m = (g0 * a + g1 * hzp).astype(BF16)
    o_ref[...] = x_ref[...] + jnp.dot(m, wo_ref[...], preferred_element_type=F32)


def _merge(xt, a, yc, x0, zb, proj, wa, wh, wo, B, L, tm=512):
    T = B * L
    nl = L // tm
    full = lambda r, c: pl.BlockSpec((r, c), lambda b, i: (0, 0))
    tok = lambda c, cb=0: pl.BlockSpec((tm, c), lambda b, i, cb=cb: (b * nl + i, cb))
    return pl.pallas_call(
        _merge_kernel,
        grid=(B, nl),
        in_specs=[tok(D_MODEL), tok(ATT_V_W),
                  pl.BlockSpec((1, HY_WIDTH // LANES, tm, LANES), lambda b, i: (b, 0, i, 0)),
                  tok(HY_WIDTH), tok(HY_WIDTH), tok(D_MODEL, 3), tok(D_MODEL, 4),
                  full(ATT_V_W, D_MODEL), full(HY_WIDTH, D_MODEL), full(D_MODEL, D_MODEL)],
        out_specs=tok(D_MODEL),
        out_shape=jax.ShapeDtypeStruct((T, D_MODEL), F32),
        compiler_params=_cparams("parallel", "parallel"),
        name="merge",
    )(xt, a, yc, x0, zb, proj, proj, wa, wh, wo)


def _memkv_kernel(m_ref, g_ref, w_ref, o_ref):
    u = _rms(m_ref[...], g_ref[...]).astype(BF16)
    o_ref[...] = jnp.dot(u, w_ref[...], preferred_element_type=F32).astype(BF16)


def _mem_kv(mem_t, g, w_xkv_bf):
    R = mem_t.shape[0]
    return pl.pallas_call(
        _memkv_kernel,
        grid=(R // N_MEM,),
        in_specs=[pl.BlockSpec((N_MEM, D_MODEL), lambda i: (i, 0)),
                  pl.BlockSpec((1, D_MODEL), lambda i: (0, 0)),
                  pl.BlockSpec((D_MODEL, 2 * D_MODEL), lambda i: (0, 0))],
        out_specs=pl.BlockSpec((N_MEM, 2 * D_MODEL), lambda i: (i, 0)),
        out_shape=jax.ShapeDtypeStruct((R, 2 * D_MODEL), BF16),
        compiler_params=_cparams("parallel"),
        name="mem_kv",
    )(mem_t, g.reshape(1, D_MODEL), w_xkv_bf)


def _xattn_kernel(x_ref, kv_ref, gx_ref, wq_ref, wo_ref, gf_ref, wr_ref, x2_ref, h_ref, aff_ref):
    x = x_ref[...]
    u = _rms(x, gx_ref[...]).astype(BF16)
    q = jnp.dot(u, wq_ref[...], preferred_element_type=F32)
    q = (q * (X_HEAD_DIM ** -0.5)).astype(BF16)
    outs = []
    for h in range(X_HEADS):
        qh = q[:, h * X_HEAD_DIM:(h + 1) * X_HEAD_DIM]
        kh = kv_ref[:, h * X_HEAD_DIM:(h + 1) * X_HEAD_DIM]
        vh = kv_ref[:, D_MODEL + h * X_HEAD_DIM:D_MODEL + (h + 1) * X_HEAD_DIM]
        s = lax.dot_general(qh, kh, (((1,), (1,)), ((), ())), preferred_element_type=F32)
        s = s - jnp.max(s, axis=1, keepdims=True)
        p = jnp.exp(s)
        p = p / jnp.sum(p, axis=1, keepdims=True)
        outs.append(jnp.dot(p.astype(BF16), vh, preferred_element_type=F32))
    o = jnp.concatenate(outs, axis=1).astype(BF16)
    x2 = x + jnp.dot(o, wo_ref[...], preferred_element_type=F32)
    x2_ref[...] = x2
    hf = _rms(x2, gf_ref[...]).astype(BF16)
    h_ref[...] = hf
    lg = lax.dot_general(wr_ref[...], hf, (((1,), (1,)), ((), ())), preferred_element_type=F32)
    lg = lg - jnp.max(lg, axis=0, keepdims=True)
    e = jnp.exp(lg)
    aff_ref[...] = e / jnp.sum(e, axis=0, keepdims=True)


def _cross_attn(x1, kv, gx, wq, wo, gf, wr_t, B, L, tm=512):
    T = B * L
    nl = L // tm
    full = lambda r, c: pl.BlockSpec((r, c), lambda b, i: (0, 0))
    tok = lambda c: pl.BlockSpec((tm, c), lambda b, i: (b * nl + i, 0))
    return pl.pallas_call(
        _xattn_kernel,
        grid=(B, nl),
        in_specs=[tok(D_MODEL),
                  pl.BlockSpec((N_MEM, 2 * D_MODEL), lambda b, i: (b, 0)),
                  full(1, D_MODEL), full(D_MODEL, D_MODEL), full(D_MODEL, D_MODEL),
                  full(1, D_MODEL), full(N_EXPERTS, D_MODEL)],
        out_specs=[tok(D_MODEL), tok(D_MODEL),
                   pl.BlockSpec((N_EXPERTS, tm), lambda b, i: (0, b * nl + i))],
        out_shape=[jax.ShapeDtypeStruct((T, D_MODEL), F32),
                   jax.ShapeDtypeStruct((T, D_MODEL), BF16),
                   jax.ShapeDtypeStruct((N_EXPERTS, T), F32)],
        compiler_params=_cparams("parallel", "parallel"),
        name="cross_attn",
    )(x1, kv, gx.reshape(1, D_MODEL), wq, wo, gf.reshape(1, D_MODEL), wr_t)


def _select_kernel(aff_ref, gate_ref, *, cap):
    aff = aff_ref[...]
    bits = pltpu.bitcast(aff, I32)
    T = aff.shape[1]

    def count(mask):
        return jnp.sum(mask.astype(I32), axis=1, keepdims=True)

    def vbody(k, thr):
        cand = thr | (jnp.int32(1) << (30 - k))
        return jnp.where(count(bits >= cand) >= cap, cand, thr)

    thr = lax.fori_loop(0, 31, vbody, jnp.zeros((N_EXPERTS, 1), I32))
    gt = bits > thr
    eq = bits == thr
    need = cap - count(gt)
    tok = lax.broadcasted_iota(I32, aff.shape, 1)
    nbit = int(T).bit_length()

    def jbody(k, j):
        cand = j + (jnp.int32(1) << (nbit - 1 - k))
        return jnp.where(count(eq & (tok < cand)) <= need, cand, j)

    j = lax.fori_loop(0, nbit, jbody, jnp.zeros((N_EXPERTS, 1), I32))
    sel = gt | (eq & (tok < j))
    gate_ref[...] = jnp.where(sel, aff, 0.0)


def _select(aff_t, cap):
    E, T = aff_t.shape
    return pl.pallas_call(
        functools.partial(_select_kernel, cap=cap),
        grid=(1,),
        in_specs=[pl.BlockSpec((E, T), lambda i: (0, 0))],
        out_specs=pl.BlockSpec((E, T), lambda i: (0, 0)),
        out_shape=jax.ShapeDtypeStruct((E, T), F32),
        compiler_params=_cparams("arbitrary"),
        name="ec_select",
    )(aff_t)


EC_TILE = 256
EC_SLOTS = 48
EC_OVER_TILE = 2048


def _rank_kernel(g_ref, rank_ref, cnt_ref):
    n = g_ref.shape[1]
    sel = jnp.where(g_ref[...] > 0.0, 1.0, 0.0).astype(BF16)
    r = lax.broadcasted_iota(I32, (n, n), 0)
    c = lax.broadcasted_iota(I32, (n, n), 1)
    before = jnp.where(r < c, 1.0, 0.0).astype(BF16)
    rank_ref[...] = jnp.dot(sel, before, preferred_element_type=F32)
    cnt_ref[...] = jnp.dot(sel, jnp.ones((n, LANES), BF16), preferred_element_type=F32)


def _tile_ranks(gate):
    E, T = gate.shape
    rows = E * T // EC_TILE
    rb = min(512, rows)
    rank, cnt = pl.pallas_call(
        _rank_kernel,
        grid=(rows // rb,),
        in_specs=[pl.BlockSpec((rb, EC_TILE), lambda i: (i, 0))],
        out_specs=[pl.BlockSpec((rb, EC_TILE), lambda i: (i, 0)),
                   pl.BlockSpec((rb, LANES), lambda i: (i, 0))],
        out_shape=[jax.ShapeDtypeStruct((rows, EC_TILE), F32),
                   jax.ShapeDtypeStruct((rows, LANES), F32)],
        compiler_params=_cparams("parallel"),
        name="ec_rank",
    )(gate.reshape(rows, EC_TILE))
    return rank.reshape(E, T), cnt[:, 0].reshape(E, T // EC_TILE)


def _dispatch_kernel(h_ref, rank_ref, gate_ref, xe_ref):
    slot = lax.broadcasted_iota(I32, (EC_SLOTS, EC_TILE), 0).astype(F32)
    parts = []
    for e in range(N_EXPERTS):
        hit = (rank_ref[e:e + 1, :] == slot) & (gate_ref[e:e + 1, :] > 0.0)
        parts.append(jnp.where(hit, 1.0, 0.0).astype(BF16))
    onehot = jnp.concatenate(parts, axis=0)
    xe = jnp.dot(onehot, h_ref[...], preferred_element_type=F32).astype(BF16)
    xe_ref[...] = xe.reshape(N_EXPERTS, EC_SLOTS, D_MODEL)


def _dispatch(h, rank, gate):
    T = h.shape[0]
    nt = T // EC_TILE
    return pl.pallas_call(
        _dispatch_kernel,
        grid=(nt,),
        in_specs=[pl.BlockSpec((EC_TILE, D_MODEL), lambda i: (i, 0)),
                  pl.BlockSpec((N_EXPERTS, EC_TILE), lambda i: (0, i)),
                  pl.BlockSpec((N_EXPERTS, EC_TILE), lambda i: (0, i))],
        out_specs=pl.BlockSpec((N_EXPERTS, None, EC_SLOTS, D_MODEL), lambda i: (0, i, 0, 0)),
        out_shape=jax.ShapeDtypeStruct((N_EXPERTS, nt, EC_SLOTS, D_MODEL), BF16),
        compiler_params=_cparams("parallel"),
        name="ec_dispatch",
    )(h, rank, gate)


def _swiglu(h, wg_ref, wu_ref, wd_ref):
    half = EXPERT_FF // 2
    y = None
    for c in range(2):
        sl = slice(c * half, (c + 1) * half)
        a = jnp.dot(h, wg_ref[0, :, sl], preferred_element_type=F32)
        b = jnp.dot(h, wu_ref[0, :, sl], preferred_element_type=F32)
        he = (jax.nn.silu(a) * b).astype(BF16)
        d = jnp.dot(he, wd_ref[0, sl, :], preferred_element_type=F32)
        y = d if y is None else y + d
    return y


def _expert_rows_kernel(x_ref, wg_ref, wu_ref, wd_ref, o_ref):
    o_ref[0] = _swiglu(x_ref[0], wg_ref, wu_ref, wd_ref).astype(o_ref.dtype)


def _expert_rows(xe, wg, wu, wd):
    E, R, _ = xe.shape
    rt = math.gcd(R, 768)
    rows = pl.BlockSpec((1, rt, D_MODEL), lambda e, j: (e, j, 0))
    wspec = lambda r, c: pl.BlockSpec((1, r, c), lambda e, j: (e, 0, 0))
    return pl.pallas_call(
        _expert_rows_kernel,
        grid=(E, R // rt),
        in_specs=[rows, wspec(D_MODEL, EXPERT_FF), wspec(D_MODEL, EXPERT_FF), wspec(EXPERT_FF, D_MODEL)],
        out_specs=rows,
        out_shape=jax.ShapeDtypeStruct((E, R, D_MODEL), BF16),
        compiler_params=_cparams("parallel", "parallel"),
        name="expert_rows",
    )(xe, wg, wu, wd)


def _overflow_kernel(flag_ref, h_ref, rank_ref, gate_ref, wg_ref, wu_ref, wd_ref, o_ref):
    i, e = pl.program_id(0), pl.program_id(1)

    @pl.when(e == 0)
    def _():
        o_ref[...] = jnp.zeros_like(o_ref)

    @pl.when(flag_ref[i * N_EXPERTS + e] > 0)
    def _():
        lane = lax.broadcasted_iota(I32, gate_ref.shape, 1)
        keep = (lane == e) & (rank_ref[...] >= float(EC_SLOTS))
        g = jnp.sum(jnp.where(keep, gate_ref[...], 0.0), axis=1, keepdims=True)
        o_ref[...] += _swiglu(h_ref[...], wg_ref, wu_ref, wd_ref) * g


def _overflow_ffn(h, rank_t, gate_t, flags, wg, wu, wd):
    T = h.shape[0]
    tm = min(EC_OVER_TILE, T)
    tok = lambda c: pl.BlockSpec((tm, c), lambda i, e, f: (i, 0))
    wspec = lambda r, c: pl.BlockSpec(
        (1, r, c), lambda i, e, f: (jnp.where(f[i * N_EXPERTS + e] > 0, e, 0), 0, 0))
    return pl.pallas_call(
        _overflow_kernel,
        grid_spec=pltpu.PrefetchScalarGridSpec(
            num_scalar_prefetch=1,
            grid=(T // tm, N_EXPERTS),
            in_specs=[tok(D_MODEL), tok(N_EXPERTS), tok(N_EXPERTS),
                      wspec(D_MODEL, EXPERT_FF), wspec(D_MODEL, EXPERT_FF), wspec(EXPERT_FF, D_MODEL)],
            out_specs=tok(D_MODEL)),
        out_shape=jax.ShapeDtypeStruct((T, D_MODEL), F32),
        compiler_params=_cparams("parallel", "arbitrary"),
        name="expert_overflow",
    )(flags, h, rank_t, gate_t, wg, wu, wd)


def _combine_kernel(ye_ref, rank_ref, gate_ref, ex_ref, sl_ref, x_ref, yo_ref, gn_ref, o_ref):
    ye = ye_ref[...].reshape(N_EXPERTS * EC_SLOTS, D_MODEL)
    r = jnp.dot(rank_ref[...].astype(BF16), ex_ref[...], preferred_element_type=F32)
    g = jnp.dot(gate_ref[...].astype(BF16), ex_ref[...], preferred_element_type=F32)
    q = jnp.where(r == sl_ref[...], g, 0.0).astype(BF16)
    y = jnp.dot(q, ye, preferred_element_type=F32)
    o_ref[...] = _rms(x_ref[...] + yo_ref[...] + y, gn_ref[...])


def _combine(ye, rank_t, gate_t, x2, y_over, gn):
    T = x2.shape[0]
    nt = T // EC_TILE
    es = N_EXPERTS * EC_SLOTS
    lane = jnp.arange(es, dtype=I32)
    expand = (lane[None, :] // EC_SLOTS == jnp.arange(N_EXPERTS, dtype=I32)[:, None]).astype(BF16)
    slot = (lane % EC_SLOTS).astype(F32).reshape(1, es)
    tok = lambda c: pl.BlockSpec((EC_TILE, c), lambda i: (i, 0))
    full = lambda r, c: pl.BlockSpec((r, c), lambda i: (0, 0))
    return pl.pallas_call(
        _combine_kernel,
        grid=(nt,),
        in_specs=[pl.BlockSpec((N_EXPERTS, None, EC_SLOTS, D_MODEL), lambda i: (0, i, 0, 0)),
                  tok(N_EXPERTS), tok(N_EXPERTS), full(N_EXPERTS, es), full(1, es),
                  tok(D_MODEL), tok(D_MODEL), full(1, D_MODEL)],
        out_specs=tok(D_MODEL),
        out_shape=jax.ShapeDtypeStruct((T, D_MODEL), F32),
        compiler_params=_cparams("parallel"),
        name="ec_combine",
    )(ye, rank_t, gate_t, expand, slot, x2, y_over, gn.reshape(1, D_MODEL))


def _expert_mixture(h, gate, x2, wg, wu, wd, gn):
    T = h.shape[0]
    nt = T // EC_TILE
    rank, cnt = _tile_ranks(gate)
    rank_t, gate_t = rank.T, gate.T
    xe = _dispatch(h, rank, gate)
    ye = _expert_rows(xe.reshape(N_EXPERTS, nt * EC_SLOTS, D_MODEL), wg, wu, wd)
    per = min(EC_OVER_TILE, T) // EC_TILE
    flags = (cnt.reshape(N_EXPERTS, nt // per, per).max(axis=-1) > EC_SLOTS).T.reshape(-1).astype(I32)
    y_over = _overflow_ffn(h, rank_t, gate_t, flags, wg, wu, wd)
    return _combine(ye.reshape(N_EXPERTS, nt, EC_SLOTS, D_MODEL), rank_t, gate_t, x2, y_over, gn)


def _run_group(x, mem, W):
    B, L, _ = x.shape
    T = B * L
    xt = x.reshape(T, D_MODEL)
    proj = _inproj(xt, W['norm_mix'], W['w_in'], L)
    a = _diff_attention(proj, W['lam'], W['subln'], B, L)
    zc, x0, zb = _hyena_prep(proj, W['hy_conv_w'], W['hy_conv_b'], W['filt_bias'], B, L)
    tables = _dft_tables(L)
    kf = _filter_spectrum(L, W, tables)
    yc = _long_conv(zc, kf, tables, nb=1 if L >= 4096 else 4)
    x1 = _merge(xt, a, yc, x0, zb, proj, W['w_br_attn'], W['w_br_hyena'], W['w_out'], B, L)
    kv = _mem_kv(mem.reshape(B * N_MEM, D_MODEL), W['norm_mem'], W['w_xkv'])
    x2, hf, aff = _cross_attn(x1, kv, W['norm_x'], W['w_xq'], W['w_xo'], W['norm_ffn'],
                              W['w_router_t'], B, L)
    gate = _select(aff, EC_CAPACITY_FACTOR * T // N_EXPERTS)
    y = _expert_mixture(hf, gate, x2, W['w_exp_gate'], W['w_exp_up'], W['w_exp_down'], W['norm_final'])
    return y.reshape(B, L, D_MODEL)


def kernel(x_prompt, x_sample, mem_prompt, mem_sample, norm_mix, w_in, lambda_q1, lambda_k1, lambda_q2, lambda_k2, subln, w_br_attn, hy_conv_w, hy_conv_b, filt_w1, filt_b1, filt_w2, filt_b2, filt_w3, filt_freq, filt_bias, w_br_hyena, w_out, norm_x, norm_mem, w_xq, w_xkv, w_xo, norm_ffn, w_router, w_exp_gate, w_exp_up, w_exp_down, norm_final):
    l = 0
    lam = (jnp.exp(jnp.sum(lambda_q1[l] * lambda_k1[l])) - jnp.exp(jnp.sum(lambda_q2[l] * lambda_k2[l]))
           + LAM_INIT).astype(F32)
    W = dict(
        norm_mix=norm_mix[l], w_in=w_in[l].astype(BF16), lam=lam, subln=subln[l],
        w_br_attn=w_br_attn[l].astype(BF16), hy_conv_w=hy_conv_w[l], hy_conv_b=hy_conv_b[l],
        filt_w1=filt_w1[l], filt_b1=filt_b1[l], filt_w2=filt_w2[l], filt_b2=filt_b2[l],
        filt_w3=filt_w3[l], filt_freq=filt_freq[l], filt_bias=filt_bias[l],
        w_br_hyena=w_br_hyena[l].astype(BF16), w_out=w_out[l].astype(BF16),
        norm_x=norm_x[l], norm_mem=norm_mem[l], w_xq=w_xq[l].astype(BF16),
        w_xkv=w_xkv[l].astype(BF16), w_xo=w_xo[l].astype(BF16), norm_ffn=norm_ffn[l],
        w_router_t=w_router[l].T.astype(BF16), w_exp_gate=w_exp_gate[l].astype(BF16),
        w_exp_up=w_exp_up[l].astype(BF16), w_exp_down=w_exp_down[l].astype(BF16),
        norm_final=norm_final)
    return (_run_group(x_prompt, mem_prompt, W), _run_group(x_sample, mem_sample, W))
```

```python
import functools
import math

import jax
import jax.numpy as jnp
from jax import lax
from jax.experimental import pallas as pl
from jax.experimental.pallas import tpu as pltpu

F32 = jnp.float32
BF16 = jnp.bfloat16
I32 = jnp.int32

D_MODEL = 1024
N_ATT_HEADS = 4
ATT_HEAD_DIM = 64
ATT_V_DIM = 128
ROT_DIM = 16
ROPE_THETA = 500000.0
ATT_QK_W = 512
ATT_V_W = 512
HY_WIDTH = 512
FILTER_EMB = 33
FILTER_HIDDEN = 64
DECAY_TARGET = 1e-2
FAST_DECAY_PCT = 0.3
SLOW_DECAY_PCT = 1.5
IN_W = 5120
N_MEM = 256
X_HEADS = 4
X_HEAD_DIM = 256
N_EXPERTS = 16
EC_CAPACITY_FACTOR = 2
EXPERT_FF = 1024
EPS = 1e-6
LAM_INIT = 0.8 - 0.6 * math.exp(-0.3 * 0)

LANES = 128
ATT_ROW_GROUPS = 8
FFT_UNROLL = 8
VMEM_LIMIT_BYTES = 56 * 1024 * 1024


def _cparams(*sem):
    return pltpu.CompilerParams(dimension_semantics=sem, vmem_limit_bytes=VMEM_LIMIT_BYTES)


def _rms(x, g):
    return x * lax.rsqrt(jnp.mean(x * x, axis=-1, keepdims=True) + EPS) * g


def _inproj_kernel(x_ref, g_ref, w_ref, cos_ref, sa_ref, sb_ref, o_ref):
    u = _rms(x_ref[...], g_ref[...]).astype(BF16)
    n_col = IN_W // 512
    for j in range(n_col):
        acc = jnp.dot(u, w_ref[:, j * 512:(j + 1) * 512], preferred_element_type=F32)
        if j < 2:
            parts = []
            for h in range(4):
                t = acc[:, h * LANES:(h + 1) * LANES]
                r = (t * cos_ref[...] + pltpu.roll(t, 8, 1) * sa_ref[...]
                     + pltpu.roll(t, LANES - 8, 1) * sb_ref[...])
                parts.append(r)
            acc = jnp.concatenate(parts, axis=1)
        if j == 0:
            acc = acc * (ATT_HEAD_DIM ** -0.5 * math.log2(math.e))
        o_ref[:, j * 512:(j + 1) * 512] = acc.astype(BF16)


def _rope_tables(L):
    inv = ROPE_THETA ** (-jnp.arange(0, ROT_DIM, 2, dtype=F32) / ROT_DIM)
    pos = jnp.arange(L, dtype=F32)
    ang = pos[:, None] * inv[None, :]
    cos, sin = jnp.cos(ang), jnp.sin(ang)
    one = jnp.ones((L, 48), F32)
    zero8 = jnp.zeros((L, 8), F32)
    zero48 = jnp.zeros((L, 48), F32)
    c64 = jnp.concatenate([cos, cos, one], axis=1)
    sa64 = jnp.concatenate([zero8, sin, zero48], axis=1)
    sb64 = jnp.concatenate([-sin, zero8, zero48], axis=1)
    tile2 = lambda a: jnp.concatenate([a, a], axis=1)
    return tile2(c64), tile2(sa64), tile2(sb64)


def _inproj(xt, g, w_in_bf, L, tm=512):
    T = xt.shape[0]
    cos, sa, sb = _rope_tables(L)
    nl = L // tm
    tab = pl.BlockSpec((tm, LANES), lambda i: (i % nl, 0))
    return pl.pallas_call(
        _inproj_kernel,
        grid=(T // tm,),
        in_specs=[pl.BlockSpec((tm, D_MODEL), lambda i: (i, 0)),
                  pl.BlockSpec((1, D_MODEL), lambda i: (0, 0)),
                  pl.BlockSpec((D_MODEL, IN_W), lambda i: (0, 0)),
                  tab, tab, tab],
        out_specs=pl.BlockSpec((tm, IN_W), lambda i: (i, 0)),
        out_shape=jax.ShapeDtypeStruct((T, IN_W), BF16),
        compiler_params=_cparams("parallel"),
        name="inproj",
    )(xt, g.reshape(1, D_MODEL), w_in_bf, cos, sa, sb)


def _attn_kernel(lam_ref, q_ref, k_ref, v_ref, sg_ref, o_ref, vx_ref, *, seq, kc):
    tq = q_ref.shape[0]

    @pl.when(pl.program_id(2) == 0)
    def _():
        lane = lax.broadcasted_iota(I32, (seq, LANES), 1)
        vx_ref[:, :LANES] = v_ref[...]
        vx_ref[:, LANES:] = jnp.where(lane == 0, 1.0, 0.0).astype(BF16)

    q = q_ref[...]
    lane = lax.broadcasted_iota(I32, q.shape, 1)
    zero = jnp.zeros_like(q)
    qq = jnp.concatenate([jnp.where(lane < ATT_HEAD_DIM, q, zero),
                          jnp.where(lane >= ATT_HEAD_DIM, q, zero)], axis=0)

    rows = 2 * tq // ATT_ROW_GROUPS
    groups = [qq[g * rows:(g + 1) * rows] for g in range(ATT_ROW_GROUPS)]

    def scores(qg, c):
        start = pl.multiple_of(c * kc, kc)
        s = lax.dot_general(qg, k_ref[pl.ds(start, kc), :], (((1,), (1,)), ((), ())),
                            preferred_element_type=F32)
        return s, vx_ref[pl.ds(start, kc), :]

    if seq == kc:
        accs = []
        for qg in groups:
            s, vx = scores(qg, 0)
            p = jnp.exp2(s - jnp.max(s, axis=1, keepdims=True))
            accs.append(jnp.dot(p.astype(BF16), vx, preferred_element_type=F32))
    else:
        def body(c, carry):
            out = []
            for qg, (m, acc) in zip(groups, carry):
                s, vx = scores(qg, c)
                m_new = jnp.maximum(m, jnp.max(s, axis=1, keepdims=True))
                p = jnp.exp2(s - m_new)
                pv = jnp.dot(p.astype(BF16), vx, preferred_element_type=F32)
                out.append((m_new, jnp.exp2(m - m_new) * acc + pv))
            return tuple(out)

        init = tuple((jnp.full((rows, 1), -jnp.inf, F32), jnp.zeros((rows, 2 * LANES), F32))
                     for _ in groups)
        accs = [a for _, a in lax.fori_loop(0, seq // kc, body, init)]
    acc = jnp.concatenate(accs, axis=0)
    o = acc[:, :ATT_V_DIM] / acc[:, ATT_V_DIM:ATT_V_DIM + 1]
    a = o[:tq] - lam_ref[0] * o[tq:]
    a = _rms(a, sg_ref[...]) * (1.0 - LAM_INIT)
    o_ref[...] = a.astype(BF16)


def _diff_attention(proj, lam, subln, B, L):
    T = B * L
    tq = min(1024, L)
    nq = L // tq
    kc = min(2048, L)
    return pl.pallas_call(
        functools.partial(_attn_kernel, seq=L, kc=kc),
        grid=(B, N_ATT_HEADS, nq),
        in_specs=[pl.BlockSpec(memory_space=pltpu.SMEM),
                  pl.BlockSpec((tq, LANES), lambda b, h, i: (b * nq + i, h)),
                  pl.BlockSpec((L, LANES), lambda b, h, i: (b, 4 + h)),
                  pl.BlockSpec((L, LANES), lambda b, h, i: (b, 8 + h)),
                  pl.BlockSpec((1, ATT_V_DIM), lambda b, h, i: (0, 0))],
        out_specs=pl.BlockSpec((tq, ATT_V_DIM), lambda b, h, i: (b * nq + i, h)),
        out_shape=jax.ShapeDtypeStruct((T, ATT_V_W), BF16),
        scratch_shapes=[pltpu.VMEM((L, 2 * LANES), BF16)],
        compiler_params=_cparams("parallel", "parallel", "arbitrary"),
        name="diff_attn",
    )(lam.reshape(1), proj, proj, proj, subln.reshape(1, ATT_V_DIM))


def _hyprep_kernel(hy_ref, prev_ref, next_ref, w_ref, b_ref, fb_ref, zc_ref, x0_ref, zb_ref):
    i = pl.program_id(1)
    n = pl.num_programs(1)
    hy = hy_ref[...].astype(F32)
    tl = hy.shape[0]
    prev_row = jnp.where(i == 0, 0.0, prev_ref[15:16, :].astype(F32))
    next_row = jnp.where(i == n - 1, 0.0, next_ref[0:1, :].astype(F32))
    row = lax.broadcasted_iota(I32, hy.shape, 0)
    up = jnp.where(row == 0, prev_row, pltpu.roll(hy, 1, 0))
    dn = jnp.where(row == tl - 1, next_row, pltpu.roll(hy, tl - 1, 0))
    conv = up * w_ref[0:1, :] + hy * w_ref[1:2, :] + dn * w_ref[2:3, :] + b_ref[...]
    x0 = conv[:, :HY_WIDTH]
    x1 = conv[:, HY_WIDTH:2 * HY_WIDTH]
    hv = conv[:, 2 * HY_WIDTH:]
    z = hv * x1
    for c in range(HY_WIDTH // LANES):
        zc_ref[0, c] = z[:, c * LANES:(c + 1) * LANES].astype(zc_ref.dtype)
    x0_ref[...] = x0.astype(x0_ref.dtype)
    zb_ref[...] = (z * fb_ref[...] * x0).astype(zb_ref.dtype)


def _hyena_prep(proj, conv_w, conv_b, filt_bias, B, L, tl=512):
    T = B * L
    nl = L // tl
    hw = 3 * HY_WIDTH
    rb = tl // 16
    nrb = T // 16
    return pl.pallas_call(
        _hyprep_kernel,
        grid=(B, nl),
        in_specs=[pl.BlockSpec((tl, hw), lambda b, i: (b * nl + i, 1)),
                  pl.BlockSpec((16, hw), lambda b, i: (jnp.maximum((b * nl + i) * rb - 1, 0), 1)),
                  pl.BlockSpec((16, hw), lambda b, i: (jnp.minimum((b * nl + i + 1) * rb, nrb - 1), 1)),
                  pl.BlockSpec((3, hw), lambda b, i: (0, 0)),
                  pl.BlockSpec((1, hw), lambda b, i: (0, 0)),
                  pl.BlockSpec((1, HY_WIDTH), lambda b, i: (0, 0))],
        out_specs=[pl.BlockSpec((1, HY_WIDTH // LANES, tl, LANES), lambda b, i: (b, 0, i, 0)),
                   pl.BlockSpec((tl, HY_WIDTH), lambda b, i: (b * nl + i, 0)),
                   pl.BlockSpec((tl, HY_WIDTH), lambda b, i: (b * nl + i, 0))],
        out_shape=[jax.ShapeDtypeStruct((B, HY_WIDTH // LANES, L, LANES), BF16),
                   jax.ShapeDtypeStruct((T, HY_WIDTH), BF16),
                   jax.ShapeDtypeStruct((T, HY_WIDTH), BF16)],
        compiler_params=_cparams("parallel", "parallel"),
        name="hyena_prep",
    )(proj, proj, proj, conv_w, conv_b.reshape(1, hw), filt_bias.reshape(1, HY_WIDTH))


def _fft_dims(L):
    n = 2 * L
    n1 = 1 << ((n.bit_length() - 1 + 1) // 2)
    n2 = n // n1
    A = n1 // 2
    K1 = n1 // 2 + 1
    K1p = -(-K1 // 8) * 8
    KP = -(-2 * K1p // LANES) * LANES
    assert n1 * n2 == n and A * n2 == L and n2 % 8 == 0 and A % 16 == 0
    return n, n1, n2, A, K1, K1p, KP


def _dft_tables(L):
    n, n1, n2, A, K1, K1p, KP = _fft_dims(L)
    b = jnp.arange(n2, dtype=I32)
    a = jnp.arange(A, dtype=I32)
    k1 = jnp.arange(K1p, dtype=I32)
    t = a[None, :] * n2 + b[:, None]
    m = (k1[None, :, None] * t[:, None, :]) % n
    th = m.astype(F32) * (2.0 * math.pi / n)
    valid = (k1 < K1).astype(F32)[None, :, None]
    c, s = jnp.cos(th) * valid, jnp.sin(th) * valid
    g_fwd = jnp.concatenate([c, -s], axis=1).astype(BF16)
    w = jnp.where((k1 == 0) | (k1 == n1 // 2), 1.0, 2.0)[None, :, None] / n
    ci = jnp.transpose(c * w, (0, 2, 1))
    si = jnp.transpose(-s * w, (0, 2, 1))
    pad = jnp.zeros((n2, A, KP - 2 * K1p), F32)
    g_inv = jnp.concatenate([ci, si, pad], axis=2).astype(BF16)
    k2 = jnp.arange(n2, dtype=I32)
    ph = ((k2[:, None] * b[None, :]) % n2).astype(F32) * (2.0 * math.pi / n2)
    c2, s2 = jnp.cos(ph), jnp.sin(ph)
    f2 = jnp.concatenate([jnp.concatenate([c2, s2], 1), jnp.concatenate([-s2, c2], 1)], 0).astype(BF16)
    f2i = jnp.concatenate([jnp.concatenate([c2, -s2], 1), jnp.concatenate([s2, c2], 1)], 0).astype(BF16)
    return g_fwd, g_inv, f2, f2i


def _fft_rows_fwd(x_ref, g_ref, yf_ref, *, nb, n2, K1p):
    def body(b, _):
        off = pl.multiple_of(b * LANES, LANES)
        xb = jnp.concatenate([x_ref[i, :, pl.ds(off, LANES)] for i in range(nb)], axis=1)
        y = jnp.dot(g_ref[b], xb, preferred_element_type=F32)
        for i in range(nb):
            for r in range(2 * K1p // 8):
                row = pl.multiple_of((r * n2 + b) * 8, 8)
                yf_ref[i, pl.ds(row, 8), :] = y[r * 8:(r + 1) * 8, i * LANES:(i + 1) * LANES]
        return 0
    lax.fori_loop(0, n2, body, 0, unroll=FFT_UNROLL)


def _fft_cols_fwd(yf_ref, f2_ref, k1, *, nb, n2, K1p):
    kt, ks = k1 // 8, k1 % 8
    parts = []
    for i in range(nb):
        re = yf_ref[i, pl.ds(kt * (n2 * 8) + ks, n2, stride=8), :]
        im = yf_ref[i, pl.ds((K1p // 8 + kt) * (n2 * 8) + ks, n2, stride=8), :]
        parts.append(jnp.concatenate([re, im], axis=0))
    yk = jnp.concatenate(parts, axis=1).astype(BF16)
    return jnp.dot(f2_ref[...], yk, preferred_element_type=F32)


def _kspec_kernel(fb_ref, l1_ref, g_ref, f2_ref, kf_ref, yf_ref, *, n2, K1p):
    _fft_rows_fwd(fb_ref, g_ref, yf_ref, nb=2, n2=n2, K1p=K1p)
    inv = 1.0 / l1_ref[...]

    def body(k1, _):
        z = _fft_cols_fwd(yf_ref, f2_ref, k1, nb=2, n2=n2, K1p=K1p)
        zf, zb = z[:, :LANES], z[:, LANES:]
        kr = (zf[:n2] + zb[:n2]) * inv
        ki = (zf[n2:] - zb[n2:]) * inv
        kf_ref[k1] = jnp.concatenate([kr, ki], axis=0).astype(kf_ref.dtype)
        return 0
    lax.fori_loop(0, K1p, body, 0, unroll=FFT_UNROLL)


def _lconv_kernel(x_ref, g_ref, f2_ref, f2i_ref, kf_ref, gi_ref, o_ref, yf_ref, wf_ref, *,
                  nb, n2, K1p, KP):
    _fft_rows_fwd(x_ref, g_ref, yf_ref, nb=nb, n2=n2, K1p=K1p)

    def kbody(k1, _):
        z = _fft_cols_fwd(yf_ref, f2_ref, k1, nb=nb, n2=n2, K1p=K1p)
        kf = kf_ref[k1].astype(F32)
        kr = jnp.concatenate([kf[:n2]] * nb, axis=1)
        ki = jnp.concatenate([kf[n2:]] * nb, axis=1)
        zr, zi = z[:n2], z[n2:]
        p = jnp.concatenate([zr * kr - zi * ki, zr * ki + zi * kr], axis=0).astype(BF16)
        v = jnp.dot(f2i_ref[...], p, preferred_element_type=F32)
        for i in range(nb):
            for r in range(2 * n2 // 8):
                row = pl.multiple_of((r * K1p + k1) * 8, 8)
                wf_ref[i, pl.ds(row, 8), :] = v[r * 8:(r + 1) * 8, i * LANES:(i + 1) * LANES]
        return 0
    lax.fori_loop(0, K1p, kbody, 0, unroll=FFT_UNROLL)

    def bbody(b, _):
        bt, bs = b // 8, b % 8
        parts = []
        for i in range(nb):
            re = wf_ref[i, pl.ds(bt * (K1p * 8) + bs, K1p, stride=8), :]
            im = wf_ref[i, pl.ds((n2 // 8 + bt) * (K1p * 8) + bs, K1p, stride=8), :]
            parts.append(jnp.concatenate([re, im, jnp.zeros((KP - 2 * K1p, LANES), F32)], axis=0))
        vb = jnp.concatenate(parts, axis=1).astype(BF16)
        ob = jnp.dot(gi_ref[b], vb, preferred_element_type=F32)
        off = pl.multiple_of(b * LANES, LANES)
        for i in range(nb):
            o_ref[i, :, pl.ds(off, LANES)] = ob[:, i * LANES:(i + 1) * LANES].astype(o_ref.dtype)
        return 0
    lax.fori_loop(0, n2, bbody, 0, unroll=FFT_UNROLL)


def _filter_kernel(w1_ref, b1_ref, w2_ref, b2_ref, w3_ref, fr_ref, fb_ref, l1_ref, *, seq):
    i = pl.program_id(0)
    tl = fb_ref.shape[2]
    bands = (FILTER_EMB - 1) // 2
    pos = (i * tl + lax.broadcasted_iota(I32, (tl, 1), 0)).astype(F32)
    t = pos * (1.0 / (seq - 1))
    wpos = pos * (2.0 * math.pi / seq)
    lane = lax.broadcasted_iota(I32, (1, LANES), 1)
    band = jnp.where(lane <= bands, lane - 1, lane - 1 - bands).astype(F32)
    fvec = 1e-4 + band * ((bands - 1 - 1e-4) / (bands - 1))
    arg = wpos * fvec
    z = jnp.where(lane == 0, t, jnp.where(lane <= bands, jnp.cos(arg),
                                          jnp.where(lane <= 2 * bands, -jnp.sin(arg), 0.0)))
    fr = fr_ref[...]
    h = jnp.sin(fr * (jnp.dot(z, w1_ref[...], preferred_element_type=F32) + b1_ref[...]))
    h = jnp.sin(fr * (jnp.dot(h, w2_ref[...], preferred_element_type=F32) + b2_ref[...]))
    h = jnp.dot(h, w3_ref[...], preferred_element_type=F32)
    max_decay = math.log(DECAY_TARGET) / FAST_DECAY_PCT
    min_decay = math.log(DECAY_TARGET) / SLOW_DECAY_PCT
    ch = lax.broadcasted_iota(I32, (1, HY_WIDTH), 1).astype(F32)
    deltas = jnp.abs(min_decay + ch * ((max_decay - min_decay) / (HY_WIDTH - 1)))
    decay = jnp.exp(-t * deltas)
    fwd = h[:, :HY_WIDTH] * decay
    bwd = jnp.where(pos == 0.0, 0.0, h[:, HY_WIDTH:] * decay)
    for c in range(HY_WIDTH // LANES):
        fb_ref[0, c] = fwd[:, c * LANES:(c + 1) * LANES].astype(fb_ref.dtype)
        fb_ref[1, c] = bwd[:, c * LANES:(c + 1) * LANES].astype(fb_ref.dtype)
    part = jnp.sum(jnp.abs(fwd) + jnp.abs(bwd), axis=0, keepdims=True)

    @pl.when(i == 0)
    def _():
        l1_ref[...] = jnp.zeros_like(l1_ref)
    l1_ref[...] += part


def _filter_spectrum(L, W, tables, tl=512):
    n, n1, n2, A, K1, K1p, KP = _fft_dims(L)
    g_fwd, _, f2, _ = tables
    nc = HY_WIDTH // LANES
    w1 = jnp.pad(W['filt_w1'], ((0, LANES - FILTER_EMB), (0, 0)))
    full = lambda r, c: pl.BlockSpec((r, c), lambda i: (0, 0))
    fb, l1 = pl.pallas_call(
        functools.partial(_filter_kernel, seq=L),
        grid=(L // tl,),
        in_specs=[full(LANES, FILTER_HIDDEN), full(1, FILTER_HIDDEN),
                  full(FILTER_HIDDEN, FILTER_HIDDEN), full(1, FILTER_HIDDEN),
                  full(FILTER_HIDDEN, 2 * HY_WIDTH), full(1, FILTER_HIDDEN)],
        out_specs=[pl.BlockSpec((2, nc, tl, LANES), lambda i: (0, 0, i, 0)),
                   pl.BlockSpec((1, HY_WIDTH), lambda i: (0, 0))],
        out_shape=[jax.ShapeDtypeStruct((2, nc, L, LANES), BF16),
                   jax.ShapeDtypeStruct((1, HY_WIDTH), F32)],
        compiler_params=_cparams("arbitrary"),
        name="hyena_filter",
    )(w1, W['filt_b1'].reshape(1, -1), W['filt_w2'], W['filt_b2'].reshape(1, -1), W['filt_w3'],
      W['filt_freq'].reshape(1, -1))
    fbr = fb.reshape(2, nc, A, n2 * LANES)
    rows = (2 * K1p // 8) * n2 * 8
    return pl.pallas_call(
        functools.partial(_kspec_kernel, n2=n2, K1p=K1p),
        grid=(nc,),
        in_specs=[pl.BlockSpec((2, None, A, n2 * LANES), lambda c: (0, c, 0, 0)),
                  pl.BlockSpec((1, LANES), lambda c: (0, c)),
                  pl.BlockSpec((n2, 2 * K1p, A), lambda c: (0, 0, 0)),
                  pl.BlockSpec((2 * n2, 2 * n2), lambda c: (0, 0))],
        out_specs=pl.BlockSpec((None, K1p, 2 * n2, LANES), lambda c: (c, 0, 0, 0)),
        out_shape=jax.ShapeDtypeStruct((nc, K1p, 2 * n2, LANES), BF16),
        scratch_shapes=[pltpu.VMEM((2, rows, LANES), F32)],
        compiler_params=_cparams("arbitrary"),
        name="filter_spectrum",
    )(fbr, l1, g_fwd, f2)


def _long_conv(zc, kf, tables, nb):
    B, nc, L, _ = zc.shape
    n, n1, n2, A, K1, K1p, KP = _fft_dims(L)
    g_fwd, g_inv, f2, f2i = tables
    x = zc.reshape(B, nc, A, n2 * LANES)
    rows_y = (2 * K1p // 8) * n2 * 8
    rows_w = (2 * n2 // 8) * K1p * 8
    const = lambda shape: pl.BlockSpec(shape, lambda c, j: (0,) * len(shape),
                                       pipeline_mode=pl.Buffered(1))
    y = pl.pallas_call(
        functools.partial(_lconv_kernel, nb=nb, n2=n2, K1p=K1p, KP=KP),
        grid=(nc, B // nb),
        in_specs=[pl.BlockSpec((nb, None, A, n2 * LANES), lambda c, j: (j, c, 0, 0)),
                  const((n2, 2 * K1p, A)), const((2 * n2, 2 * n2)), const((2 * n2, 2 * n2)),
                  pl.BlockSpec((None, K1p, 2 * n2, LANES), lambda c, j: (c, 0, 0, 0)),
                  const((n2, A, KP))],
        out_specs=pl.BlockSpec((nb, None, A, n2 * LANES), lambda c, j: (j, c, 0, 0)),
        out_shape=jax.ShapeDtypeStruct((B, nc, A, n2 * LANES), BF16),
        scratch_shapes=[pltpu.VMEM((nb, rows_y, LANES), F32), pltpu.VMEM((nb, rows_w, LANES), F32)],
        compiler_params=_cparams("arbitrary", "arbitrary"),
        name="long_conv",
    )(x, g_fwd, f2, f2i, kf, g_inv)
    return y.reshape(B, nc, L, LANES)


def _merge_kernel(x_ref, a_ref, y_ref, x0_ref, zb_ref, g0_ref, g1_ref, wa_ref, wh_ref, wo_ref, o_ref):
    a = jnp.dot(a_ref[...], wa_ref[...], preferred_element_type=F32)
    x0 = x0_ref[...].astype(F32)
    y = jnp.concatenate([y_ref[0, c].astype(F32) for c in range(HY_WIDTH // LANES)], axis=1)
    hz = (y * x0 + zb_ref[...].astype(F32)).astype(BF16)
    hzp = jnp.dot(hz, wh_ref[...], preferred_element_type=F32)
    g0 = jax.nn.sigmoid(g0_ref[...].astype(F32))
    g1 = jax.nn.sigmoid(g1_ref[...].astype(F32))
    m = (g0 * a + g1 * hzp).astype(BF16)
    o_ref[...] = x_ref[...] + jnp.dot(m, wo_ref[...], preferred_element_type=F32)


def _merge(xt, a, yc, x0, zb, proj, wa, wh, wo, B, L, tm=512):
    T = B * L
    nl = L // tm
    full = lambda r, c: pl.BlockSpec((r, c), lambda b, i: (0, 0))
    tok = lambda c, cb=0: pl.BlockSpec((tm, c), lambda b, i, cb=cb: (b * nl + i, cb))
    return pl.pallas_call(
        _merge_kernel,
        grid=(B, nl),
        in_specs=[tok(D_MODEL), tok(ATT_V_W),
                  pl.BlockSpec((1, HY_WIDTH // LANES, tm, LANES), lambda b, i: (b, 0, i, 0)),
                  tok(HY_WIDTH), tok(HY_WIDTH), tok(D_MODEL, 3), tok(D_MODEL, 4),
                  full(ATT_V_W, D_MODEL), full(HY_WIDTH, D_MODEL), full(D_MODEL, D_MODEL)],
        out_specs=tok(D_MODEL),
        out_shape=jax.ShapeDtypeStruct((T, D_MODEL), F32),
        compiler_params=_cparams("parallel", "parallel"),
        name="merge",
    )(xt, a, yc, x0, zb, proj, proj, wa, wh, wo)


def _memkv_kernel(m_ref, g_ref, w_ref, o_ref):
    u = _rms(m_ref[...], g_ref[...]).astype(BF16)
    o_ref[...] = jnp.dot(u, w_ref[...], preferred_element_type=F32).astype(BF16)


def _mem_kv(mem_t, g, w_xkv_bf):
    R = mem_t.shape[0]
    return pl.pallas_call(
        _memkv_kernel,
        grid=(R // N_MEM,),
        in_specs=[pl.BlockSpec((N_MEM, D_MODEL), lambda i: (i, 0)),
                  pl.BlockSpec((1, D_MODEL), lambda i: (0, 0)),
                  pl.BlockSpec((D_MODEL, 2 * D_MODEL), lambda i: (0, 0))],
        out_specs=pl.BlockSpec((N_MEM, 2 * D_MODEL), lambda i: (i, 0)),
        out_shape=jax.ShapeDtypeStruct((R, 2 * D_MODEL), BF16),
        compiler_params=_cparams("parallel"),
        name="mem_kv",
    )(mem_t, g.reshape(1, D_MODEL), w_xkv_bf)


def _xattn_kernel(x_ref, kv_ref, gx_ref, wq_ref, wo_ref, gf_ref, wr_ref, x2_ref, h_ref, aff_ref):
    x = x_ref[...]
    u = _rms(x, gx_ref[...]).astype(BF16)
    q = jnp.dot(u, wq_ref[...], preferred_element_type=F32)
    q = (q * (X_HEAD_DIM ** -0.5)).astype(BF16)
    outs = []
    for h in range(X_HEADS):
        qh = q[:, h * X_HEAD_DIM:(h + 1) * X_HEAD_DIM]
        kh = kv_ref[:, h * X_HEAD_DIM:(h + 1) * X_HEAD_DIM]
        vh = kv_ref[:, D_MODEL + h * X_HEAD_DIM:D_MODEL + (h + 1) * X_HEAD_DIM]
        s = lax.dot_general(qh, kh, (((1,), (1,)), ((), ())), preferred_element_type=F32)
        s = s - jnp.max(s, axis=1, keepdims=True)
        p = jnp.exp(s)
        p = p / jnp.sum(p, axis=1, keepdims=True)
        outs.append(jnp.dot(p.astype(BF16), vh, preferred_element_type=F32))
    o = jnp.concatenate(outs, axis=1).astype(BF16)
    x2 = x + jnp.dot(o, wo_ref[...], preferred_element_type=F32)
    x2_ref[...] = x2
    hf = _rms(x2, gf_ref[...]).astype(BF16)
    h_ref[...] = hf
    lg = lax.dot_general(wr_ref[...], hf, (((1,), (1,)), ((), ())), preferred_element_type=F32)
    lg = lg - jnp.max(lg, axis=0, keepdims=True)
    e = jnp.exp(lg)
    aff_ref[...] = e / jnp.sum(e, axis=0, keepdims=True)


def _cross_attn(x1, kv, gx, wq, wo, gf, wr_t, B, L, tm=512):
    T = B * L
    nl = L // tm
    full = lambda r, c: pl.BlockSpec((r, c), lambda b, i: (0, 0))
    tok = lambda c: pl.BlockSpec((tm, c), lambda b, i: (b * nl + i, 0))
    return pl.pallas_call(
        _xattn_kernel,
        grid=(B, nl),
        in_specs=[tok(D_MODEL),
                  pl.BlockSpec((N_MEM, 2 * D_MODEL), lambda b, i: (b, 0)),
                  full(1, D_MODEL), full(D_MODEL, D_MODEL), full(D_MODEL, D_MODEL),
                  full(1, D_MODEL), full(N_EXPERTS, D_MODEL)],
        out_specs=[tok(D_MODEL), tok(D_MODEL),
                   pl.BlockSpec((N_EXPERTS, tm), lambda b, i: (0, b * nl + i))],
        out_shape=[jax.ShapeDtypeStruct((T, D_MODEL), F32),
                   jax.ShapeDtypeStruct((T, D_MODEL), BF16),
                   jax.ShapeDtypeStruct((N_EXPERTS, T), F32)],
        compiler_params=_cparams("parallel", "parallel"),
        name="cross_attn",
    )(x1, kv, gx.reshape(1, D_MODEL), wq, wo, gf.reshape(1, D_MODEL), wr_t)


def _select_kernel(aff_ref, gate_ref, *, cap):
    aff = aff_ref[...]
    bits = pltpu.bitcast(aff, I32)
    T = aff.shape[1]

    def count(mask):
        return jnp.sum(mask.astype(I32), axis=1, keepdims=True)

    def vbody(k, thr):
        cand = thr | (jnp.int32(1) << (30 - k))
        return jnp.where(count(bits >= cand) >= cap, cand, thr)

    thr = lax.fori_loop(0, 31, vbody, jnp.zeros((N_EXPERTS, 1), I32))
    gt = bits > thr
    eq = bits == thr
    need = cap - count(gt)
    tok = lax.broadcasted_iota(I32, aff.shape, 1)
    nbit = int(T).bit_length()

    def jbody(k, j):
        cand = j + (jnp.int32(1) << (nbit - 1 - k))
        return jnp.where(count(eq & (tok < cand)) <= need, cand, j)

    j = lax.fori_loop(0, nbit, jbody, jnp.zeros((N_EXPERTS, 1), I32))
    sel = gt | (eq & (tok < j))
    gate_ref[...] = jnp.where(sel, aff, 0.0)


def _select(aff_t, cap):
    E, T = aff_t.shape
    return pl.pallas_call(
        functools.partial(_select_kernel, cap=cap),
        grid=(1,),
        in_specs=[pl.BlockSpec((E, T), lambda i: (0, 0))],
        out_specs=pl.BlockSpec((E, T), lambda i: (0, 0)),
        out_shape=jax.ShapeDtypeStruct((E, T), F32),
        compiler_params=_cparams("arbitrary"),
        name="ec_select",
    )(aff_t)


EC_TILE = 256
EC_SLOTS = 64


def _rank_kernel(g_ref, rank_ref, cnt_ref):
    n = g_ref.shape[1]
    sel = jnp.where(g_ref[...] > 0.0, 1.0, 0.0).astype(BF16)
    r = lax.broadcasted_iota(I32, (n, n), 0)
    c = lax.broadcasted_iota(I32, (n, n), 1)
    before = jnp.where(r < c, 1.0, 0.0).astype(BF16)
    rank_ref[...] = jnp.dot(sel, before, preferred_element_type=F32)
    cnt_ref[...] = jnp.dot(sel, jnp.ones((n, LANES), BF16), preferred_element_type=F32)


def _tile_ranks(gate):
    E, T = gate.shape
    rows = E * T // EC_TILE
    rb = min(512, rows)
    rank, cnt = pl.pallas_call(
        _rank_kernel,
        grid=(rows // rb,),
        in_specs=[pl.BlockSpec((rb, EC_TILE), lambda i: (i, 0))],
        out_specs=[pl.BlockSpec((rb, EC_TILE), lambda i: (i, 0)),
                   pl.BlockSpec((rb, LANES), lambda i: (i, 0))],
        out_shape=[jax.ShapeDtypeStruct((rows, EC_TILE), F32),
                   jax.ShapeDtypeStruct((rows, LANES), F32)],
        compiler_params=_cparams("parallel"),
        name="ec_rank",
    )(gate.reshape(rows, EC_TILE))
    return rank.reshape(E, T), cnt[:, 0].reshape(E, T // EC_TILE)


def _dispatch_kernel(h_ref, rank_ref, gate_ref, xe_ref):
    slot = lax.broadcasted_iota(I32, (EC_SLOTS, EC_TILE), 0).astype(F32)
    parts = []
    for e in range(N_EXPERTS):
        hit = (rank_ref[e:e + 1, :] == slot) & (gate_ref[e:e + 1, :] > 0.0)
        parts.append(jnp.where(hit, 1.0, 0.0).astype(BF16))
    onehot = jnp.concatenate(parts, axis=0)
    xe = jnp.dot(onehot, h_ref[...], preferred_element_type=F32).astype(BF16)
    xe_ref[...] = xe.reshape(N_EXPERTS, EC_SLOTS, D_MODEL)


def _dispatch(h, rank, gate):
    T = h.shape[0]
    nt = T // EC_TILE
    return pl.pallas_call(
        _dispatch_kernel,
        grid=(nt,),
        in_specs=[pl.BlockSpec((EC_TILE, D_MODEL), lambda i: (i, 0)),
                  pl.BlockSpec((N_EXPERTS, EC_TILE), lambda i: (0, i)),
                  pl.BlockSpec((N_EXPERTS, EC_TILE), lambda i: (0, i))],
        out_specs=pl.BlockSpec((N_EXPERTS, None, EC_SLOTS, D_MODEL), lambda i: (0, i, 0, 0)),
        out_shape=jax.ShapeDtypeStruct((N_EXPERTS, nt, EC_SLOTS, D_MODEL), BF16),
        compiler_params=_cparams("parallel"),
        name="ec_dispatch",
    )(h, rank, gate)


def _swiglu(h, wg_ref, wu_ref, wd_ref):
    half = EXPERT_FF // 2
    y = None
    for c in range(2):
        sl = slice(c * half, (c + 1) * half)
        a = jnp.dot(h, wg_ref[:, sl], preferred_element_type=F32)
        b = jnp.dot(h, wu_ref[:, sl], preferred_element_type=F32)
        he = (jax.nn.silu(a) * b).astype(BF16)
        d = jnp.dot(he, wd_ref[sl, :], preferred_element_type=F32)
        y = d if y is None else y + d
    return y


def _expert_rows_kernel(x_ref, wg_ref, wu_ref, wd_ref, o_ref):
    o_ref[0] = _swiglu(x_ref[0], wg_ref.at[0], wu_ref.at[0], wd_ref.at[0]).astype(o_ref.dtype)


def _expert_rows(xe, wg, wu, wd):
    E, R, _ = xe.shape
    rt = math.gcd(R, 1024)
    rows = pl.BlockSpec((1, rt, D_MODEL), lambda e, j: (e, j, 0))
    wspec = lambda r, c: pl.BlockSpec((1, r, c), lambda e, j: (e, 0, 0))
    return pl.pallas_call(
        _expert_rows_kernel,
        grid=(E, R // rt),
        in_specs=[rows, wspec(D_MODEL, EXPERT_FF), wspec(D_MODEL, EXPERT_FF), wspec(EXPERT_FF, D_MODEL)],
        out_specs=rows,
        out_shape=jax.ShapeDtypeStruct((E, R, D_MODEL), BF16),
        compiler_params=_cparams("parallel", "parallel"),
        name="expert_rows",
    )(xe, wg, wu, wd)


def _combine_kernel(flag_ref, tflag_ref, ye_ref, rank_ref, gate_ref, ex_ref, sl_ref, x_ref, h_ref,
                    gn_ref, wg_hbm, wu_hbm, wd_hbm, o_ref, wbuf, sem):
    i = pl.program_id(0)
    ye = ye_ref[...].reshape(N_EXPERTS * EC_SLOTS, D_MODEL)
    rank = rank_ref[...]
    gate = gate_ref[...]
    r = jnp.dot(rank.astype(BF16), ex_ref[...], preferred_element_type=F32)
    g = jnp.dot(gate.astype(BF16), ex_ref[...], preferred_element_type=F32)
    q = jnp.where(r == sl_ref[...], g, 0.0).astype(BF16)
    o_ref[...] = x_ref[...] + jnp.dot(q, ye, preferred_element_type=F32)

    @pl.when(tflag_ref[i] > 0)
    def _():
        def expert(e, carry):
            @pl.when(flag_ref[i * N_EXPERTS + e] > 0)
            def _():
                copies = [pltpu.make_async_copy(w.at[e], wbuf.at[k], sem.at[k])
                          for k, w in enumerate((wg_hbm, wu_hbm, wd_hbm))]
                for cp in copies:
                    cp.start()
                for cp in copies:
                    cp.wait()
                lane = lax.broadcasted_iota(I32, rank.shape, 1)
                keep = (lane == e) & (rank >= float(EC_SLOTS))
                ge = jnp.sum(jnp.where(keep, gate, 0.0), axis=1, keepdims=True)
                o_ref[...] += _swiglu(h_ref[...], wbuf.at[0], wbuf.at[1], wbuf.at[2]) * ge
            return carry
        lax.fori_loop(0, N_EXPERTS, expert, 0)

    o_ref[...] = _rms(o_ref[...], gn_ref[...])


def _combine(ye, rank_t, gate_t, x2, h, flags, tflags, wg, wu, wd, gn):
    T = x2.shape[0]
    nt = T // EC_TILE
    es = N_EXPERTS * EC_SLOTS
    lane = jnp.arange(es, dtype=I32)
    expand = (lane[None, :] // EC_SLOTS == jnp.arange(N_EXPERTS, dtype=I32)[:, None]).astype(BF16)
    slot = (lane % EC_SLOTS).astype(F32).reshape(1, es)
    tok = lambda c: pl.BlockSpec((EC_TILE, c), lambda i, f, t: (i, 0))
    full = lambda r, c: pl.BlockSpec((r, c), lambda i, f, t: (0, 0))
    hbm = pl.BlockSpec(memory_space=pl.ANY)
    return pl.pallas_call(
        _combine_kernel,
        grid_spec=pltpu.PrefetchScalarGridSpec(
            num_scalar_prefetch=2,
            grid=(nt,),
            in_specs=[pl.BlockSpec((N_EXPERTS, None, EC_SLOTS, D_MODEL), lambda i, f, t: (0, i, 0, 0)),
                      tok(N_EXPERTS), tok(N_EXPERTS), full(N_EXPERTS, es), full(1, es),
                      tok(D_MODEL), tok(D_MODEL), full(1, D_MODEL), hbm, hbm, hbm],
            out_specs=tok(D_MODEL),
            scratch_shapes=[pltpu.VMEM((3, D_MODEL, EXPERT_FF), BF16), pltpu.SemaphoreType.DMA((3,))]),
        out_shape=jax.ShapeDtypeStruct((T, D_MODEL), F32),
        compiler_params=_cparams("arbitrary"),
        name="ec_combine",
    )(flags, tflags, ye, rank_t, gate_t, expand, slot, x2, h, gn.reshape(1, D_MODEL), wg, wu, wd)


def _expert_mixture(h, gate, x2, wg, wu, wd, gn):
    T = h.shape[0]
    nt = T // EC_TILE
    rank, cnt = _tile_ranks(gate)
    rank_t, gate_t = rank.T, gate.T
    xe = _dispatch(h, rank, gate)
    ye = _expert_rows(xe.reshape(N_EXPERTS, nt * EC_SLOTS, D_MODEL), wg, wu, wd)
    over = cnt.T > EC_SLOTS
    flags = over.reshape(-1).astype(I32)
    tflags = over.any(axis=1).astype(I32)
    return _combine(ye.reshape(N_EXPERTS, nt, EC_SLOTS, D_MODEL), rank_t, gate_t, x2, h, flags, tflags,
                    wg, wu, wd, gn)


def _run_group(x, mem, W):
    B, L, _ = x.shape
    T = B * L
    xt = x.reshape(T, D_MODEL)
    proj = _inproj(xt, W['norm_mix'], W['w_in'], L)
    a = _diff_attention(proj, W['lam'], W['subln'], B, L)
    zc, x0, zb = _hyena_prep(proj, W['hy_conv_w'], W['hy_conv_b'], W['filt_bias'], B, L)
    tables = _dft_tables(L)
    kf = _filter_spectrum(L, W, tables)
    yc = _long_conv(zc, kf, tables, nb=1 if L >= 4096 else 4)
    x1 = _merge(xt, a, yc, x0, zb, proj, W['w_br_attn'], W['w_br_hyena'], W['w_out'], B, L)
    kv = _mem_kv(mem.reshape(B * N_MEM, D_MODEL), W['norm_mem'], W['w_xkv'])
    x2, hf, aff = _cross_attn(x1, kv, W['norm_x'], W['w_xq'], W['w_xo'], W['norm_ffn'],
                              W['w_router_t'], B, L)
    gate = _select(aff, EC_CAPACITY_FACTOR * T // N_EXPERTS)
    y = _expert_mixture(hf, gate, x2, W['w_exp_gate'], W['w_exp_up'], W['w_exp_down'], W['norm_final'])
    return y.reshape(B, L, D_MODEL)


def kernel(x_prompt, x_sample, mem_prompt, mem_sample, norm_mix, w_in, lambda_q1, lambda_k1, lambda_q2, lambda_k2, subln, w_br_attn, hy_conv_w, hy_conv_b, filt_w1, filt_b1, filt_w2, filt_b2, filt_w3, filt_freq, filt_bias, w_br_hyena, w_out, norm_x, norm_mem, w_xq, w_xkv, w_xo, norm_ffn, w_router, w_exp_gate, w_exp_up, w_exp_down, norm_final):
    l = 0
    lam = (jnp.exp(jnp.sum(lambda_q1[l] * lambda_k1[l])) - jnp.exp(jnp.sum(lambda_q2[l] * lambda_k2[l]))
           + LAM_INIT).astype(F32)
    W = dict(
        norm_mix=norm_mix[l], w_in=w_in[l].astype(BF16), lam=lam, subln=subln[l],
        w_br_attn=w_br_attn[l].astype(BF16), hy_conv_w=hy_conv_w[l], hy_conv_b=hy_conv_b[l],
        filt_w1=filt_w1[l], filt_b1=filt_b1[l], filt_w2=filt_w2[l], filt_b2=filt_b2[l],
        filt_w3=filt_w3[l], filt_freq=filt_freq[l], filt_bias=filt_bias[l],
        w_br_hyena=w_br_hyena[l].astype(BF16), w_out=w_out[l].astype(BF16),
        norm_x=norm_x[l], norm_mem=norm_mem[l], w_xq=w_xq[l].astype(BF16),
        w_xkv=w_xkv[l].astype(BF16), w_xo=w_xo[l].astype(BF16), norm_ffn=norm_ffn[l],
        w_router_t=w_router[l].T.astype(BF16), w_exp_gate=w_exp_gate[l].astype(BF16),
        w_exp_up=w_exp_up[l].astype(BF16), w_exp_down=w_exp_down[l].astype(BF16),
        norm_final=norm_final)
    return (_run_group(x_prompt, mem_prompt, W), _run_group(x_sample, mem_sample, W))
```

```python
import functools
import math

import jax
import jax.numpy as jnp
from jax import lax
from jax.experimental import pallas as pl
from jax.experimental.pallas import tpu as pltpu

F32 = jnp.float32
BF16 = jnp.bfloat16
I32 = jnp.int32

D_MODEL = 1024
N_ATT_HEADS = 4
ATT_HEAD_DIM = 64
ATT_V_DIM = 128
ROT_DIM = 16
ROPE_THETA = 500000.0
ATT_QK_W = 512
ATT_V_W = 512
HY_WIDTH = 512
FILTER_EMB = 33
FILTER_HIDDEN = 64
DECAY_TARGET = 1e-2
FAST_DECAY_PCT = 0.3
SLOW_DECAY_PCT = 1.5
IN_W = 5120
N_MEM = 256
X_HEADS = 4
X_HEAD_DIM = 256
N_EXPERTS = 16
EC_CAPACITY_FACTOR = 2
EXPERT_FF = 1024
EPS = 1e-6
LAM_INIT = 0.8 - 0.6 * math.exp(-0.3 * 0)

LANES = 128
ATT_ROW_GROUPS = 8
FFT_UNROLL = 8
VMEM_LIMIT_BYTES = 56 * 1024 * 1024


def _cparams(*sem):
    return pltpu.CompilerParams(dimension_semantics=sem, vmem_limit_bytes=VMEM_LIMIT_BYTES)


def _rms(x, g):
    return x * lax.rsqrt(jnp.mean(x * x, axis=-1, keepdims=True) + EPS) * g


def _inproj_kernel(x_ref, g_ref, w_ref, cos_ref, sa_ref, sb_ref, o_ref):
    u = _rms(x_ref[...], g_ref[...]).astype(BF16)
    n_col = IN_W // 512
    for j in range(n_col):
        acc = jnp.dot(u, w_ref[:, j * 512:(j + 1) * 512], preferred_element_type=F32)
        if j < 2:
            parts = []
            for h in range(4):
                t = acc[:, h * LANES:(h + 1) * LANES]
                r = (t * cos_ref[...] + pltpu.roll(t, 8, 1) * sa_ref[...]
                     + pltpu.roll(t, LANES - 8, 1) * sb_ref[...])
                parts.append(r)
            acc = jnp.concatenate(parts, axis=1)
        if j == 0:
            acc = acc * (ATT_HEAD_DIM ** -0.5 * math.log2(math.e))
        o_ref[:, j * 512:(j + 1) * 512] = acc.astype(BF16)


def _rope_tables(L):
    inv = ROPE_THETA ** (-jnp.arange(0, ROT_DIM, 2, dtype=F32) / ROT_DIM)
    pos = jnp.arange(L, dtype=F32)
    ang = pos[:, None] * inv[None, :]
    cos, sin = jnp.cos(ang), jnp.sin(ang)
    one = jnp.ones((L, 48), F32)
    zero8 = jnp.zeros((L, 8), F32)
    zero48 = jnp.zeros((L, 48), F32)
    c64 = jnp.concatenate([cos, cos, one], axis=1)
    sa64 = jnp.concatenate([zero8, sin, zero48], axis=1)
    sb64 = jnp.concatenate([-sin, zero8, zero48], axis=1)
    tile2 = lambda a: jnp.concatenate([a, a], axis=1)
    return tile2(c64), tile2(sa64), tile2(sb64)


def _inproj(xt, g, w_in_bf, L, tm=512):
    T = xt.shape[0]
    cos, sa, sb = _rope_tables(L)
    nl = L // tm
    tab = pl.BlockSpec((tm, LANES), lambda i: (i % nl, 0))
    return pl.pallas_call(
        _inproj_kernel,
        grid=(T // tm,),
        in_specs=[pl.BlockSpec((tm, D_MODEL), lambda i: (i, 0)),
                  pl.BlockSpec((1, D_MODEL), lambda i: (0, 0)),
                  pl.BlockSpec((D_MODEL, IN_W), lambda i: (0, 0)),
                  tab, tab, tab],
        out_specs=pl.BlockSpec((tm, IN_W), lambda i: (i, 0)),
        out_shape=jax.ShapeDtypeStruct((T, IN_W), BF16),
        compiler_params=_cparams("parallel"),
        name="inproj",
    )(xt, g.reshape(1, D_MODEL), w_in_bf, cos, sa, sb)


def _attn_kernel(lam_ref, q_ref, k_ref, v_ref, sg_ref, o_ref, vx_ref, *, seq, kc):
    tq = q_ref.shape[0]

    @pl.when(pl.program_id(2) == 0)
    def _():
        lane = lax.broadcasted_iota(I32, (seq, LANES), 1)
        vx_ref[:, :LANES] = v_ref[...]
        vx_ref[:, LANES:] = jnp.where(lane == 0, 1.0, 0.0).astype(BF16)

    q = q_ref[...]
    lane = lax.broadcasted_iota(I32, q.shape, 1)
    zero = jnp.zeros_like(q)
    qq = jnp.concatenate([jnp.where(lane < ATT_HEAD_DIM, q, zero),
                          jnp.where(lane >= ATT_HEAD_DIM, q, zero)], axis=0)

    rows = 2 * tq // ATT_ROW_GROUPS
    groups = [qq[g * rows:(g + 1) * rows] for g in range(ATT_ROW_GROUPS)]

    def scores(qg, c):
        start = pl.multiple_of(c * kc, kc)
        s = lax.dot_general(qg, k_ref[pl.ds(start, kc), :], (((1,), (1,)), ((), ())),
                            preferred_element_type=F32)
        return s, vx_ref[pl.ds(start, kc), :]

    if seq == kc:
        accs = []
        for qg in groups:
            s, vx = scores(qg, 0)
            p = jnp.exp2(s - jnp.max(s, axis=1, keepdims=True))
            accs.append(jnp.dot(p.astype(BF16), vx, preferred_element_type=F32))
    else:
        def body(c, carry):
            out = []
            for qg, (m, acc) in zip(groups, carry):
                s, vx = scores(qg, c)
                m_new = jnp.maximum(m, jnp.max(s, axis=1, keepdims=True))
                p = jnp.exp2(s - m_new)
                pv = jnp.dot(p.astype(BF16), vx, preferred_element_type=F32)
                out.append((m_new, jnp.exp2(m - m_new) * acc + pv))
            return tuple(out)

        init = tuple((jnp.full((rows, 1), -jnp.inf, F32), jnp.zeros((rows, 2 * LANES), F32))
                     for _ in groups)
        accs = [a for _, a in lax.fori_loop(0, seq // kc, body, init)]
    acc = jnp.concatenate(accs, axis=0)
    o = acc[:, :ATT_V_DIM] / acc[:, ATT_V_DIM:ATT_V_DIM + 1]
    a = o[:tq] - lam_ref[0] * o[tq:]
    a = _rms(a, sg_ref[...]) * (1.0 - LAM_INIT)
    o_ref[...] = a.astype(BF16)


def _diff_attention(proj, lam, subln, B, L):
    T = B * L
    tq = min(1024, L)
    nq = L // tq
    kc = min(2048, L)
    return pl.pallas_call(
        functools.partial(_attn_kernel, seq=L, kc=kc),
        grid=(B, N_ATT_HEADS, nq),
        in_specs=[pl.BlockSpec(memory_space=pltpu.SMEM),
                  pl.BlockSpec((tq, LANES), lambda b, h, i: (b * nq + i, h)),
                  pl.BlockSpec((L, LANES), lambda b, h, i: (b, 4 + h)),
                  pl.BlockSpec((L, LANES), lambda b, h, i: (b, 8 + h)),
                  pl.BlockSpec((1, ATT_V_DIM), lambda b, h, i: (0, 0))],
        out_specs=pl.BlockSpec((tq, ATT_V_DIM), lambda b, h, i: (b * nq + i, h)),
        out_shape=jax.ShapeDtypeStruct((T, ATT_V_W), BF16),
        scratch_shapes=[pltpu.VMEM((L, 2 * LANES), BF16)],
        compiler_params=_cparams("parallel", "parallel", "arbitrary"),
        name="diff_attn",
    )(lam.reshape(1), proj, proj, proj, subln.reshape(1, ATT_V_DIM))


def _hyprep_kernel(hy_ref, prev_ref, next_ref, w_ref, b_ref, fb_ref, zc_ref, x0_ref, zb_ref):
    i = pl.program_id(1)
    n = pl.num_programs(1)
    hy = hy_ref[...].astype(F32)
    tl = hy.shape[0]
    prev_row = jnp.where(i == 0, 0.0, prev_ref[15:16, :].astype(F32))
    next_row = jnp.where(i == n - 1, 0.0, next_ref[0:1, :].astype(F32))
    row = lax.broadcasted_iota(I32, hy.shape, 0)
    up = jnp.where(row == 0, prev_row, pltpu.roll(hy, 1, 0))
    dn = jnp.where(row == tl - 1, next_row, pltpu.roll(hy, tl - 1, 0))
    conv = up * w_ref[0:1, :] + hy * w_ref[1:2, :] + dn * w_ref[2:3, :] + b_ref[...]
    x0 = conv[:, :HY_WIDTH]
    x1 = conv[:, HY_WIDTH:2 * HY_WIDTH]
    hv = conv[:, 2 * HY_WIDTH:]
    z = hv * x1
    for c in range(HY_WIDTH // LANES):
        zc_ref[0, c] = z[:, c * LANES:(c + 1) * LANES].astype(zc_ref.dtype)
    x0_ref[...] = x0.astype(x0_ref.dtype)
    zb_ref[...] = (z * fb_ref[...] * x0).astype(zb_ref.dtype)


def _hyena_prep(proj, conv_w, conv_b, filt_bias, B, L, tl=512):
    T = B * L
    nl = L // tl
    hw = 3 * HY_WIDTH
    rb = tl // 16
    nrb = T // 16
    return pl.pallas_call(
        _hyprep_kernel,
        grid=(B, nl),
        in_specs=[pl.BlockSpec((tl, hw), lambda b, i: (b * nl + i, 1)),
                  pl.BlockSpec((16, hw), lambda b, i: (jnp.maximum((b * nl + i) * rb - 1, 0), 1)),
                  pl.BlockSpec((16, hw), lambda b, i: (jnp.minimum((b * nl + i + 1) * rb, nrb - 1), 1)),
                  pl.BlockSpec((3, hw), lambda b, i: (0, 0)),
                  pl.BlockSpec((1, hw), lambda b, i: (0, 0)),
                  pl.BlockSpec((1, HY_WIDTH), lambda b, i: (0, 0))],
        out_specs=[pl.BlockSpec((1, HY_WIDTH // LANES, tl, LANES), lambda b, i: (b, 0, i, 0)),
                   pl.BlockSpec((tl, HY_WIDTH), lambda b, i: (b * nl + i, 0)),
                   pl.BlockSpec((tl, HY_WIDTH), lambda b, i: (b * nl + i, 0))],
        out_shape=[jax.ShapeDtypeStruct((B, HY_WIDTH // LANES, L, LANES), BF16),
                   jax.ShapeDtypeStruct((T, HY_WIDTH), BF16),
                   jax.ShapeDtypeStruct((T, HY_WIDTH), BF16)],
        compiler_params=_cparams("parallel", "parallel"),
        name="hyena_prep",
    )(proj, proj, proj, conv_w, conv_b.reshape(1, hw), filt_bias.reshape(1, HY_WIDTH))


def _fft_dims(L):
    n = 2 * L
    n1 = 1 << ((n.bit_length() - 1 + 1) // 2)
    n2 = n // n1
    A = n1 // 2
    K1 = n1 // 2 + 1
    K1p = -(-K1 // 8) * 8
    KP = -(-2 * K1p // LANES) * LANES
    assert n1 * n2 == n and A * n2 == L and n2 % 8 == 0 and A % 16 == 0
    return n, n1, n2, A, K1, K1p, KP


def _dft_tables(L):
    n, n1, n2, A, K1, K1p, KP = _fft_dims(L)
    b = jnp.arange(n2, dtype=I32)
    a = jnp.arange(A, dtype=I32)
    k1 = jnp.arange(K1p, dtype=I32)
    t = a[None, :] * n2 + b[:, None]
    m = (k1[None, :, None] * t[:, None, :]) % n
    th = m.astype(F32) * (2.0 * math.pi / n)
    valid = (k1 < K1).astype(F32)[None, :, None]
    c, s = jnp.cos(th) * valid, jnp.sin(th) * valid
    g_fwd = jnp.concatenate([c, -s], axis=1).astype(BF16)
    w = jnp.where((k1 == 0) | (k1 == n1 // 2), 1.0, 2.0)[None, :, None] / n
    ci = jnp.transpose(c * w, (0, 2, 1))
    si = jnp.transpose(-s * w, (0, 2, 1))
    pad = jnp.zeros((n2, A, KP - 2 * K1p), F32)
    g_inv = jnp.concatenate([ci, si, pad], axis=2).astype(BF16)
    k2 = jnp.arange(n2, dtype=I32)
    ph = ((k2[:, None] * b[None, :]) % n2).astype(F32) * (2.0 * math.pi / n2)
    c2, s2 = jnp.cos(ph), jnp.sin(ph)
    f2 = jnp.concatenate([jnp.concatenate([c2, s2], 1), jnp.concatenate([-s2, c2], 1)], 0).astype(BF16)
    f2i = jnp.concatenate([jnp.concatenate([c2, -s2], 1), jnp.concatenate([s2, c2], 1)], 0).astype(BF16)
    return g_fwd, g_inv, f2, f2i


def _fft_rows_fwd(x_ref, g_ref, yf_ref, *, nb, n2, K1p):
    def body(b, _):
        off = pl.multiple_of(b * LANES, LANES)
        xb = jnp.concatenate([x_ref[i, :, pl.ds(off, LANES)] for i in range(nb)], axis=1)
        y = jnp.dot(g_ref[b], xb, preferred_element_type=F32)
        for i in range(nb):
            for r in range(2 * K1p // 8):
                row = pl.multiple_of((r * n2 + b) * 8, 8)
                yf_ref[i, pl.ds(row, 8), :] = y[r * 8:(r + 1) * 8, i * LANES:(i + 1) * LANES]
        return 0
    lax.fori_loop(0, n2, body, 0, unroll=FFT_UNROLL)


def _fft_cols_fwd(yf_ref, f2_ref, k1, *, nb, n2, K1p):
    kt, ks = k1 // 8, k1 % 8
    parts = []
    for i in range(nb):
        re = yf_ref[i, pl.ds(kt * (n2 * 8) + ks, n2, stride=8), :]
        im = yf_ref[i, pl.ds((K1p // 8 + kt) * (n2 * 8) + ks, n2, stride=8), :]
        parts.append(jnp.concatenate([re, im], axis=0))
    yk = jnp.concatenate(parts, axis=1).astype(BF16)
    return jnp.dot(f2_ref[...], yk, preferred_element_type=F32)


def _kspec_kernel(fb_ref, l1_ref, g_ref, f2_ref, kf_ref, yf_ref, *, n2, K1p):
    _fft_rows_fwd(fb_ref, g_ref, yf_ref, nb=2, n2=n2, K1p=K1p)
    inv = 1.0 / l1_ref[...]

    def body(k1, _):
        z = _fft_cols_fwd(yf_ref, f2_ref, k1, nb=2, n2=n2, K1p=K1p)
        zf, zb = z[:, :LANES], z[:, LANES:]
        kr = (zf[:n2] + zb[:n2]) * inv
        ki = (zf[n2:] - zb[n2:]) * inv
        kf_ref[k1] = jnp.concatenate([kr, ki], axis=0).astype(kf_ref.dtype)
        return 0
    lax.fori_loop(0, K1p, body, 0, unroll=FFT_UNROLL)


def _lconv_kernel(x_ref, g_ref, f2_ref, f2i_ref, kf_ref, gi_ref, o_ref, yf_ref, wf_ref, *,
                  nb, n2, K1p, KP):
    _fft_rows_fwd(x_ref, g_ref, yf_ref, nb=nb, n2=n2, K1p=K1p)

    def kbody(k1, _):
        z = _fft_cols_fwd(yf_ref, f2_ref, k1, nb=nb, n2=n2, K1p=K1p)
        kf = kf_ref[k1].astype(F32)
        kr = jnp.concatenate([kf[:n2]] * nb, axis=1)
        ki = jnp.concatenate([kf[n2:]] * nb, axis=1)
        zr, zi = z[:n2], z[n2:]
        p = jnp.concatenate([zr * kr - zi * ki, zr * ki + zi * kr], axis=0).astype(BF16)
        v = jnp.dot(f2i_ref[...], p, preferred_element_type=F32)
        for i in range(nb):
            for r in range(2 * n2 // 8):
                row = pl.multiple_of((r * K1p + k1) * 8, 8)
                wf_ref[i, pl.ds(row, 8), :] = v[r * 8:(r + 1) * 8, i * LANES:(i + 1) * LANES]
        return 0
    lax.fori_loop(0, K1p, kbody, 0, unroll=FFT_UNROLL)

    def bbody(b, _):
        bt, bs = b // 8, b % 8
        parts = []
        for i in range(nb):
            re = wf_ref[i, pl.ds(bt * (K1p * 8) + bs, K1p, stride=8), :]
            im = wf_ref[i, pl.ds((n2 // 8 + bt) * (K1p * 8) + bs, K1p, stride=8), :]
            parts.append(jnp.concatenate([re, im, jnp.zeros((KP - 2 * K1p, LANES), F32)], axis=0))
        vb = jnp.concatenate(parts, axis=1).astype(BF16)
        ob = jnp.dot(gi_ref[b], vb, preferred_element_type=F32)
        off = pl.multiple_of(b * LANES, LANES)
        for i in range(nb):
            o_ref[i, :, pl.ds(off, LANES)] = ob[:, i * LANES:(i + 1) * LANES].astype(o_ref.dtype)
        return 0
    lax.fori_loop(0, n2, bbody, 0, unroll=FFT_UNROLL)


def _filter_kernel(w1_ref, b1_ref, w2_ref, b2_ref, w3_ref, fr_ref, fb_ref, l1_ref, *, seq):
    i = pl.program_id(0)
    tl = fb_ref.shape[2]
    bands = (FILTER_EMB - 1) // 2
    pos = (i * tl + lax.broadcasted_iota(I32, (tl, 1), 0)).astype(F32)
    t = pos * (1.0 / (seq - 1))
    wpos = pos * (2.0 * math.pi / seq)
    lane = lax.broadcasted_iota(I32, (1, LANES), 1)
    band = jnp.where(lane <= bands, lane - 1, lane - 1 - bands).astype(F32)
    fvec = 1e-4 + band * ((bands - 1 - 1e-4) / (bands - 1))
    arg = wpos * fvec
    z = jnp.where(lane == 0, t, jnp.where(lane <= bands, jnp.cos(arg),
                                          jnp.where(lane <= 2 * bands, -jnp.sin(arg), 0.0)))
    fr = fr_ref[...]
    h = jnp.sin(fr * (jnp.dot(z, w1_ref[...], preferred_element_type=F32) + b1_ref[...]))
    h = jnp.sin(fr * (jnp.dot(h, w2_ref[...], preferred_element_type=F32) + b2_ref[...]))
    h = jnp.dot(h, w3_ref[...], preferred_element_type=F32)
    max_decay = math.log(DECAY_TARGET) / FAST_DECAY_PCT
    min_decay = math.log(DECAY_TARGET) / SLOW_DECAY_PCT
    ch = lax.broadcasted_iota(I32, (1, HY_WIDTH), 1).astype(F32)
    deltas = jnp.abs(min_decay + ch * ((max_decay - min_decay) / (HY_WIDTH - 1)))
    decay = jnp.exp(-t * deltas)
    fwd = h[:, :HY_WIDTH] * decay
    bwd = jnp.where(pos == 0.0, 0.0, h[:, HY_WIDTH:] * decay)
    for c in range(HY_WIDTH // LANES):
        fb_ref[0, c] = fwd[:, c * LANES:(c + 1) * LANES].astype(fb_ref.dtype)
        fb_ref[1, c] = bwd[:, c * LANES:(c + 1) * LANES].astype(fb_ref.dtype)
    part = jnp.sum(jnp.abs(fwd) + jnp.abs(bwd), axis=0, keepdims=True)

    @pl.when(i == 0)
    def _():
        l1_ref[...] = jnp.zeros_like(l1_ref)
    l1_ref[...] += part


def _filter_spectrum(L, W, tables, tl=512):
    n, n1, n2, A, K1, K1p, KP = _fft_dims(L)
    g_fwd, _, f2, _ = tables
    nc = HY_WIDTH // LANES
    w1 = jnp.pad(W['filt_w1'], ((0, LANES - FILTER_EMB), (0, 0)))
    full = lambda r, c: pl.BlockSpec((r, c), lambda i: (0, 0))
    fb, l1 = pl.pallas_call(
        functools.partial(_filter_kernel, seq=L),
        grid=(L // tl,),
        in_specs=[full(LANES, FILTER_HIDDEN), full(1, FILTER_HIDDEN),
                  full(FILTER_HIDDEN, FILTER_HIDDEN), full(1, FILTER_HIDDEN),
                  full(FILTER_HIDDEN, 2 * HY_WIDTH), full(1, FILTER_HIDDEN)],
        out_specs=[pl.BlockSpec((2, nc, tl, LANES), lambda i: (0, 0, i, 0)),
                   pl.BlockSpec((1, HY_WIDTH), lambda i: (0, 0))],
        out_shape=[jax.ShapeDtypeStruct((2, nc, L, LANES), BF16),
                   jax.ShapeDtypeStruct((1, HY_WIDTH), F32)],
        compiler_params=_cparams("arbitrary"),
        name="hyena_filter",
    )(w1, W['filt_b1'].reshape(1, -1), W['filt_w2'], W['filt_b2'].reshape(1, -1), W['filt_w3'],
      W['filt_freq'].reshape(1, -1))
    fbr = fb.reshape(2, nc, A, n2 * LANES)
    rows = (2 * K1p // 8) * n2 * 8
    return pl.pallas_call(
        functools.partial(_kspec_kernel, n2=n2, K1p=K1p),
        grid=(nc,),
        in_specs=[pl.BlockSpec((2, None, A, n2 * LANES), lambda c: (0, c, 0, 0)),
                  pl.BlockSpec((1, LANES), lambda c: (0, c)),
                  pl.BlockSpec((n2, 2 * K1p, A), lambda c: (0, 0, 0)),
                  pl.BlockSpec((2 * n2, 2 * n2), lambda c: (0, 0))],
        out_specs=pl.BlockSpec((None, K1p, 2 * n2, LANES), lambda c: (c, 0, 0, 0)),
        out_shape=jax.ShapeDtypeStruct((nc, K1p, 2 * n2, LANES), BF16),
        scratch_shapes=[pltpu.VMEM((2, rows, LANES), F32)],
        compiler_params=_cparams("arbitrary"),
        name="filter_spectrum",
    )(fbr, l1, g_fwd, f2)


def _long_conv(zc, kf, tables, nb):
    B, nc, L, _ = zc.shape
    n, n1, n2, A, K1, K1p, KP = _fft_dims(L)
    g_fwd, g_inv, f2, f2i = tables
    x = zc.reshape(B, nc, A, n2 * LANES)
    rows_y = (2 * K1p // 8) * n2 * 8
    rows_w = (2 * n2 // 8) * K1p * 8
    const = lambda shape: pl.BlockSpec(shape, lambda c, j: (0,) * len(shape),
                                       pipeline_mode=pl.Buffered(1))
    y = pl.pallas_call(
        functools.partial(_lconv_kernel, nb=nb, n2=n2, K1p=K1p, KP=KP),
        grid=(nc, B // nb),
        in_specs=[pl.BlockSpec((nb, None, A, n2 * LANES), lambda c, j: (j, c, 0, 0)),
                  const((n2, 2 * K1p, A)), const((2 * n2, 2 * n2)), const((2 * n2, 2 * n2)),
                  pl.BlockSpec((None, K1p, 2 * n2, LANES), lambda c, j: (c, 0, 0, 0)),
                  const((n2, A, KP))],
        out_specs=pl.BlockSpec((nb, None, A, n2 * LANES), lambda c, j: (j, c, 0, 0)),
        out_shape=jax.ShapeDtypeStruct((B, nc, A, n2 * LANES), BF16),
        scratch_shapes=[pltpu.VMEM((nb, rows_y, LANES), F32), pltpu.VMEM((nb, rows_w, LANES), F32)],
        compiler_params=_cparams("arbitrary", "arbitrary"),
        name="long_conv",
    )(x, g_fwd, f2, f2i, kf, g_inv)
    return y.reshape(B, nc, L, LANES)


def _merge_kernel(x_ref, a_ref, y_ref, x0_ref, zb_ref, g0_ref, g1_ref, wa_ref, wh_ref, wo_ref, o_ref):
    a = jnp.dot(a_ref[...], wa_ref[...], preferred_element_type=F32)
    x0 = x0_ref[...].astype(F32)
    y = jnp.concatenate([y_ref[0, c].astype(F32) for c in range(HY_WIDTH // LANES)], axis=1)
    hz = (y * x0 + zb_ref[...].astype(F32)).astype(BF16)
    hzp = jnp.dot(hz, wh_ref[...], preferred_element_type=F32)
    g0 = jax.nn.sigmoid(g0_ref[...].astype(F32))
    g1 = jax.nn.sigmoid(g1_ref[...].astype(F32))
    m = (g0 * a + g1 * hzp).astype(BF16)
    o_ref[...] = x_ref[...] + jnp.dot(m, wo_ref[...], preferred_element_type=F32)


def _merge(xt, a, yc, x0, zb, proj, wa, wh, wo, B, L, tm=512):
    T = B * L
    nl = L // tm
    full = lambda r, c: pl.BlockSpec((r, c), lambda b, i: (0, 0))
    tok = lambda c, cb=0: pl.BlockSpec((tm, c), lambda b, i, cb=cb: (b * nl + i, cb))
    return pl.pallas_call(
        _merge_kernel,
        grid=(B, nl),
        in_specs=[tok(D_MODEL), tok(ATT_V_W),
                  pl.BlockSpec((1, HY_WIDTH // LANES, tm, LANES), lambda b, i: (b, 0, i, 0)),
                  tok(HY_WIDTH), tok(HY_WIDTH), tok(D_MODEL, 3), tok(D_MODEL, 4),
                  full(ATT_V_W, D_MODEL), full(HY_WIDTH, D_MODEL), full(D_MODEL, D_MODEL)],
        out_specs=tok(D_MODEL),
        out_shape=jax.ShapeDtypeStruct((T, D_MODEL), F32),
        compiler_params=_cparams("parallel", "parallel"),
        name="merge",
    )(xt, a, yc, x0, zb, proj, proj, wa, wh, wo)


def _memkv_kernel(m_ref, g_ref, w_ref, o_ref):
    u = _rms(m_ref[...], g_ref[...]).astype(BF16)
    o_ref[...] = jnp.dot(u, w_ref[...], preferred_element_type=F32).astype(BF16)


def _mem_kv(mem_t, g, w_xkv_bf):
    R = mem_t.shape[0]
    return pl.pallas_call(
        _memkv_kernel,
        grid=(R // N_MEM,),
        in_specs=[pl.BlockSpec((N_MEM, D_MODEL), lambda i: (i, 0)),
                  pl.BlockSpec((1, D_MODEL), lambda i: (0, 0)),
                  pl.BlockSpec((D_MODEL, 2 * D_MODEL), lambda i: (0, 0))],
        out_specs=pl.BlockSpec((N_MEM, 2 * D_MODEL), lambda i: (i, 0)),
        out_shape=jax.ShapeDtypeStruct((R, 2 * D_MODEL), BF16),
        compiler_params=_cparams("parallel"),
        name="mem_kv",
    )(mem_t, g.reshape(1, D_MODEL), w_xkv_bf)


def _xattn_kernel(x_ref, kv_ref, gx_ref, wq_ref, wo_ref, gf_ref, wr_ref, x2_ref, h_ref, aff_ref):
    x = x_ref[...]
    u = _rms(x, gx_ref[...]).astype(BF16)
    q = jnp.dot(u, wq_ref[...], preferred_element_type=F32)
    q = (q * (X_HEAD_DIM ** -0.5)).astype(BF16)
    outs = []
    for h in range(X_HEADS):
        qh = q[:, h * X_HEAD_DIM:(h + 1) * X_HEAD_DIM]
        kh = kv_ref[:, h * X_HEAD_DIM:(h + 1) * X_HEAD_DIM]
        vh = kv_ref[:, D_MODEL + h * X_HEAD_DIM:D_MODEL + (h + 1) * X_HEAD_DIM]
        s = lax.dot_general(qh, kh, (((1,), (1,)), ((), ())), preferred_element_type=F32)
        s = s - jnp.max(s, axis=1, keepdims=True)
        p = jnp.exp(s)
        p = p / jnp.sum(p, axis=1, keepdims=True)
        outs.append(jnp.dot(p.astype(BF16), vh, preferred_element_type=F32))
    o = jnp.concatenate(outs, axis=1).astype(BF16)
    x2 = x + jnp.dot(o, wo_ref[...], preferred_element_type=F32)
    x2_ref[...] = x2
    hf = _rms(x2, gf_ref[...]).astype(BF16)
    h_ref[...] = hf
    lg = lax.dot_general(wr_ref[...], hf, (((1,), (1,)), ((), ())), preferred_element_type=F32)
    lg = lg - jnp.max(lg, axis=0, keepdims=True)
    e = jnp.exp(lg)
    aff_ref[...] = e / jnp.sum(e, axis=0, keepdims=True)


def _cross_attn(x1, kv, gx, wq, wo, gf, wr_t, B, L, tm=512):
    T = B * L
    nl = L // tm
    full = lambda r, c: pl.BlockSpec((r, c), lambda b, i: (0, 0))
    tok = lambda c: pl.BlockSpec((tm, c), lambda b, i: (b * nl + i, 0))
    return pl.pallas_call(
        _xattn_kernel,
        grid=(B, nl),
        in_specs=[tok(D_MODEL),
                  pl.BlockSpec((N_MEM, 2 * D_MODEL), lambda b, i: (b, 0)),
                  full(1, D_MODEL), full(D_MODEL, D_MODEL), full(D_MODEL, D_MODEL),
                  full(1, D_MODEL), full(N_EXPERTS, D_MODEL)],
        out_specs=[tok(D_MODEL), tok(D_MODEL),
                   pl.BlockSpec((N_EXPERTS, tm), lambda b, i: (0, b * nl + i))],
        out_shape=[jax.ShapeDtypeStruct((T, D_MODEL), F32),
                   jax.ShapeDtypeStruct((T, D_MODEL), BF16),
                   jax.ShapeDtypeStruct((N_EXPERTS, T), F32)],
        compiler_params=_cparams("parallel", "parallel"),
        name="cross_attn",
    )(x1, kv, gx.reshape(1, D_MODEL), wq, wo, gf.reshape(1, D_MODEL), wr_t)


def _select_kernel(aff_ref, gate_ref, *, cap):
    aff = aff_ref[...]
    bits = pltpu.bitcast(aff, I32)
    T = aff.shape[1]

    def count(mask):
        return jnp.sum(mask.astype(I32), axis=1, keepdims=True)

    def vbody(k, thr):
        cand = thr | (jnp.int32(1) << (30 - k))
        return jnp.where(count(bits >= cand) >= cap, cand, thr)

    thr = lax.fori_loop(0, 31, vbody, jnp.zeros((N_EXPERTS, 1), I32))
    gt = bits > thr
    eq = bits == thr
    need = cap - count(gt)
    tok = lax.broadcasted_iota(I32, aff.shape, 1)
    nbit = int(T).bit_length()

    def jbody(k, j):
        cand = j + (jnp.int32(1) << (nbit - 1 - k))
        return jnp.where(count(eq & (tok < cand)) <= need, cand, j)

    j = lax.fori_loop(0, nbit, jbody, jnp.zeros((N_EXPERTS, 1), I32))
    sel = gt | (eq & (tok < j))
    gate_ref[...] = jnp.where(sel, aff, 0.0)


def _select(aff_t, cap):
    E, T = aff_t.shape
    return pl.pallas_call(
        functools.partial(_select_kernel, cap=cap),
        grid=(1,),
        in_specs=[pl.BlockSpec((E, T), lambda i: (0, 0))],
        out_specs=pl.BlockSpec((E, T), lambda i: (0, 0)),
        out_shape=jax.ShapeDtypeStruct((E, T), F32),
        compiler_params=_cparams("arbitrary"),
        name="ec_select",
    )(aff_t)


EC_TILE = 256
EC_TILE_SEQS = 16


def _ec_tiling(B, L):
    tb = min(B, EC_TILE_SEQS)
    tl = EC_TILE // tb
    assert B % tb == 0 and L % tl == 0 and tl % 16 == 0
    return tb, tl, (48 if tb >= EC_TILE_SEQS else 64)


def _to_tile_order(a, B, L):
    tb, tl, _ = _ec_tiling(B, L)
    E = a.shape[0]
    return a.reshape(E, B // tb, tb, L // tl, tl).transpose(0, 3, 1, 2, 4).reshape(E, B * L)


def _rank_kernel(g_ref, rank_ref, cnt_ref):
    n = g_ref.shape[1]
    sel = jnp.where(g_ref[...] > 0.0, 1.0, 0.0).astype(BF16)
    r = lax.broadcasted_iota(I32, (n, n), 0)
    c = lax.broadcasted_iota(I32, (n, n), 1)
    before = jnp.where(r < c, 1.0, 0.0).astype(BF16)
    rank_ref[...] = jnp.dot(sel, before, preferred_element_type=F32)
    cnt_ref[...] = jnp.dot(sel, jnp.ones((n, LANES), BF16), preferred_element_type=F32)


def _tile_ranks(gate):
    E, T = gate.shape
    rows = E * T // EC_TILE
    rb = min(512, rows)
    rank, cnt = pl.pallas_call(
        _rank_kernel,
        grid=(rows // rb,),
        in_specs=[pl.BlockSpec((rb, EC_TILE), lambda i: (i, 0))],
        out_specs=[pl.BlockSpec((rb, EC_TILE), lambda i: (i, 0)),
                   pl.BlockSpec((rb, LANES), lambda i: (i, 0))],
        out_shape=[jax.ShapeDtypeStruct((rows, EC_TILE), F32),
                   jax.ShapeDtypeStruct((rows, LANES), F32)],
        compiler_params=_cparams("parallel"),
        name="ec_rank",
    )(gate.reshape(rows, EC_TILE))
    return rank.reshape(E, T), cnt[:, 0].reshape(E, T // EC_TILE)


def _dispatch_kernel(h_ref, rank_ref, gate_ref, xe_ref):
    slots = xe_ref.shape[1]
    slot = lax.broadcasted_iota(I32, (slots, EC_TILE), 0).astype(F32)
    parts = []
    for e in range(N_EXPERTS):
        hit = (rank_ref[e:e + 1, :] == slot) & (gate_ref[e:e + 1, :] > 0.0)
        parts.append(jnp.where(hit, 1.0, 0.0).astype(BF16))
    onehot = jnp.concatenate(parts, axis=0)
    h = h_ref[...].reshape(EC_TILE, D_MODEL)
    xe = jnp.dot(onehot, h, preferred_element_type=F32).astype(BF16)
    xe_ref[...] = xe.reshape(N_EXPERTS, slots, D_MODEL)


def _tile_spec(B, L, width):
    tb, tl, _ = _ec_tiling(B, L)
    groups = B // tb
    return pl.BlockSpec((tb, tl, width), lambda j, *_: (j % groups, j // groups, 0))


def _dispatch(h, rank, gate, B, L):
    T = B * L
    nt = T // EC_TILE
    slots = _ec_tiling(B, L)[2]
    return pl.pallas_call(
        _dispatch_kernel,
        grid=(nt,),
        in_specs=[_tile_spec(B, L, D_MODEL),
                  pl.BlockSpec((N_EXPERTS, EC_TILE), lambda i: (0, i)),
                  pl.BlockSpec((N_EXPERTS, EC_TILE), lambda i: (0, i))],
        out_specs=pl.BlockSpec((N_EXPERTS, None, slots, D_MODEL), lambda i: (0, i, 0, 0)),
        out_shape=jax.ShapeDtypeStruct((N_EXPERTS, nt, slots, D_MODEL), BF16),
        compiler_params=_cparams("parallel"),
        name="ec_dispatch",
    )(h.reshape(B, L, D_MODEL), rank, gate)


def _swiglu(h, wg_ref, wu_ref, wd_ref):
    half = EXPERT_FF // 2
    y = None
    for c in range(2):
        sl = slice(c * half, (c + 1) * half)
        a = jnp.dot(h, wg_ref[:, sl], preferred_element_type=F32)
        b = jnp.dot(h, wu_ref[:, sl], preferred_element_type=F32)
        he = (jax.nn.silu(a) * b).astype(BF16)
        d = jnp.dot(he, wd_ref[sl, :], preferred_element_type=F32)
        y = d if y is None else y + d
    return y


def _expert_rows_kernel(x_ref, wg_ref, wu_ref, wd_ref, o_ref):
    o_ref[0] = _swiglu(x_ref[0], wg_ref.at[0], wu_ref.at[0], wd_ref.at[0]).astype(o_ref.dtype)


def _expert_rows(xe, wg, wu, wd):
    E, R, _ = xe.shape
    rt = math.gcd(R, 1024)
    rows = pl.BlockSpec((1, rt, D_MODEL), lambda e, j: (e, j, 0))
    wspec = lambda r, c: pl.BlockSpec((1, r, c), lambda e, j: (e, 0, 0))
    return pl.pallas_call(
        _expert_rows_kernel,
        grid=(E, R // rt),
        in_specs=[rows, wspec(D_MODEL, EXPERT_FF), wspec(D_MODEL, EXPERT_FF), wspec(EXPERT_FF, D_MODEL)],
        out_specs=rows,
        out_shape=jax.ShapeDtypeStruct((E, R, D_MODEL), BF16),
        compiler_params=_cparams("parallel", "parallel"),
        name="expert_rows",
    )(xe, wg, wu, wd)


def _combine_kernel(flag_ref, tflag_ref, ye_ref, rank_ref, gate_ref, ex_ref, sl_ref, x_ref, h_ref,
                    gn_ref, wg_hbm, wu_hbm, wd_hbm, o_ref, wbuf, sem):
    i = pl.program_id(0)
    slots = ye_ref.shape[1]
    blk = o_ref.shape
    ye = ye_ref[...].reshape(N_EXPERTS * slots, D_MODEL)
    rank = rank_ref[...]
    gate = gate_ref[...]
    r = jnp.dot(rank.astype(BF16), ex_ref[...], preferred_element_type=F32)
    g = jnp.dot(gate.astype(BF16), ex_ref[...], preferred_element_type=F32)
    q = jnp.where(r == sl_ref[...], g, 0.0).astype(BF16)
    o_ref[...] = x_ref[...] + jnp.dot(q, ye, preferred_element_type=F32).reshape(blk)

    @pl.when(tflag_ref[i] > 0)
    def _():
        def expert(e, carry):
            @pl.when(flag_ref[i * N_EXPERTS + e] > 0)
            def _():
                copies = [pltpu.make_async_copy(w.at[e], wbuf.at[k], sem.at[k])
                          for k, w in enumerate((wg_hbm, wu_hbm, wd_hbm))]
                for cp in copies:
                    cp.start()
                for cp in copies:
                    cp.wait()
                lane = lax.broadcasted_iota(I32, rank.shape, 1)
                keep = (lane == e) & (rank >= float(slots))
                ge = jnp.sum(jnp.where(keep, gate, 0.0), axis=1, keepdims=True)
                h = h_ref[...].reshape(EC_TILE, D_MODEL)
                o_ref[...] += (_swiglu(h, wbuf.at[0], wbuf.at[1], wbuf.at[2]) * ge).reshape(blk)
            return carry
        lax.fori_loop(0, N_EXPERTS, expert, 0)

    o_ref[...] = _rms(o_ref[...], gn_ref[...])


def _combine(ye, rank_t, gate_t, x2, h, flags, tflags, wg, wu, wd, gn, B, L):
    T = B * L
    nt = T // EC_TILE
    slots = ye.shape[2]
    es = N_EXPERTS * slots
    lane = jnp.arange(es, dtype=I32)
    expand = (lane[None, :] // slots == jnp.arange(N_EXPERTS, dtype=I32)[:, None]).astype(BF16)
    slot = (lane % slots).astype(F32).reshape(1, es)
    tok = lambda c: pl.BlockSpec((EC_TILE, c), lambda i, f, t: (i, 0))
    full = lambda r, c: pl.BlockSpec((r, c), lambda i, f, t: (0, 0))
    tile = _tile_spec(B, L, D_MODEL)
    hbm = pl.BlockSpec(memory_space=pl.ANY)
    return pl.pallas_call(
        _combine_kernel,
        grid_spec=pltpu.PrefetchScalarGridSpec(
            num_scalar_prefetch=2,
            grid=(nt,),
            in_specs=[pl.BlockSpec((N_EXPERTS, None, slots, D_MODEL), lambda i, f, t: (0, i, 0, 0)),
                      tok(N_EXPERTS), tok(N_EXPERTS), full(N_EXPERTS, es), full(1, es),
                      tile, tile, full(1, D_MODEL), hbm, hbm, hbm],
            out_specs=tile,
            scratch_shapes=[pltpu.VMEM((3, D_MODEL, EXPERT_FF), BF16), pltpu.SemaphoreType.DMA((3,))]),
        out_shape=jax.ShapeDtypeStruct((B, L, D_MODEL), F32),
        compiler_params=_cparams("arbitrary"),
        name="ec_combine",
    )(flags, tflags, ye, rank_t, gate_t, expand, slot, x2.reshape(B, L, D_MODEL),
      h.reshape(B, L, D_MODEL), gn.reshape(1, D_MODEL), wg, wu, wd)


def _expert_mixture(h, gate, x2, wg, wu, wd, gn, B, L):
    nt = B * L // EC_TILE
    slots = _ec_tiling(B, L)[2]
    gate = _to_tile_order(gate, B, L)
    rank, cnt = _tile_ranks(gate)
    xe = _dispatch(h, rank, gate, B, L)
    ye = _expert_rows(xe.reshape(N_EXPERTS, nt * slots, D_MODEL), wg, wu, wd)
    over = cnt.T > slots
    flags = over.reshape(-1).astype(I32)
    tflags = over.any(axis=1).astype(I32)
    return _combine(ye.reshape(N_EXPERTS, nt, slots, D_MODEL), rank.T, gate.T, x2, h, flags, tflags,
                    wg, wu, wd, gn, B, L)


def _run_group(x, mem, W):
    B, L, _ = x.shape
    T = B * L
    xt = x.reshape(T, D_MODEL)
    proj = _inproj(xt, W['norm_mix'], W['w_in'], L)
    a = _diff_attention(proj, W['lam'], W['subln'], B, L)
    zc, x0, zb = _hyena_prep(proj, W['hy_conv_w'], W['hy_conv_b'], W['filt_bias'], B, L)
    tables = _dft_tables(L)
    kf = _filter_spectrum(L, W, tables)
    yc = _long_conv(zc, kf, tables, nb=1 if L >= 4096 else 2)
    x1 = _merge(xt, a, yc, x0, zb, proj, W['w_br_attn'], W['w_br_hyena'], W['w_out'], B, L)
    kv = _mem_kv(mem.reshape(B * N_MEM, D_MODEL), W['norm_mem'], W['w_xkv'])
    x2, hf, aff = _cross_attn(x1, kv, W['norm_x'], W['w_xq'], W['w_xo'], W['norm_ffn'],
                              W['w_router_t'], B, L)
    gate = _select(aff, EC_CAPACITY_FACTOR * T // N_EXPERTS)
    return _expert_mixture(hf, gate, x2, W['w_exp_gate'], W['w_exp_up'], W['w_exp_down'],
                           W['norm_final'], B, L)


def kernel(x_prompt, x_sample, mem_prompt, mem_sample, norm_mix, w_in, lambda_q1, lambda_k1, lambda_q2, lambda_k2, subln, w_br_attn, hy_conv_w, hy_conv_b, filt_w1, filt_b1, filt_w2, filt_b2, filt_w3, filt_freq, filt_bias, w_br_hyena, w_out, norm_x, norm_mem, w_xq, w_xkv, w_xo, norm_ffn, w_router, w_exp_gate, w_exp_up, w_exp_down, norm_final):
    l = 0
    lam = (jnp.exp(jnp.sum(lambda_q1[l] * lambda_k1[l])) - jnp.exp(jnp.sum(lambda_q2[l] * lambda_k2[l]))
           + LAM_INIT).astype(F32)
    W = dict(
        norm_mix=norm_mix[l], w_in=w_in[l].astype(BF16), lam=lam, subln=subln[l],
        w_br_attn=w_br_attn[l].astype(BF16), hy_conv_w=hy_conv_w[l], hy_conv_b=hy_conv_b[l],
        filt_w1=filt_w1[l], filt_b1=filt_b1[l], filt_w2=filt_w2[l], filt_b2=filt_b2[l],
        filt_w3=filt_w3[l], filt_freq=filt_freq[l], filt_bias=filt_bias[l],
        w_br_hyena=w_br_hyena[l].astype(BF16), w_out=w_out[l].astype(BF16),
        norm_x=norm_x[l], norm_mem=norm_mem[l], w_xq=w_xq[l].astype(BF16),
        w_xkv=w_xkv[l].astype(BF16), w_xo=w_xo[l].astype(BF16), norm_ffn=norm_ffn[l],
        w_router_t=w_router[l].T.astype(BF16), w_exp_gate=w_exp_gate[l].astype(BF16),
        w_exp_up=w_exp_up[l].astype(BF16), w_exp_down=w_exp_down[l].astype(BF16),
        norm_final=norm_final)
    return (_run_group(x_prompt, mem_prompt, W), _run_group(x_sample, mem_sample, W))
```

```python
import functools
import math

import jax
import jax.numpy as jnp
from jax import lax
from jax.experimental import pallas as pl
from jax.experimental.pallas import tpu as pltpu

F32 = jnp.float32
BF16 = jnp.bfloat16
I32 = jnp.int32

D_MODEL = 1024
N_ATT_HEADS = 4
ATT_HEAD_DIM = 64
ATT_V_DIM = 128
ROT_DIM = 16
ROPE_THETA = 500000.0
ATT_QK_W = 512
ATT_V_W = 512
HY_WIDTH = 512
FILTER_EMB = 33
FILTER_HIDDEN = 64
DECAY_TARGET = 1e-2
FAST_DECAY_PCT = 0.3
SLOW_DECAY_PCT = 1.5
IN_W = 5120
N_MEM = 256
X_HEADS = 4
X_HEAD_DIM = 256
N_EXPERTS = 16
EC_CAPACITY_FACTOR = 2
EXPERT_FF = 1024
EPS = 1e-6
LAM_INIT = 0.8 - 0.6 * math.exp(-0.3 * 0)

LANES = 128
ATT_ROW_GROUPS = 16
ATT_Q_TILE = 2048
FFT_UNROLL = 8
VMEM_LIMIT_BYTES = 56 * 1024 * 1024


def _cparams(*sem):
    return pltpu.CompilerParams(dimension_semantics=sem, vmem_limit_bytes=VMEM_LIMIT_BYTES)


def _rms(x, g):
    return x * lax.rsqrt(jnp.mean(x * x, axis=-1, keepdims=True) + EPS) * g


def _inproj_kernel(x_ref, g_ref, w_ref, cos_ref, sa_ref, sb_ref, o_ref):
    u = _rms(x_ref[...], g_ref[...]).astype(BF16)
    n_col = IN_W // 512
    for j in range(n_col):
        acc = jnp.dot(u, w_ref[:, j * 512:(j + 1) * 512], preferred_element_type=F32)
        if j < 2:
            parts = []
            for h in range(4):
                t = acc[:, h * LANES:(h + 1) * LANES]
                r = (t * cos_ref[...] + pltpu.roll(t, 8, 1) * sa_ref[...]
                     + pltpu.roll(t, LANES - 8, 1) * sb_ref[...])
                parts.append(r)
            acc = jnp.concatenate(parts, axis=1)
        if j == 0:
            acc = acc * (ATT_HEAD_DIM ** -0.5 * math.log2(math.e))
        o_ref[:, j * 512:(j + 1) * 512] = acc.astype(BF16)


def _rope_tables(L):
    inv = ROPE_THETA ** (-jnp.arange(0, ROT_DIM, 2, dtype=F32) / ROT_DIM)
    pos = jnp.arange(L, dtype=F32)
    ang = pos[:, None] * inv[None, :]
    cos, sin = jnp.cos(ang), jnp.sin(ang)
    one = jnp.ones((L, 48), F32)
    zero8 = jnp.zeros((L, 8), F32)
    zero48 = jnp.zeros((L, 48), F32)
    c64 = jnp.concatenate([cos, cos, one], axis=1)
    sa64 = jnp.concatenate([zero8, sin, zero48], axis=1)
    sb64 = jnp.concatenate([-sin, zero8, zero48], axis=1)
    tile2 = lambda a: jnp.concatenate([a, a], axis=1)
    return tile2(c64), tile2(sa64), tile2(sb64)


def _inproj(xt, g, w_in_bf, L, tm=512):
    T = xt.shape[0]
    cos, sa, sb = _rope_tables(L)
    nl = L // tm
    tab = pl.BlockSpec((tm, LANES), lambda i: (i % nl, 0))
    return pl.pallas_call(
        _inproj_kernel,
        grid=(T // tm,),
        in_specs=[pl.BlockSpec((tm, D_MODEL), lambda i: (i, 0)),
                  pl.BlockSpec((1, D_MODEL), lambda i: (0, 0)),
                  pl.BlockSpec((D_MODEL, IN_W), lambda i: (0, 0)),
                  tab, tab, tab],
        out_specs=pl.BlockSpec((tm, IN_W), lambda i: (i, 0)),
        out_shape=jax.ShapeDtypeStruct((T, IN_W), BF16),
        compiler_params=_cparams("parallel"),
        name="inproj",
    )(xt, g.reshape(1, D_MODEL), w_in_bf, cos, sa, sb)


def _attn_kernel(lam_ref, q_ref, k_ref, v_ref, sg_ref, o_ref, vx_ref, *, seq, kc):
    tq = q_ref.shape[0]

    @pl.when(pl.program_id(2) == 0)
    def _():
        lane = lax.broadcasted_iota(I32, (seq, LANES), 1)
        vx_ref[:, :LANES] = v_ref[...]
        vx_ref[:, LANES:] = jnp.where(lane == 0, 1.0, 0.0).astype(BF16)

    q = q_ref[...]
    lane = lax.broadcasted_iota(I32, q.shape, 1)
    zero = jnp.zeros_like(q)
    qq = jnp.concatenate([jnp.where(lane < ATT_HEAD_DIM, q, zero),
                          jnp.where(lane >= ATT_HEAD_DIM, q, zero)], axis=0)

    rows = 2 * tq // ATT_ROW_GROUPS
    groups = [qq[g * rows:(g + 1) * rows] for g in range(ATT_ROW_GROUPS)]

    def scores(qg, c):
        start = pl.multiple_of(c * kc, kc)
        s = lax.dot_general(qg, k_ref[pl.ds(start, kc), :], (((1,), (1,)), ((), ())),
                            preferred_element_type=F32)
        return s, vx_ref[pl.ds(start, kc), :]

    if seq == kc:
        accs = []
        for qg in groups:
            s, vx = scores(qg, 0)
            p = jnp.exp2(s - jnp.max(s, axis=1, keepdims=True))
            accs.append(jnp.dot(p.astype(BF16), vx, preferred_element_type=F32))
    else:
        def body(c, carry):
            out = []
            for qg, (m, acc) in zip(groups, carry):
                s, vx = scores(qg, c)
                m_new = jnp.maximum(m, jnp.max(s, axis=1, keepdims=True))
                p = jnp.exp2(s - m_new)
                pv = jnp.dot(p.astype(BF16), vx, preferred_element_type=F32)
                out.append((m_new, jnp.exp2(m - m_new) * acc + pv))
            return tuple(out)

        init = tuple((jnp.full((rows, 1), -jnp.inf, F32), jnp.zeros((rows, 2 * LANES), F32))
                     for _ in groups)
        accs = [a for _, a in lax.fori_loop(0, seq // kc, body, init)]
    acc = jnp.concatenate(accs, axis=0)
    o = acc[:, :ATT_V_DIM] / acc[:, ATT_V_DIM:ATT_V_DIM + 1]
    a = o[:tq] - lam_ref[0] * o[tq:]
    a = _rms(a, sg_ref[...]) * (1.0 - LAM_INIT)
    o_ref[...] = a.astype(BF16)


def _diff_attention(proj, lam, subln, B, L):
    T = B * L
    tq = min(ATT_Q_TILE, L)
    nq = L // tq
    kc = min(2048, L)
    return pl.pallas_call(
        functools.partial(_attn_kernel, seq=L, kc=kc),
        grid=(B, N_ATT_HEADS, nq),
        in_specs=[pl.BlockSpec(memory_space=pltpu.SMEM),
                  pl.BlockSpec((tq, LANES), lambda b, h, i: (b * nq + i, h)),
                  pl.BlockSpec((L, LANES), lambda b, h, i: (b, 4 + h)),
                  pl.BlockSpec((L, LANES), lambda b, h, i: (b, 8 + h)),
                  pl.BlockSpec((1, ATT_V_DIM), lambda b, h, i: (0, 0))],
        out_specs=pl.BlockSpec((tq, ATT_V_DIM), lambda b, h, i: (b * nq + i, h)),
        out_shape=jax.ShapeDtypeStruct((T, ATT_V_W), BF16),
        scratch_shapes=[pltpu.VMEM((L, 2 * LANES), BF16)],
        compiler_params=_cparams("parallel", "parallel", "arbitrary"),
        name="diff_attn",
    )(lam.reshape(1), proj, proj, proj, subln.reshape(1, ATT_V_DIM))


def _hyprep_kernel(hy_ref, prev_ref, next_ref, w_ref, b_ref, fb_ref, zc_ref, x0_ref, zb_ref):
    i = pl.program_id(1)
    n = pl.num_programs(1)
    hy = hy_ref[...].astype(F32)
    tl = hy.shape[0]
    prev_row = jnp.where(i == 0, 0.0, prev_ref[15:16, :].astype(F32))
    next_row = jnp.where(i == n - 1, 0.0, next_ref[0:1, :].astype(F32))
    row = lax.broadcasted_iota(I32, hy.shape, 0)
    up = jnp.where(row == 0, prev_row, pltpu.roll(hy, 1, 0))
    dn = jnp.where(row == tl - 1, next_row, pltpu.roll(hy, tl - 1, 0))
    conv = up * w_ref[0:1, :] + hy * w_ref[1:2, :] + dn * w_ref[2:3, :] + b_ref[...]
    x0 = conv[:, :HY_WIDTH]
    x1 = conv[:, HY_WIDTH:2 * HY_WIDTH]
    hv = conv[:, 2 * HY_WIDTH:]
    z = hv * x1
    for c in range(HY_WIDTH // LANES):
        zc_ref[0, c] = z[:, c * LANES:(c + 1) * LANES].astype(zc_ref.dtype)
    x0_ref[...] = x0.astype(x0_ref.dtype)
    zb_ref[...] = (z * fb_ref[...] * x0).astype(zb_ref.dtype)


def _hyena_prep(proj, conv_w, conv_b, filt_bias, B, L, tl=512):
    T = B * L
    nl = L // tl
    hw = 3 * HY_WIDTH
    rb = tl // 16
    nrb = T // 16
    return pl.pallas_call(
        _hyprep_kernel,
        grid=(B, nl),
        in_specs=[pl.BlockSpec((tl, hw), lambda b, i: (b * nl + i, 1)),
                  pl.BlockSpec((16, hw), lambda b, i: (jnp.maximum((b * nl + i) * rb - 1, 0), 1)),
                  pl.BlockSpec((16, hw), lambda b, i: (jnp.minimum((b * nl + i + 1) * rb, nrb - 1), 1)),
                  pl.BlockSpec((3, hw), lambda b, i: (0, 0)),
                  pl.BlockSpec((1, hw), lambda b, i: (0, 0)),
                  pl.BlockSpec((1, HY_WIDTH), lambda b, i: (0, 0))],
        out_specs=[pl.BlockSpec((1, HY_WIDTH // LANES, tl, LANES), lambda b, i: (b, 0, i, 0)),
                   pl.BlockSpec((tl, HY_WIDTH), lambda b, i: (b * nl + i, 0)),
                   pl.BlockSpec((tl, HY_WIDTH), lambda b, i: (b * nl + i, 0))],
        out_shape=[jax.ShapeDtypeStruct((B, HY_WIDTH // LANES, L, LANES), BF16),
                   jax.ShapeDtypeStruct((T, HY_WIDTH), BF16),
                   jax.ShapeDtypeStruct((T, HY_WIDTH), BF16)],
        compiler_params=_cparams("parallel", "parallel"),
        name="hyena_prep",
    )(proj, proj, proj, conv_w, conv_b.reshape(1, hw), filt_bias.reshape(1, HY_WIDTH))


def _fft_dims(L):
    n = 2 * L
    n1 = 1 << ((n.bit_length() - 1 + 1) // 2)
    n2 = n // n1
    A = n1 // 2
    K1 = n1 // 2 + 1
    K1p = -(-K1 // 8) * 8
    KP = -(-2 * K1p // LANES) * LANES
    assert n1 * n2 == n and A * n2 == L and n2 % 8 == 0 and A % 16 == 0
    return n, n1, n2, A, K1, K1p, KP


def _dft_tables(L):
    n, n1, n2, A, K1, K1p, KP = _fft_dims(L)
    b = jnp.arange(n2, dtype=I32)
    a = jnp.arange(A, dtype=I32)
    k1 = jnp.arange(K1p, dtype=I32)
    t = a[None, :] * n2 + b[:, None]
    m = (k1[None, :, None] * t[:, None, :]) % n
    th = m.astype(F32) * (2.0 * math.pi / n)
    valid = (k1 < K1).astype(F32)[None, :, None]
    c, s = jnp.cos(th) * valid, jnp.sin(th) * valid
    g_fwd = jnp.concatenate([c, -s], axis=1).astype(BF16)
    w = jnp.where((k1 == 0) | (k1 == n1 // 2), 1.0, 2.0)[None, :, None] / n
    ci = jnp.transpose(c * w, (0, 2, 1))
    si = jnp.transpose(-s * w, (0, 2, 1))
    pad = jnp.zeros((n2, A, KP - 2 * K1p), F32)
    g_inv = jnp.concatenate([ci, si, pad], axis=2).astype(BF16)
    k2 = jnp.arange(n2, dtype=I32)
    ph = ((k2[:, None] * b[None, :]) % n2).astype(F32) * (2.0 * math.pi / n2)
    c2, s2 = jnp.cos(ph), jnp.sin(ph)
    f2 = jnp.concatenate([jnp.concatenate([c2, s2], 1), jnp.concatenate([-s2, c2], 1)], 0).astype(BF16)
    f2i = jnp.concatenate([jnp.concatenate([c2, -s2], 1), jnp.concatenate([s2, c2], 1)], 0).astype(BF16)
    return g_fwd, g_inv, f2, f2i


def _fft_rows_fwd(x_ref, g_ref, yf_ref, *, nb, n2, K1p):
    def body(b, _):
        off = pl.multiple_of(b * LANES, LANES)
        xb = jnp.concatenate([x_ref[i, :, pl.ds(off, LANES)] for i in range(nb)], axis=1)
        y = jnp.dot(g_ref[b], xb, preferred_element_type=F32)
        for i in range(nb):
            for r in range(2 * K1p // 8):
                row = pl.multiple_of((r * n2 + b) * 8, 8)
                yf_ref[i, pl.ds(row, 8), :] = y[r * 8:(r + 1) * 8, i * LANES:(i + 1) * LANES]
        return 0
    lax.fori_loop(0, n2, body, 0, unroll=FFT_UNROLL)


def _fft_cols_fwd(yf_ref, f2_ref, k1, *, nb, n2, K1p):
    kt, ks = k1 // 8, k1 % 8
    parts = []
    for i in range(nb):
        re = yf_ref[i, pl.ds(kt * (n2 * 8) + ks, n2, stride=8), :]
        im = yf_ref[i, pl.ds((K1p // 8 + kt) * (n2 * 8) + ks, n2, stride=8), :]
        parts.append(jnp.concatenate([re, im], axis=0))
    yk = jnp.concatenate(parts, axis=1).astype(BF16)
    return jnp.dot(f2_ref[...], yk, preferred_element_type=F32)


def _kspec_kernel(fb_ref, l1_ref, g_ref, f2_ref, kf_ref, yf_ref, *, n2, K1p):
    _fft_rows_fwd(fb_ref, g_ref, yf_ref, nb=2, n2=n2, K1p=K1p)
    inv = 1.0 / l1_ref[...]

    def body(k1, _):
        z = _fft_cols_fwd(yf_ref, f2_ref, k1, nb=2, n2=n2, K1p=K1p)
        zf, zb = z[:, :LANES], z[:, LANES:]
        kr = (zf[:n2] + zb[:n2]) * inv
        ki = (zf[n2:] - zb[n2:]) * inv
        kf_ref[k1] = jnp.concatenate([kr, ki], axis=0).astype(kf_ref.dtype)
        return 0
    lax.fori_loop(0, K1p, body, 0, unroll=FFT_UNROLL)


def _lconv_kernel(x_ref, g_ref, f2_ref, f2i_ref, kf_ref, gi_ref, o_ref, yf_ref, wf_ref, *,
                  nb, n2, K1p, KP):
    _fft_rows_fwd(x_ref, g_ref, yf_ref, nb=nb, n2=n2, K1p=K1p)

    def kbody(k1, _):
        z = _fft_cols_fwd(yf_ref, f2_ref, k1, nb=nb, n2=n2, K1p=K1p)
        kf = kf_ref[k1].astype(F32)
        kr = jnp.concatenate([kf[:n2]] * nb, axis=1)
        ki = jnp.concatenate([kf[n2:]] * nb, axis=1)
        zr, zi = z[:n2], z[n2:]
        p = jnp.concatenate([zr * kr - zi * ki, zr * ki + zi * kr], axis=0).astype(BF16)
        v = jnp.dot(f2i_ref[...], p, preferred_element_type=F32)
        for i in range(nb):
            for r in range(2 * n2 // 8):
                row = pl.multiple_of((r * K1p + k1) * 8, 8)
                wf_ref[i, pl.ds(row, 8), :] = v[r * 8:(r + 1) * 8, i * LANES:(i + 1) * LANES]
        return 0
    lax.fori_loop(0, K1p, kbody, 0, unroll=FFT_UNROLL)

    def bbody(b, _):
        bt, bs = b // 8, b % 8
        parts = []
        for i in range(nb):
            re = wf_ref[i, pl.ds(bt * (K1p * 8) + bs, K1p, stride=8), :]
            im = wf_ref[i, pl.ds((n2 // 8 + bt) * (K1p * 8) + bs, K1p, stride=8), :]
            parts.append(jnp.concatenate([re, im, jnp.zeros((KP - 2 * K1p, LANES), F32)], axis=0))
        vb = jnp.concatenate(parts, axis=1).astype(BF16)
        ob = jnp.dot(gi_ref[b], vb, preferred_element_type=F32)
        off = pl.multiple_of(b * LANES, LANES)
        for i in range(nb):
            o_ref[i, :, pl.ds(off, LANES)] = ob[:, i * LANES:(i + 1) * LANES].astype(o_ref.dtype)
        return 0
    lax.fori_loop(0, n2, bbody, 0, unroll=FFT_UNROLL)


def _filter_kernel(w1_ref, b1_ref, w2_ref, b2_ref, w3_ref, fr_ref, fb_ref, l1_ref, *, seq):
    i = pl.program_id(0)
    tl = fb_ref.shape[2]
    bands = (FILTER_EMB - 1) // 2
    pos = (i * tl + lax.broadcasted_iota(I32, (tl, 1), 0)).astype(F32)
    t = pos * (1.0 / (seq - 1))
    wpos = pos * (2.0 * math.pi / seq)
    lane = lax.broadcasted_iota(I32, (1, LANES), 1)
    band = jnp.where(lane <= bands, lane - 1, lane - 1 - bands).astype(F32)
    fvec = 1e-4 + band * ((bands - 1 - 1e-4) / (bands - 1))
    arg = wpos * fvec
    z = jnp.where(lane == 0, t, jnp.where(lane <= bands, jnp.cos(arg),
                                          jnp.where(lane <= 2 * bands, -jnp.sin(arg), 0.0)))
    fr = fr_ref[...]
    h = jnp.sin(fr * (jnp.dot(z, w1_ref[...], preferred_element_type=F32) + b1_ref[...]))
    h = jnp.sin(fr * (jnp.dot(h, w2_ref[...], preferred_element_type=F32) + b2_ref[...]))
    h = jnp.dot(h, w3_ref[...], preferred_element_type=F32)
    max_decay = math.log(DECAY_TARGET) / FAST_DECAY_PCT
    min_decay = math.log(DECAY_TARGET) / SLOW_DECAY_PCT
    ch = lax.broadcasted_iota(I32, (1, HY_WIDTH), 1).astype(F32)
    deltas = jnp.abs(min_decay + ch * ((max_decay - min_decay) / (HY_WIDTH - 1)))
    decay = jnp.exp(-t * deltas)
    fwd = h[:, :HY_WIDTH] * decay
    bwd = jnp.where(pos == 0.0, 0.0, h[:, HY_WIDTH:] * decay)
    for c in range(HY_WIDTH // LANES):
        fb_ref[0, c] = fwd[:, c * LANES:(c + 1) * LANES].astype(fb_ref.dtype)
        fb_ref[1, c] = bwd[:, c * LANES:(c + 1) * LANES].astype(fb_ref.dtype)
    part = jnp.sum(jnp.abs(fwd) + jnp.abs(bwd), axis=0, keepdims=True)

    @pl.when(i == 0)
    def _():
        l1_ref[...] = jnp.zeros_like(l1_ref)
    l1_ref[...] += part


def _filter_spectrum(L, W, tables, tl=512):
    n, n1, n2, A, K1, K1p, KP = _fft_dims(L)
    g_fwd, _, f2, _ = tables
    nc = HY_WIDTH // LANES
    w1 = jnp.pad(W['filt_w1'], ((0, LANES - FILTER_EMB), (0, 0)))
    full = lambda r, c: pl.BlockSpec((r, c), lambda i: (0, 0))
    fb, l1 = pl.pallas_call(
        functools.partial(_filter_kernel, seq=L),
        grid=(L // tl,),
        in_specs=[full(LANES, FILTER_HIDDEN), full(1, FILTER_HIDDEN),
                  full(FILTER_HIDDEN, FILTER_HIDDEN), full(1, FILTER_HIDDEN),
                  full(FILTER_HIDDEN, 2 * HY_WIDTH), full(1, FILTER_HIDDEN)],
        out_specs=[pl.BlockSpec((2, nc, tl, LANES), lambda i: (0, 0, i, 0)),
                   pl.BlockSpec((1, HY_WIDTH), lambda i: (0, 0))],
        out_shape=[jax.ShapeDtypeStruct((2, nc, L, LANES), BF16),
                   jax.ShapeDtypeStruct((1, HY_WIDTH), F32)],
        compiler_params=_cparams("arbitrary"),
        name="hyena_filter",
    )(w1, W['filt_b1'].reshape(1, -1), W['filt_w2'], W['filt_b2'].reshape(1, -1), W['filt_w3'],
      W['filt_freq'].reshape(1, -1))
    fbr = fb.reshape(2, nc, A, n2 * LANES)
    rows = (2 * K1p // 8) * n2 * 8
    return pl.pallas_call(
        functools.partial(_kspec_kernel, n2=n2, K1p=K1p),
        grid=(nc,),
        in_specs=[pl.BlockSpec((2, None, A, n2 * LANES), lambda c: (0, c, 0, 0)),
                  pl.BlockSpec((1, LANES), lambda c: (0, c)),
                  pl.BlockSpec((n2, 2 * K1p, A), lambda c: (0, 0, 0)),
                  pl.BlockSpec((2 * n2, 2 * n2), lambda c: (0, 0))],
        out_specs=pl.BlockSpec((None, K1p, 2 * n2, LANES), lambda c: (c, 0, 0, 0)),
        out_shape=jax.ShapeDtypeStruct((nc, K1p, 2 * n2, LANES), BF16),
        scratch_shapes=[pltpu.VMEM((2, rows, LANES), F32)],
        compiler_params=_cparams("arbitrary"),
        name="filter_spectrum",
    )(fbr, l1, g_fwd, f2)


def _long_conv(zc, kf, tables, nb):
    B, nc, L, _ = zc.shape
    n, n1, n2, A, K1, K1p, KP = _fft_dims(L)
    g_fwd, g_inv, f2, f2i = tables
    x = zc.reshape(B, nc, A, n2 * LANES)
    rows_y = (2 * K1p // 8) * n2 * 8
    rows_w = (2 * n2 // 8) * K1p * 8
    const = lambda shape: pl.BlockSpec(shape, lambda c, j: (0,) * len(shape),
                                       pipeline_mode=pl.Buffered(1))
    y = pl.pallas_call(
        functools.partial(_lconv_kernel, nb=nb, n2=n2, K1p=K1p, KP=KP),
        grid=(nc, B // nb),
        in_specs=[pl.BlockSpec((nb, None, A, n2 * LANES), lambda c, j: (j, c, 0, 0)),
                  const((n2, 2 * K1p, A)), const((2 * n2, 2 * n2)), const((2 * n2, 2 * n2)),
                  pl.BlockSpec((None, K1p, 2 * n2, LANES), lambda c, j: (c, 0, 0, 0)),
                  const((n2, A, KP))],
        out_specs=pl.BlockSpec((nb, None, A, n2 * LANES), lambda c, j: (j, c, 0, 0)),
        out_shape=jax.ShapeDtypeStruct((B, nc, A, n2 * LANES), BF16),
        scratch_shapes=[pltpu.VMEM((nb, rows_y, LANES), F32), pltpu.VMEM((nb, rows_w, LANES), F32)],
        compiler_params=_cparams("arbitrary", "arbitrary"),
        name="long_conv",
    )(x, g_fwd, f2, f2i, kf, g_inv)
    return y.reshape(B, nc, L, LANES)


def _merge_rows(x_ref, a_ref, y_ref, x0_ref, zb_ref, g0_ref, g1_ref, wa_ref, wh_ref, wo_ref):
    a = jnp.dot(a_ref[...], wa_ref[...], preferred_element_type=F32)
    x0 = x0_ref[...].astype(F32)
    y = jnp.concatenate([y_ref[0, c].astype(F32) for c in range(HY_WIDTH // LANES)], axis=1)
    hz = (y * x0 + zb_ref[...].astype(F32)).astype(BF16)
    hzp = jnp.dot(hz, wh_ref[...], preferred_element_type=F32)
    g0 = jax.nn.sigmoid(g0_ref[...].astype(F32))
    g1 = jax.nn.sigmoid(g1_ref[...].astype(F32))
    m = (g0 * a + g1 * hzp).astype(BF16)
    return x_ref[...] + jnp.dot(m, wo_ref[...], preferred_element_type=F32)


def _memkv_kernel(m_ref, g_ref, w_ref, o_ref):
    u = _rms(m_ref[...], g_ref[...]).astype(BF16)
    o_ref[...] = jnp.dot(u, w_ref[...], preferred_element_type=F32).astype(BF16)


def _mem_kv(mem_t, g, w_xkv_bf):
    R = mem_t.shape[0]
    return pl.pallas_call(
        _memkv_kernel,
        grid=(R // N_MEM,),
        in_specs=[pl.BlockSpec((N_MEM, D_MODEL), lambda i: (i, 0)),
                  pl.BlockSpec((1, D_MODEL), lambda i: (0, 0)),
                  pl.BlockSpec((D_MODEL, 2 * D_MODEL), lambda i: (0, 0))],
        out_specs=pl.BlockSpec((N_MEM, 2 * D_MODEL), lambda i: (i, 0)),
        out_shape=jax.ShapeDtypeStruct((R, 2 * D_MODEL), BF16),
        compiler_params=_cparams("parallel"),
        name="mem_kv",
    )(mem_t, g.reshape(1, D_MODEL), w_xkv_bf)


def _mix_xattn_kernel(x_ref, a_ref, y_ref, x0_ref, zb_ref, g0_ref, g1_ref, wa_ref, wh_ref, wmo_ref,
                      kv_ref, gx_ref, wq_ref, wo_ref, gf_ref, wr_ref, x2_ref, h_ref, aff_ref):
    x = _merge_rows(x_ref, a_ref, y_ref, x0_ref, zb_ref, g0_ref, g1_ref, wa_ref, wh_ref, wmo_ref)
    u = _rms(x, gx_ref[...]).astype(BF16)
    q = jnp.dot(u, wq_ref[...], preferred_element_type=F32)
    q = (q * (X_HEAD_DIM ** -0.5)).astype(BF16)
    outs = []
    for h in range(X_HEADS):
        qh = q[:, h * X_HEAD_DIM:(h + 1) * X_HEAD_DIM]
        kh = kv_ref[:, h * X_HEAD_DIM:(h + 1) * X_HEAD_DIM]
        vh = kv_ref[:, D_MODEL + h * X_HEAD_DIM:D_MODEL + (h + 1) * X_HEAD_DIM]
        s = lax.dot_general(qh, kh, (((1,), (1,)), ((), ())), preferred_element_type=F32)
        s = s - jnp.max(s, axis=1, keepdims=True)
        p = jnp.exp(s)
        p = p / jnp.sum(p, axis=1, keepdims=True)
        outs.append(jnp.dot(p.astype(BF16), vh, preferred_element_type=F32))
    o = jnp.concatenate(outs, axis=1).astype(BF16)
    x2 = x + jnp.dot(o, wo_ref[...], preferred_element_type=F32)
    x2_ref[...] = x2
    hf = _rms(x2, gf_ref[...]).astype(BF16)
    h_ref[...] = hf
    lg = lax.dot_general(wr_ref[...], hf, (((1,), (1,)), ((), ())), preferred_element_type=F32)
    lg = lg - jnp.max(lg, axis=0, keepdims=True)
    e = jnp.exp(lg)
    aff_ref[...] = e / jnp.sum(e, axis=0, keepdims=True)


def _merge_cross_attn(xt, a, yc, x0, zb, proj, wa, wh, wmo, kv, gx, wq, wo, gf, wr_t, B, L, tm=512):
    T = B * L
    nl = L // tm
    full = lambda r, c: pl.BlockSpec((r, c), lambda b, i: (0, 0))
    tok = lambda c, cb=0: pl.BlockSpec((tm, c), lambda b, i, cb=cb: (b * nl + i, cb))
    return pl.pallas_call(
        _mix_xattn_kernel,
        grid=(B, nl),
        in_specs=[tok(D_MODEL), tok(ATT_V_W),
                  pl.BlockSpec((1, HY_WIDTH // LANES, tm, LANES), lambda b, i: (b, 0, i, 0)),
                  tok(HY_WIDTH), tok(HY_WIDTH), tok(D_MODEL, 3), tok(D_MODEL, 4),
                  full(ATT_V_W, D_MODEL), full(HY_WIDTH, D_MODEL), full(D_MODEL, D_MODEL),
                  pl.BlockSpec((N_MEM, 2 * D_MODEL), lambda b, i: (b, 0)),
                  full(1, D_MODEL), full(D_MODEL, D_MODEL), full(D_MODEL, D_MODEL),
                  full(1, D_MODEL), full(N_EXPERTS, D_MODEL)],
        out_specs=[tok(D_MODEL), tok(D_MODEL),
                   pl.BlockSpec((N_EXPERTS, tm), lambda b, i: (0, b * nl + i))],
        out_shape=[jax.ShapeDtypeStruct((T, D_MODEL), F32),
                   jax.ShapeDtypeStruct((T, D_MODEL), BF16),
                   jax.ShapeDtypeStruct((N_EXPERTS, T), F32)],
        compiler_params=_cparams("parallel", "parallel"),
        name="merge_cross_attn",
    )(xt, a, yc, x0, zb, proj, proj, wa, wh, wmo, kv, gx.reshape(1, D_MODEL), wq, wo,
      gf.reshape(1, D_MODEL), wr_t)


def _select_kernel(aff_ref, gate_ref, *, cap):
    aff = aff_ref[...]
    bits = pltpu.bitcast(aff, I32)
    T = aff.shape[1]

    def count(mask):
        return jnp.sum(mask.astype(I32), axis=1, keepdims=True)

    def vbody(k, thr):
        cand = thr | (jnp.int32(1) << (30 - k))
        return jnp.where(count(bits >= cand) >= cap, cand, thr)

    thr = lax.fori_loop(0, 31, vbody, jnp.zeros((N_EXPERTS, 1), I32))
    gt = bits > thr
    eq = bits == thr
    need = cap - count(gt)
    tok = lax.broadcasted_iota(I32, aff.shape, 1)
    nbit = int(T).bit_length()

    def jbody(k, j):
        cand = j + (jnp.int32(1) << (nbit - 1 - k))
        return jnp.where(count(eq & (tok < cand)) <= need, cand, j)

    j = lax.fori_loop(0, nbit, jbody, jnp.zeros((N_EXPERTS, 1), I32))
    sel = gt | (eq & (tok < j))
    gate_ref[...] = jnp.where(sel, aff, 0.0)


def _select(aff_t, cap):
    E, T = aff_t.shape
    return pl.pallas_call(
        functools.partial(_select_kernel, cap=cap),
        grid=(1,),
        in_specs=[pl.BlockSpec((E, T), lambda i: (0, 0))],
        out_specs=pl.BlockSpec((E, T), lambda i: (0, 0)),
        out_shape=jax.ShapeDtypeStruct((E, T), F32),
        compiler_params=_cparams("arbitrary"),
        name="ec_select",
    )(aff_t)


EC_TILE = 256
EC_TILE_SEQS = 16


def _ec_tiling(B, L):
    tb = min(B, EC_TILE_SEQS)
    tl = EC_TILE // tb
    assert B % tb == 0 and L % tl == 0 and tl % 16 == 0
    return tb, tl, (48 if tb >= EC_TILE_SEQS else 64)


def _to_tile_order(a, B, L):
    tb, tl, _ = _ec_tiling(B, L)
    E = a.shape[0]
    return a.reshape(E, B // tb, tb, L // tl, tl).transpose(0, 3, 1, 2, 4).reshape(E, B * L)


def _rank_kernel(g_ref, rank_ref, cnt_ref):
    n = g_ref.shape[1]
    sel = jnp.where(g_ref[...] > 0.0, 1.0, 0.0).astype(BF16)
    r = lax.broadcasted_iota(I32, (n, n), 0)
    c = lax.broadcasted_iota(I32, (n, n), 1)
    before = jnp.where(r < c, 1.0, 0.0).astype(BF16)
    rank_ref[...] = jnp.dot(sel, before, preferred_element_type=F32)
    cnt_ref[...] = jnp.dot(sel, jnp.ones((n, LANES), BF16), preferred_element_type=F32)


def _tile_ranks(gate):
    E, T = gate.shape
    rows = E * T // EC_TILE
    rb = min(512, rows)
    rank, cnt = pl.pallas_call(
        _rank_kernel,
        grid=(rows // rb,),
        in_specs=[pl.BlockSpec((rb, EC_TILE), lambda i: (i, 0))],
        out_specs=[pl.BlockSpec((rb, EC_TILE), lambda i: (i, 0)),
                   pl.BlockSpec((rb, LANES), lambda i: (i, 0))],
        out_shape=[jax.ShapeDtypeStruct((rows, EC_TILE), F32),
                   jax.ShapeDtypeStruct((rows, LANES), F32)],
        compiler_params=_cparams("parallel"),
        name="ec_rank",
    )(gate.reshape(rows, EC_TILE))
    return rank.reshape(E, T), cnt[:, 0].reshape(E, T // EC_TILE)


def _dispatch_kernel(h_ref, rank_ref, gate_ref, xe_ref):
    slots = xe_ref.shape[1]
    slot = lax.broadcasted_iota(I32, (slots, EC_TILE), 0).astype(F32)
    parts = []
    for e in range(N_EXPERTS):
        hit = (rank_ref[e:e + 1, :] == slot) & (gate_ref[e:e + 1, :] > 0.0)
        parts.append(jnp.where(hit, 1.0, 0.0).astype(BF16))
    onehot = jnp.concatenate(parts, axis=0)
    h = h_ref[...].reshape(EC_TILE, D_MODEL)
    xe = jnp.dot(onehot, h, preferred_element_type=F32).astype(BF16)
    xe_ref[...] = xe.reshape(N_EXPERTS, slots, D_MODEL)


def _tile_spec(B, L, width):
    tb, tl, _ = _ec_tiling(B, L)
    groups = B // tb
    return pl.BlockSpec((tb, tl, width), lambda j, *_: (j % groups, j // groups, 0))


def _dispatch(h, rank, gate, B, L):
    T = B * L
    nt = T // EC_TILE
    slots = _ec_tiling(B, L)[2]
    return pl.pallas_call(
        _dispatch_kernel,
        grid=(nt,),
        in_specs=[_tile_spec(B, L, D_MODEL),
                  pl.BlockSpec((N_EXPERTS, EC_TILE), lambda i: (0, i)),
                  pl.BlockSpec((N_EXPERTS, EC_TILE), lambda i: (0, i))],
        out_specs=pl.BlockSpec((N_EXPERTS, None, slots, D_MODEL), lambda i: (0, i, 0, 0)),
        out_shape=jax.ShapeDtypeStruct((N_EXPERTS, nt, slots, D_MODEL), BF16),
        compiler_params=_cparams("parallel"),
        name="ec_dispatch",
    )(h.reshape(B, L, D_MODEL), rank, gate)


def _swiglu(h, wg_ref, wu_ref, wd_ref):
    half = EXPERT_FF // 2
    y = None
    for c in range(2):
        sl = slice(c * half, (c + 1) * half)
        a = jnp.dot(h, wg_ref[:, sl], preferred_element_type=F32)
        b = jnp.dot(h, wu_ref[:, sl], preferred_element_type=F32)
        he = (jax.nn.silu(a) * b).astype(BF16)
        d = jnp.dot(he, wd_ref[sl, :], preferred_element_type=F32)
        y = d if y is None else y + d
    return y


def _expert_rows_kernel(x_ref, wg_ref, wu_ref, wd_ref, o_ref):
    o_ref[0] = _swiglu(x_ref[0], wg_ref.at[0], wu_ref.at[0], wd_ref.at[0]).astype(o_ref.dtype)


def _expert_rows(xe, wg, wu, wd):
    E, R, _ = xe.shape
    rt = math.gcd(R, 1024)
    rows = pl.BlockSpec((1, rt, D_MODEL), lambda e, j: (e, j, 0))
    wspec = lambda r, c: pl.BlockSpec((1, r, c), lambda e, j: (e, 0, 0))
    return pl.pallas_call(
        _expert_rows_kernel,
        grid=(E, R // rt),
        in_specs=[rows, wspec(D_MODEL, EXPERT_FF), wspec(D_MODEL, EXPERT_FF), wspec(EXPERT_FF, D_MODEL)],
        out_specs=rows,
        out_shape=jax.ShapeDtypeStruct((E, R, D_MODEL), BF16),
        compiler_params=_cparams("parallel", "parallel"),
        name="expert_rows",
    )(xe, wg, wu, wd)


def _combine_kernel(flag_ref, tflag_ref, ye_ref, rank_ref, gate_ref, ex_ref, sl_ref, x_ref, h_ref,
                    gn_ref, wg_hbm, wu_hbm, wd_hbm, o_ref, wbuf, sem):
    i = pl.program_id(0)
    slots = ye_ref.shape[1]
    blk = o_ref.shape
    ye = ye_ref[...].reshape(N_EXPERTS * slots, D_MODEL)
    rank = rank_ref[...]
    gate = gate_ref[...]
    r = jnp.dot(rank.astype(BF16), ex_ref[...], preferred_element_type=F32)
    g = jnp.dot(gate.astype(BF16), ex_ref[...], preferred_element_type=F32)
    q = jnp.where(r == sl_ref[...], g, 0.0).astype(BF16)
    o_ref[...] = x_ref[...] + jnp.dot(q, ye, preferred_element_type=F32).reshape(blk)

    @pl.when(tflag_ref[i] > 0)
    def _():
        def expert(e, carry):
            @pl.when(flag_ref[i * N_EXPERTS + e] > 0)
            def _():
                copies = [pltpu.make_async_copy(w.at[e], wbuf.at[k], sem.at[k])
                          for k, w in enumerate((wg_hbm, wu_hbm, wd_hbm))]
                for cp in copies:
                    cp.start()
                for cp in copies:
                    cp.wait()
                lane = lax.broadcasted_iota(I32, rank.shape, 1)
                keep = (lane == e) & (rank >= float(slots))
                ge = jnp.sum(jnp.where(keep, gate, 0.0), axis=1, keepdims=True)
                h = h_ref[...].reshape(EC_TILE, D_MODEL)
                o_ref[...] += (_swiglu(h, wbuf.at[0], wbuf.at[1], wbuf.at[2]) * ge).reshape(blk)
            return carry
        lax.fori_loop(0, N_EXPERTS, expert, 0)

    o_ref[...] = _rms(o_ref[...], gn_ref[...])


def _combine(ye, rank_t, gate_t, x2, h, flags, tflags, wg, wu, wd, gn, B, L):
    T = B * L
    nt = T // EC_TILE
    slots = ye.shape[2]
    es = N_EXPERTS * slots
    lane = jnp.arange(es, dtype=I32)
    expand = (lane[None, :] // slots == jnp.arange(N_EXPERTS, dtype=I32)[:, None]).astype(BF16)
    slot = (lane % slots).astype(F32).reshape(1, es)
    tok = lambda c: pl.BlockSpec((EC_TILE, c), lambda i, f, t: (i, 0))
    full = lambda r, c: pl.BlockSpec((r, c), lambda i, f, t: (0, 0))
    tile = _tile_spec(B, L, D_MODEL)
    hbm = pl.BlockSpec(memory_space=pl.ANY)
    return pl.pallas_call(
        _combine_kernel,
        grid_spec=pltpu.PrefetchScalarGridSpec(
            num_scalar_prefetch=2,
            grid=(nt,),
            in_specs=[pl.BlockSpec((N_EXPERTS, None, slots, D_MODEL), lambda i, f, t: (0, i, 0, 0)),
                      tok(N_EXPERTS), tok(N_EXPERTS), full(N_EXPERTS, es), full(1, es),
                      tile, tile, full(1, D_MODEL), hbm, hbm, hbm],
            out_specs=tile,
            scratch_shapes=[pltpu.VMEM((3, D_MODEL, EXPERT_FF), BF16), pltpu.SemaphoreType.DMA((3,))]),
        out_shape=jax.ShapeDtypeStruct((B, L, D_MODEL), F32),
        compiler_params=_cparams("arbitrary"),
        name="ec_combine",
    )(flags, tflags, ye, rank_t, gate_t, expand, slot, x2.reshape(B, L, D_MODEL),
      h.reshape(B, L, D_MODEL), gn.reshape(1, D_MODEL), wg, wu, wd)


def _expert_mixture(h, gate, x2, wg, wu, wd, gn, B, L):
    nt = B * L // EC_TILE
    slots = _ec_tiling(B, L)[2]
    gate = _to_tile_order(gate, B, L)
    rank, cnt = _tile_ranks(gate)
    xe = _dispatch(h, rank, gate, B, L)
    ye = _expert_rows(xe.reshape(N_EXPERTS, nt * slots, D_MODEL), wg, wu, wd)
    over = cnt.T > slots
    flags = over.reshape(-1).astype(I32)
    tflags = over.any(axis=1).astype(I32)
    return _combine(ye.reshape(N_EXPERTS, nt, slots, D_MODEL), rank.T, gate.T, x2, h, flags, tflags,
                    wg, wu, wd, gn, B, L)


def _run_group(x, mem, W):
    B, L, _ = x.shape
    T = B * L
    xt = x.reshape(T, D_MODEL)
    proj = _inproj(xt, W['norm_mix'], W['w_in'], L)
    a = _diff_attention(proj, W['lam'], W['subln'], B, L)
    zc, x0, zb = _hyena_prep(proj, W['hy_conv_w'], W['hy_conv_b'], W['filt_bias'], B, L)
    tables = _dft_tables(L)
    kf = _filter_spectrum(L, W, tables)
    yc = _long_conv(zc, kf, tables, nb=1 if L >= 4096 else 2)
    kv = _mem_kv(mem.reshape(B * N_MEM, D_MODEL), W['norm_mem'], W['w_xkv'])
    x2, hf, aff = _merge_cross_attn(xt, a, yc, x0, zb, proj, W['w_br_attn'], W['w_br_hyena'],
                                    W['w_out'], kv, W['norm_x'], W['w_xq'], W['w_xo'], W['norm_ffn'],
                                    W['w_router_t'], B, L)
    gate = _select(aff, EC_CAPACITY_FACTOR * T // N_EXPERTS)
    return _expert_mixture(hf, gate, x2, W['w_exp_gate'], W['w_exp_up'], W['w_exp_down'],
                           W['norm_final'], B, L)


def kernel(x_prompt, x_sample, mem_prompt, mem_sample, norm_mix, w_in, lambda_q1, lambda_k1, lambda_q2, lambda_k2, subln, w_br_attn, hy_conv_w, hy_conv_b, filt_w1, filt_b1, filt_w2, filt_b2, filt_w3, filt_freq, filt_bias, w_br_hyena, w_out, norm_x, norm_mem, w_xq, w_xkv, w_xo, norm_ffn, w_router, w_exp_gate, w_exp_up, w_exp_down, norm_final):
    l = 0
    lam = (jnp.exp(jnp.sum(lambda_q1[l] * lambda_k1[l])) - jnp.exp(jnp.sum(lambda_q2[l] * lambda_k2[l]))
           + LAM_INIT).astype(F32)
    W = dict(
        norm_mix=norm_mix[l], w_in=w_in[l].astype(BF16), lam=lam, subln=subln[l],
        w_br_attn=w_br_attn[l].astype(BF16), hy_conv_w=hy_conv_w[l], hy_conv_b=hy_conv_b[l],
        filt_w1=filt_w1[l], filt_b1=filt_b1[l], filt_w2=filt_w2[l], filt_b2=filt_b2[l],
        filt_w3=filt_w3[l], filt_freq=filt_freq[l], filt_bias=filt_bias[l],
        w_br_hyena=w_br_hyena[l].astype(BF16), w_out=w_out[l].astype(BF16),
        norm_x=norm_x[l], norm_mem=norm_mem[l], w_xq=w_xq[l].astype(BF16),
        w_xkv=w_xkv[l].astype(BF16), w_xo=w_xo[l].astype(BF16), norm_ffn=norm_ffn[l],
        w_router_t=w_router[l].T.astype(BF16), w_exp_gate=w_exp_gate[l].astype(BF16),
        w_exp_up=w_exp_up[l].astype(BF16), w_exp_down=w_exp_down[l].astype(BF16),
        norm_final=norm_final)
    return (_run_group(x_prompt, mem_prompt, W), _run_group(x_sample, mem_sample, W))
```

```python
import functools
import math

import jax
import jax.numpy as jnp
from jax import lax
from jax.experimental import pallas as pl
from jax.experimental.pallas import tpu as pltpu

F32 = jnp.float32
BF16 = jnp.bfloat16
I32 = jnp.int32

D_MODEL = 1024
N_ATT_HEADS = 4
ATT_HEAD_DIM = 64
ATT_V_DIM = 128
ROT_DIM = 16
ROPE_THETA = 500000.0
ATT_QK_W = 512
ATT_V_W = 512
HY_WIDTH = 512
FILTER_EMB = 33
FILTER_HIDDEN = 64
DECAY_TARGET = 1e-2
FAST_DECAY_PCT = 0.3
SLOW_DECAY_PCT = 1.5
IN_W = 5120
N_MEM = 256
X_HEADS = 4
X_HEAD_DIM = 256
N_EXPERTS = 16
EC_CAPACITY_FACTOR = 2
EXPERT_FF = 1024
EPS = 1e-6
LAM_INIT = 0.8 - 0.6 * math.exp(-0.3 * 0)

LANES = 128
ATT_ROW_GROUPS = 16
ATT_Q_TILE = 2048
MIX_ROW_GROUPS = 1
FFT_UNROLL = 8
VMEM_LIMIT_BYTES = 56 * 1024 * 1024


def _cparams(*sem):
    return pltpu.CompilerParams(dimension_semantics=sem, vmem_limit_bytes=VMEM_LIMIT_BYTES)


def _rms(x, g):
    return x * lax.rsqrt(jnp.mean(x * x, axis=-1, keepdims=True) + EPS) * g


def _inproj_kernel(x_ref, g_ref, w_ref, cos_ref, sa_ref, sb_ref, o_ref):
    u = _rms(x_ref[...], g_ref[...]).astype(BF16)
    n_col = IN_W // 512
    for j in range(n_col):
        acc = jnp.dot(u, w_ref[:, j * 512:(j + 1) * 512], preferred_element_type=F32)
        if j < 2:
            parts = []
            for h in range(4):
                t = acc[:, h * LANES:(h + 1) * LANES]
                r = (t * cos_ref[...] + pltpu.roll(t, 8, 1) * sa_ref[...]
                     + pltpu.roll(t, LANES - 8, 1) * sb_ref[...])
                parts.append(r)
            acc = jnp.concatenate(parts, axis=1)
        if j == 0:
            acc = acc * (ATT_HEAD_DIM ** -0.5 * math.log2(math.e))
        o_ref[:, j * 512:(j + 1) * 512] = acc.astype(BF16)


def _rope_tables(L):
    inv = ROPE_THETA ** (-jnp.arange(0, ROT_DIM, 2, dtype=F32) / ROT_DIM)
    pos = jnp.arange(L, dtype=F32)
    ang = pos[:, None] * inv[None, :]
    cos, sin = jnp.cos(ang), jnp.sin(ang)
    one = jnp.ones((L, 48), F32)
    zero8 = jnp.zeros((L, 8), F32)
    zero48 = jnp.zeros((L, 48), F32)
    c64 = jnp.concatenate([cos, cos, one], axis=1)
    sa64 = jnp.concatenate([zero8, sin, zero48], axis=1)
    sb64 = jnp.concatenate([-sin, zero8, zero48], axis=1)
    tile2 = lambda a: jnp.concatenate([a, a], axis=1)
    return tile2(c64), tile2(sa64), tile2(sb64)


def _inproj(xt, g, w_in_bf, L, tm=512):
    T = xt.shape[0]
    cos, sa, sb = _rope_tables(L)
    nl = L // tm
    tab = pl.BlockSpec((tm, LANES), lambda i: (i % nl, 0))
    return pl.pallas_call(
        _inproj_kernel,
        grid=(T // tm,),
        in_specs=[pl.BlockSpec((tm, D_MODEL), lambda i: (i, 0)),
                  pl.BlockSpec((1, D_MODEL), lambda i: (0, 0)),
                  pl.BlockSpec((D_MODEL, IN_W), lambda i: (0, 0)),
                  tab, tab, tab],
        out_specs=pl.BlockSpec((tm, IN_W), lambda i: (i, 0)),
        out_shape=jax.ShapeDtypeStruct((T, IN_W), BF16),
        compiler_params=_cparams("parallel"),
        name="inproj",
    )(xt, g.reshape(1, D_MODEL), w_in_bf, cos, sa, sb)


def _attn_kernel(lam_ref, q_ref, k_ref, v_ref, sg_ref, o_ref, vx_ref, *, seq, kc):
    tq = q_ref.shape[0]

    @pl.when(pl.program_id(2) == 0)
    def _():
        lane = lax.broadcasted_iota(I32, (seq, LANES), 1)
        vx_ref[:, :LANES] = v_ref[...]
        vx_ref[:, LANES:] = jnp.where(lane == 0, 1.0, 0.0).astype(BF16)

    q = q_ref[...]
    lane = lax.broadcasted_iota(I32, q.shape, 1)
    zero = jnp.zeros_like(q)
    qq = jnp.concatenate([jnp.where(lane < ATT_HEAD_DIM, q, zero),
                          jnp.where(lane >= ATT_HEAD_DIM, q, zero)], axis=0)

    rows = 2 * tq // ATT_ROW_GROUPS
    groups = [qq[g * rows:(g + 1) * rows] for g in range(ATT_ROW_GROUPS)]

    def scores(qg, c):
        start = pl.multiple_of(c * kc, kc)
        s = lax.dot_general(qg, k_ref[pl.ds(start, kc), :], (((1,), (1,)), ((), ())),
                            preferred_element_type=F32)
        return s, vx_ref[pl.ds(start, kc), :]

    if seq == kc:
        accs = []
        for qg in groups:
            s, vx = scores(qg, 0)
            p = jnp.exp2(s - jnp.max(s, axis=1, keepdims=True))
            accs.append(jnp.dot(p.astype(BF16), vx, preferred_element_type=F32))
    else:
        def body(c, carry):
            out = []
            for qg, (m, acc) in zip(groups, carry):
                s, vx = scores(qg, c)
                m_new = jnp.maximum(m, jnp.max(s, axis=1, keepdims=True))
                p = jnp.exp2(s - m_new)
                pv = jnp.dot(p.astype(BF16), vx, preferred_element_type=F32)
                out.append((m_new, jnp.exp2(m - m_new) * acc + pv))
            return tuple(out)

        init = tuple((jnp.full((rows, 1), -jnp.inf, F32), jnp.zeros((rows, 2 * LANES), F32))
                     for _ in groups)
        accs = [a for _, a in lax.fori_loop(0, seq // kc, body, init)]
    acc = jnp.concatenate(accs, axis=0)
    o = acc[:, :ATT_V_DIM] / acc[:, ATT_V_DIM:ATT_V_DIM + 1]
    a = o[:tq] - lam_ref[0] * o[tq:]
    a = _rms(a, sg_ref[...]) * (1.0 - LAM_INIT)
    o_ref[...] = a.astype(BF16)


def _diff_attention(proj, lam, subln, B, L):
    T = B * L
    tq = min(ATT_Q_TILE, L)
    nq = L // tq
    kc = min(2048, L)
    return pl.pallas_call(
        functools.partial(_attn_kernel, seq=L, kc=kc),
        grid=(B, N_ATT_HEADS, nq),
        in_specs=[pl.BlockSpec(memory_space=pltpu.SMEM),
                  pl.BlockSpec((tq, LANES), lambda b, h, i: (b * nq + i, h)),
                  pl.BlockSpec((L, LANES), lambda b, h, i: (b, 4 + h)),
                  pl.BlockSpec((L, LANES), lambda b, h, i: (b, 8 + h)),
                  pl.BlockSpec((1, ATT_V_DIM), lambda b, h, i: (0, 0))],
        out_specs=pl.BlockSpec((tq, ATT_V_DIM), lambda b, h, i: (b * nq + i, h)),
        out_shape=jax.ShapeDtypeStruct((T, ATT_V_W), BF16),
        scratch_shapes=[pltpu.VMEM((L, 2 * LANES), BF16)],
        compiler_params=_cparams("parallel", "parallel", "arbitrary"),
        name="diff_attn",
    )(lam.reshape(1), proj, proj, proj, subln.reshape(1, ATT_V_DIM))


def _hyprep_kernel(hy_ref, prev_ref, next_ref, w_ref, b_ref, fb_ref, zc_ref, x0_ref, zb_ref):
    i = pl.program_id(1)
    n = pl.num_programs(1)
    hy = hy_ref[...].astype(F32)
    tl = hy.shape[0]
    prev_row = jnp.where(i == 0, 0.0, prev_ref[15:16, :].astype(F32))
    next_row = jnp.where(i == n - 1, 0.0, next_ref[0:1, :].astype(F32))
    row = lax.broadcasted_iota(I32, hy.shape, 0)
    up = jnp.where(row == 0, prev_row, pltpu.roll(hy, 1, 0))
    dn = jnp.where(row == tl - 1, next_row, pltpu.roll(hy, tl - 1, 0))
    conv = up * w_ref[0:1, :] + hy * w_ref[1:2, :] + dn * w_ref[2:3, :] + b_ref[...]
    x0 = conv[:, :HY_WIDTH]
    x1 = conv[:, HY_WIDTH:2 * HY_WIDTH]
    hv = conv[:, 2 * HY_WIDTH:]
    z = hv * x1
    for c in range(HY_WIDTH // LANES):
        zc_ref[0, c] = z[:, c * LANES:(c + 1) * LANES].astype(zc_ref.dtype)
    x0_ref[...] = x0.astype(x0_ref.dtype)
    zb_ref[...] = (z * fb_ref[...] * x0).astype(zb_ref.dtype)


def _hyena_prep(proj, conv_w, conv_b, filt_bias, B, L, tl=512):
    T = B * L
    nl = L // tl
    hw = 3 * HY_WIDTH
    rb = tl // 16
    nrb = T // 16
    return pl.pallas_call(
        _hyprep_kernel,
        grid=(B, nl),
        in_specs=[pl.BlockSpec((tl, hw), lambda b, i: (b * nl + i, 1)),
                  pl.BlockSpec((16, hw), lambda b, i: (jnp.maximum((b * nl + i) * rb - 1, 0), 1)),
                  pl.BlockSpec((16, hw), lambda b, i: (jnp.minimum((b * nl + i + 1) * rb, nrb - 1), 1)),
                  pl.BlockSpec((3, hw), lambda b, i: (0, 0)),
                  pl.BlockSpec((1, hw), lambda b, i: (0, 0)),
                  pl.BlockSpec((1, HY_WIDTH), lambda b, i: (0, 0))],
        out_specs=[pl.BlockSpec((1, HY_WIDTH // LANES, tl, LANES), lambda b, i: (b, 0, i, 0)),
                   pl.BlockSpec((tl, HY_WIDTH), lambda b, i: (b * nl + i, 0)),
                   pl.BlockSpec((tl, HY_WIDTH), lambda b, i: (b * nl + i, 0))],
        out_shape=[jax.ShapeDtypeStruct((B, HY_WIDTH // LANES, L, LANES), BF16),
                   jax.ShapeDtypeStruct((T, HY_WIDTH), BF16),
                   jax.ShapeDtypeStruct((T, HY_WIDTH), BF16)],
        compiler_params=_cparams("parallel", "parallel"),
        name="hyena_prep",
    )(proj, proj, proj, conv_w, conv_b.reshape(1, hw), filt_bias.reshape(1, HY_WIDTH))


def _fft_dims(L):
    n = 2 * L
    n1 = 1 << ((n.bit_length() - 1 + 1) // 2)
    n2 = n // n1
    A = n1 // 2
    K1 = n1 // 2 + 1
    K1p = -(-K1 // 8) * 8
    KP = -(-2 * K1p // LANES) * LANES
    assert n1 * n2 == n and A * n2 == L and n2 % 8 == 0 and A % 16 == 0
    return n, n1, n2, A, K1, K1p, KP


def _dft_tables(L):
    n, n1, n2, A, K1, K1p, KP = _fft_dims(L)
    b = jnp.arange(n2, dtype=I32)
    a = jnp.arange(A, dtype=I32)
    k1 = jnp.arange(K1p, dtype=I32)
    t = a[None, :] * n2 + b[:, None]
    m = (k1[None, :, None] * t[:, None, :]) % n
    th = m.astype(F32) * (2.0 * math.pi / n)
    valid = (k1 < K1).astype(F32)[None, :, None]
    c, s = jnp.cos(th) * valid, jnp.sin(th) * valid
    g_fwd = jnp.concatenate([c, -s], axis=1).astype(BF16)
    w = jnp.where((k1 == 0) | (k1 == n1 // 2), 1.0, 2.0)[None, :, None] / n
    ci = jnp.transpose(c * w, (0, 2, 1))
    si = jnp.transpose(-s * w, (0, 2, 1))
    pad = jnp.zeros((n2, A, KP - 2 * K1p), F32)
    g_inv = jnp.concatenate([ci, si, pad], axis=2).astype(BF16)
    k2 = jnp.arange(n2, dtype=I32)
    ph = ((k2[:, None] * b[None, :]) % n2).astype(F32) * (2.0 * math.pi / n2)
    c2, s2 = jnp.cos(ph), jnp.sin(ph)
    f2 = jnp.concatenate([jnp.concatenate([c2, s2], 1), jnp.concatenate([-s2, c2], 1)], 0).astype(BF16)
    f2i = jnp.concatenate([jnp.concatenate([c2, -s2], 1), jnp.concatenate([s2, c2], 1)], 0).astype(BF16)
    return g_fwd, g_inv, f2, f2i


def _fft_rows_fwd(x_ref, g_ref, yf_ref, *, nb, n2, K1p):
    def body(b, _):
        off = pl.multiple_of(b * LANES, LANES)
        xb = jnp.concatenate([x_ref[i, :, pl.ds(off, LANES)] for i in range(nb)], axis=1)
        y = jnp.dot(g_ref[b], xb, preferred_element_type=F32)
        for i in range(nb):
            for r in range(2 * K1p // 8):
                row = pl.multiple_of((r * n2 + b) * 8, 8)
                yf_ref[i, pl.ds(row, 8), :] = y[r * 8:(r + 1) * 8, i * LANES:(i + 1) * LANES]
        return 0
    lax.fori_loop(0, n2, body, 0, unroll=2 * FFT_UNROLL)


def _fft_cols_fwd(yf_ref, f2_ref, k1, *, nb, n2, K1p):
    kt, ks = k1 // 8, k1 % 8
    parts = []
    for i in range(nb):
        re = yf_ref[i, pl.ds(kt * (n2 * 8) + ks, n2, stride=8), :]
        im = yf_ref[i, pl.ds((K1p // 8 + kt) * (n2 * 8) + ks, n2, stride=8), :]
        parts.append(jnp.concatenate([re, im], axis=0))
    yk = jnp.concatenate(parts, axis=1).astype(BF16)
    return jnp.dot(f2_ref[...], yk, preferred_element_type=F32)


def _kspec_kernel(fb_ref, l1_ref, g_ref, f2_ref, kf_ref, yf_ref, *, n2, K1p):
    _fft_rows_fwd(fb_ref, g_ref, yf_ref, nb=2, n2=n2, K1p=K1p)
    inv = 1.0 / l1_ref[...]

    def body(k1, _):
        z = _fft_cols_fwd(yf_ref, f2_ref, k1, nb=2, n2=n2, K1p=K1p)
        zf, zb = z[:, :LANES], z[:, LANES:]
        kr = (zf[:n2] + zb[:n2]) * inv
        ki = (zf[n2:] - zb[n2:]) * inv
        kf_ref[k1] = jnp.concatenate([kr, ki], axis=0).astype(kf_ref.dtype)
        return 0
    lax.fori_loop(0, K1p, body, 0, unroll=FFT_UNROLL)


def _lconv_kernel(x_ref, g_ref, f2_ref, f2i_ref, kf_ref, gi_ref, o_ref, yf_ref, wf_ref, *,
                  nb, n2, K1p, KP):
    _fft_rows_fwd(x_ref, g_ref, yf_ref, nb=nb, n2=n2, K1p=K1p)

    def kbody(k1, _):
        z = _fft_cols_fwd(yf_ref, f2_ref, k1, nb=nb, n2=n2, K1p=K1p)
        kf = kf_ref[k1].astype(F32)
        kr = jnp.concatenate([kf[:n2]] * nb, axis=1)
        ki = jnp.concatenate([kf[n2:]] * nb, axis=1)
        zr, zi = z[:n2], z[n2:]
        p = jnp.concatenate([zr * kr - zi * ki, zr * ki + zi * kr], axis=0).astype(BF16)
        v = jnp.dot(f2i_ref[...], p, preferred_element_type=F32)
        for i in range(nb):
            for r in range(2 * n2 // 8):
                row = pl.multiple_of((r * K1p + k1) * 8, 8)
                wf_ref[i, pl.ds(row, 8), :] = v[r * 8:(r + 1) * 8, i * LANES:(i + 1) * LANES]
        return 0
    lax.fori_loop(0, K1p, kbody, 0, unroll=FFT_UNROLL)

    def bbody(b, _):
        bt, bs = b // 8, b % 8
        parts = []
        for i in range(nb):
            re = wf_ref[i, pl.ds(bt * (K1p * 8) + bs, K1p, stride=8), :]
            im = wf_ref[i, pl.ds((n2 // 8 + bt) * (K1p * 8) + bs, K1p, stride=8), :]
            parts.append(jnp.concatenate([re, im, jnp.zeros((KP - 2 * K1p, LANES), F32)], axis=0))
        vb = jnp.concatenate(parts, axis=1).astype(BF16)
        ob = jnp.dot(gi_ref[b], vb, preferred_element_type=F32)
        off = pl.multiple_of(b * LANES, LANES)
        for i in range(nb):
            o_ref[i, :, pl.ds(off, LANES)] = ob[:, i * LANES:(i + 1) * LANES].astype(o_ref.dtype)
        return 0
    lax.fori_loop(0, n2, bbody, 0, unroll=2 * FFT_UNROLL)


def _filter_kernel(w1_ref, b1_ref, w2_ref, b2_ref, w3_ref, fr_ref, fb_ref, l1_ref, *, seq):
    i = pl.program_id(0)
    tl = fb_ref.shape[2]
    bands = (FILTER_EMB - 1) // 2
    pos = (i * tl + lax.broadcasted_iota(I32, (tl, 1), 0)).astype(F32)
    t = pos * (1.0 / (seq - 1))
    wpos = pos * (2.0 * math.pi / seq)
    lane = lax.broadcasted_iota(I32, (1, LANES), 1)
    band = jnp.where(lane <= bands, lane - 1, lane - 1 - bands).astype(F32)
    fvec = 1e-4 + band * ((bands - 1 - 1e-4) / (bands - 1))
    arg = wpos * fvec
    z = jnp.where(lane == 0, t, jnp.where(lane <= bands, jnp.cos(arg),
                                          jnp.where(lane <= 2 * bands, -jnp.sin(arg), 0.0)))
    fr = fr_ref[...]
    h = jnp.sin(fr * (jnp.dot(z, w1_ref[...], preferred_element_type=F32) + b1_ref[...]))
    h = jnp.sin(fr * (jnp.dot(h, w2_ref[...], preferred_element_type=F32) + b2_ref[...]))
    h = jnp.dot(h, w3_ref[...], preferred_element_type=F32)
    max_decay = math.log(DECAY_TARGET) / FAST_DECAY_PCT
    min_decay = math.log(DECAY_TARGET) / SLOW_DECAY_PCT
    ch = lax.broadcasted_iota(I32, (1, HY_WIDTH), 1).astype(F32)
    deltas = jnp.abs(min_decay + ch * ((max_decay - min_decay) / (HY_WIDTH - 1)))
    decay = jnp.exp(-t * deltas)
    fwd = h[:, :HY_WIDTH] * decay
    bwd = jnp.where(pos == 0.0, 0.0, h[:, HY_WIDTH:] * decay)
    for c in range(HY_WIDTH // LANES):
        fb_ref[0, c] = fwd[:, c * LANES:(c + 1) * LANES].astype(fb_ref.dtype)
        fb_ref[1, c] = bwd[:, c * LANES:(c + 1) * LANES].astype(fb_ref.dtype)
    part = jnp.sum(jnp.abs(fwd) + jnp.abs(bwd), axis=0, keepdims=True)

    @pl.when(i == 0)
    def _():
        l1_ref[...] = jnp.zeros_like(l1_ref)
    l1_ref[...] += part


def _filter_spectrum(L, W, tables, tl=512):
    n, n1, n2, A, K1, K1p, KP = _fft_dims(L)
    g_fwd, _, f2, _ = tables
    nc = HY_WIDTH // LANES
    w1 = jnp.pad(W['filt_w1'], ((0, LANES - FILTER_EMB), (0, 0)))
    full = lambda r, c: pl.BlockSpec((r, c), lambda i: (0, 0))
    fb, l1 = pl.pallas_call(
        functools.partial(_filter_kernel, seq=L),
        grid=(L // tl,),
        in_specs=[full(LANES, FILTER_HIDDEN), full(1, FILTER_HIDDEN),
                  full(FILTER_HIDDEN, FILTER_HIDDEN), full(1, FILTER_HIDDEN),
                  full(FILTER_HIDDEN, 2 * HY_WIDTH), full(1, FILTER_HIDDEN)],
        out_specs=[pl.BlockSpec((2, nc, tl, LANES), lambda i: (0, 0, i, 0)),
                   pl.BlockSpec((1, HY_WIDTH), lambda i: (0, 0))],
        out_shape=[jax.ShapeDtypeStruct((2, nc, L, LANES), BF16),
                   jax.ShapeDtypeStruct((1, HY_WIDTH), F32)],
        compiler_params=_cparams("arbitrary"),
        name="hyena_filter",
    )(w1, W['filt_b1'].reshape(1, -1), W['filt_w2'], W['filt_b2'].reshape(1, -1), W['filt_w3'],
      W['filt_freq'].reshape(1, -1))
    fbr = fb.reshape(2, nc, A, n2 * LANES)
    rows = (2 * K1p // 8) * n2 * 8
    return pl.pallas_call(
        functools.partial(_kspec_kernel, n2=n2, K1p=K1p),
        grid=(nc,),
        in_specs=[pl.BlockSpec((2, None, A, n2 * LANES), lambda c: (0, c, 0, 0)),
                  pl.BlockSpec((1, LANES), lambda c: (0, c)),
                  pl.BlockSpec((n2, 2 * K1p, A), lambda c: (0, 0, 0)),
                  pl.BlockSpec((2 * n2, 2 * n2), lambda c: (0, 0))],
        out_specs=pl.BlockSpec((None, K1p, 2 * n2, LANES), lambda c: (c, 0, 0, 0)),
        out_shape=jax.ShapeDtypeStruct((nc, K1p, 2 * n2, LANES), BF16),
        scratch_shapes=[pltpu.VMEM((2, rows, LANES), F32)],
        compiler_params=_cparams("arbitrary"),
        name="filter_spectrum",
    )(fbr, l1, g_fwd, f2)


def _long_conv(zc, kf, tables, nb):
    B, nc, L, _ = zc.shape
    n, n1, n2, A, K1, K1p, KP = _fft_dims(L)
    g_fwd, g_inv, f2, f2i = tables
    x = zc.reshape(B, nc, A, n2 * LANES)
    rows_y = (2 * K1p // 8) * n2 * 8
    rows_w = (2 * n2 // 8) * K1p * 8
    const = lambda shape: pl.BlockSpec(shape, lambda c, j: (0,) * len(shape),
                                       pipeline_mode=pl.Buffered(1))
    y = pl.pallas_call(
        functools.partial(_lconv_kernel, nb=nb, n2=n2, K1p=K1p, KP=KP),
        grid=(nc, B // nb),
        in_specs=[pl.BlockSpec((nb, None, A, n2 * LANES), lambda c, j: (j, c, 0, 0)),
                  const((n2, 2 * K1p, A)), const((2 * n2, 2 * n2)), const((2 * n2, 2 * n2)),
                  pl.BlockSpec((None, K1p, 2 * n2, LANES), lambda c, j: (c, 0, 0, 0)),
                  const((n2, A, KP))],
        out_specs=pl.BlockSpec((nb, None, A, n2 * LANES), lambda c, j: (j, c, 0, 0)),
        out_shape=jax.ShapeDtypeStruct((B, nc, A, n2 * LANES), BF16),
        scratch_shapes=[pltpu.VMEM((nb, rows_y, LANES), F32), pltpu.VMEM((nb, rows_w, LANES), F32)],
        compiler_params=_cparams("arbitrary", "arbitrary"),
        name="long_conv",
    )(x, g_fwd, f2, f2i, kf, g_inv)
    return y.reshape(B, nc, L, LANES)


def _merge_rows(rows, x_ref, a_ref, y_ref, x0_ref, zb_ref, g0_ref, g1_ref, wa_ref, wh_ref, wo_ref):
    a = jnp.dot(a_ref[rows, :], wa_ref[...], preferred_element_type=F32)
    x0 = x0_ref[rows, :].astype(F32)
    y = jnp.concatenate([y_ref[0, c, rows, :].astype(F32) for c in range(HY_WIDTH // LANES)], axis=1)
    hz = (y * x0 + zb_ref[rows, :].astype(F32)).astype(BF16)
    hzp = jnp.dot(hz, wh_ref[...], preferred_element_type=F32)
    g0 = jax.nn.sigmoid(g0_ref[rows, :].astype(F32))
    g1 = jax.nn.sigmoid(g1_ref[rows, :].astype(F32))
    m = (g0 * a + g1 * hzp).astype(BF16)
    return x_ref[rows, :] + jnp.dot(m, wo_ref[...], preferred_element_type=F32)


def _memkv_kernel(m_ref, g_ref, w_ref, o_ref):
    u = _rms(m_ref[...], g_ref[...]).astype(BF16)
    o_ref[...] = jnp.dot(u, w_ref[...], preferred_element_type=F32).astype(BF16)


def _mem_kv(mem_t, g, w_xkv_bf):
    R = mem_t.shape[0]
    return pl.pallas_call(
        _memkv_kernel,
        grid=(R // N_MEM,),
        in_specs=[pl.BlockSpec((N_MEM, D_MODEL), lambda i: (i, 0)),
                  pl.BlockSpec((1, D_MODEL), lambda i: (0, 0)),
                  pl.BlockSpec((D_MODEL, 2 * D_MODEL), lambda i: (0, 0))],
        out_specs=pl.BlockSpec((N_MEM, 2 * D_MODEL), lambda i: (i, 0)),
        out_shape=jax.ShapeDtypeStruct((R, 2 * D_MODEL), BF16),
        compiler_params=_cparams("parallel"),
        name="mem_kv",
    )(mem_t, g.reshape(1, D_MODEL), w_xkv_bf)


def _mix_xattn_kernel(x_ref, a_ref, y_ref, x0_ref, zb_ref, g0_ref, g1_ref, wa_ref, wh_ref, wmo_ref,
                      kv_ref, gx_ref, wq_ref, wo_ref, gf_ref, wr_ref, x2_ref, h_ref, aff_ref):
    step = x_ref.shape[0] // MIX_ROW_GROUPS
    for r in range(MIX_ROW_GROUPS):
        rows = slice(r * step, (r + 1) * step)
        x = _merge_rows(rows, x_ref, a_ref, y_ref, x0_ref, zb_ref, g0_ref, g1_ref, wa_ref, wh_ref, wmo_ref)
        u = _rms(x, gx_ref[...]).astype(BF16)
        q = jnp.dot(u, wq_ref[...], preferred_element_type=F32)
        q = (q * (X_HEAD_DIM ** -0.5)).astype(BF16)
        outs = []
        for h in range(X_HEADS):
            qh = q[:, h * X_HEAD_DIM:(h + 1) * X_HEAD_DIM]
            kh = kv_ref[:, h * X_HEAD_DIM:(h + 1) * X_HEAD_DIM]
            vh = kv_ref[:, D_MODEL + h * X_HEAD_DIM:D_MODEL + (h + 1) * X_HEAD_DIM]
            s = lax.dot_general(qh, kh, (((1,), (1,)), ((), ())), preferred_element_type=F32)
            s = s - jnp.max(s, axis=1, keepdims=True)
            p = jnp.exp(s)
            p = p / jnp.sum(p, axis=1, keepdims=True)
            outs.append(jnp.dot(p.astype(BF16), vh, preferred_element_type=F32))
        o = jnp.concatenate(outs, axis=1).astype(BF16)
        x2 = x + jnp.dot(o, wo_ref[...], preferred_element_type=F32)
        x2_ref[rows, :] = x2
        hf = _rms(x2, gf_ref[...]).astype(BF16)
        h_ref[rows, :] = hf
        lg = lax.dot_general(wr_ref[...], hf, (((1,), (1,)), ((), ())), preferred_element_type=F32)
        lg = lg - jnp.max(lg, axis=0, keepdims=True)
        e = jnp.exp(lg)
        aff_ref[:, rows] = e / jnp.sum(e, axis=0, keepdims=True)


def _merge_cross_attn(xt, a, yc, x0, zb, proj, wa, wh, wmo, kv, gx, wq, wo, gf, wr_t, B, L, tm=512):
    T = B * L
    nl = L // tm
    full = lambda r, c: pl.BlockSpec((r, c), lambda b, i: (0, 0))
    tok = lambda c, cb=0: pl.BlockSpec((tm, c), lambda b, i, cb=cb: (b * nl + i, cb))
    return pl.pallas_call(
        _mix_xattn_kernel,
        grid=(B, nl),
        in_specs=[tok(D_MODEL), tok(ATT_V_W),
                  pl.BlockSpec((1, HY_WIDTH // LANES, tm, LANES), lambda b, i: (b, 0, i, 0)),
                  tok(HY_WIDTH), tok(HY_WIDTH), tok(D_MODEL, 3), tok(D_MODEL, 4),
                  full(ATT_V_W, D_MODEL), full(HY_WIDTH, D_MODEL), full(D_MODEL, D_MODEL),
                  pl.BlockSpec((N_MEM, 2 * D_MODEL), lambda b, i: (b, 0)),
                  full(1, D_MODEL), full(D_MODEL, D_MODEL), full(D_MODEL, D_MODEL),
                  full(1, D_MODEL), full(N_EXPERTS, D_MODEL)],
        out_specs=[tok(D_MODEL), tok(D_MODEL),
                   pl.BlockSpec((N_EXPERTS, tm), lambda b, i: (0, b * nl + i))],
        out_shape=[jax.ShapeDtypeStruct((T, D_MODEL), F32),
                   jax.ShapeDtypeStruct((T, D_MODEL), BF16),
                   jax.ShapeDtypeStruct((N_EXPERTS, T), F32)],
        compiler_params=_cparams("parallel", "parallel"),
        name="merge_cross_attn",
    )(xt, a, yc, x0, zb, proj, proj, wa, wh, wmo, kv, gx.reshape(1, D_MODEL), wq, wo,
      gf.reshape(1, D_MODEL), wr_t)


def _select_kernel(aff_ref, gate_ref, *, cap):
    aff = aff_ref[...]
    bits = pltpu.bitcast(aff, I32)
    T = aff.shape[1]

    def count(mask):
        return jnp.sum(mask.astype(I32), axis=1, keepdims=True)

    def vbody(k, thr):
        cand = thr | (jnp.int32(1) << (30 - k))
        return jnp.where(count(bits >= cand) >= cap, cand, thr)

    thr = lax.fori_loop(0, 31, vbody, jnp.zeros((N_EXPERTS, 1), I32))
    gt = bits > thr
    eq = bits == thr
    need = cap - count(gt)
    tok = lax.broadcasted_iota(I32, aff.shape, 1)
    nbit = int(T).bit_length()

    def jbody(k, j):
        cand = j + (jnp.int32(1) << (nbit - 1 - k))
        return jnp.where(count(eq & (tok < cand)) <= need, cand, j)

    j = lax.fori_loop(0, nbit, jbody, jnp.zeros((N_EXPERTS, 1), I32))
    sel = gt | (eq & (tok < j))
    gate_ref[...] = jnp.where(sel, aff, 0.0)


def _select(aff_t, cap):
    E, T = aff_t.shape
    return pl.pallas_call(
        functools.partial(_select_kernel, cap=cap),
        grid=(1,),
        in_specs=[pl.BlockSpec((E, T), lambda i: (0, 0))],
        out_specs=pl.BlockSpec((E, T), lambda i: (0, 0)),
        out_shape=jax.ShapeDtypeStruct((E, T), F32),
        compiler_params=_cparams("arbitrary"),
        name="ec_select",
    )(aff_t)


EC_TILE = 256
EC_TILE_SEQS = 16


def _ec_tiling(B, L):
    tb = min(B, EC_TILE_SEQS)
    tl = EC_TILE // tb
    assert B % tb == 0 and L % tl == 0 and tl % 16 == 0
    return tb, tl, (48 if tb >= EC_TILE_SEQS else 64)


def _to_tile_order(a, B, L):
    tb, tl, _ = _ec_tiling(B, L)
    E = a.shape[0]
    return a.reshape(E, B // tb, tb, L // tl, tl).transpose(0, 3, 1, 2, 4).reshape(E, B * L)


def _rank_kernel(g_ref, rank_ref, cnt_ref):
    n = g_ref.shape[1]
    sel = jnp.where(g_ref[...] > 0.0, 1.0, 0.0).astype(BF16)
    r = lax.broadcasted_iota(I32, (n, n), 0)
    c = lax.broadcasted_iota(I32, (n, n), 1)
    before = jnp.where(r < c, 1.0, 0.0).astype(BF16)
    rank_ref[...] = jnp.dot(sel, before, preferred_element_type=F32)
    cnt_ref[...] = jnp.dot(sel, jnp.ones((n, LANES), BF16), preferred_element_type=F32)


def _tile_ranks(gate):
    E, T = gate.shape
    rows = E * T // EC_TILE
    rb = min(512, rows)
    rank, cnt = pl.pallas_call(
        _rank_kernel,
        grid=(rows // rb,),
        in_specs=[pl.BlockSpec((rb, EC_TILE), lambda i: (i, 0))],
        out_specs=[pl.BlockSpec((rb, EC_TILE), lambda i: (i, 0)),
                   pl.BlockSpec((rb, LANES), lambda i: (i, 0))],
        out_shape=[jax.ShapeDtypeStruct((rows, EC_TILE), F32),
                   jax.ShapeDtypeStruct((rows, LANES), F32)],
        compiler_params=_cparams("parallel"),
        name="ec_rank",
    )(gate.reshape(rows, EC_TILE))
    return rank.reshape(E, T), cnt[:, 0].reshape(E, T // EC_TILE)


def _dispatch_kernel(h_ref, rank_ref, gate_ref, xe_ref):
    slots = xe_ref.shape[1]
    slot = lax.broadcasted_iota(I32, (slots, EC_TILE), 0).astype(F32)
    parts = []
    for e in range(N_EXPERTS):
        hit = (rank_ref[e:e + 1, :] == slot) & (gate_ref[e:e + 1, :] > 0.0)
        parts.append(jnp.where(hit, 1.0, 0.0).astype(BF16))
    onehot = jnp.concatenate(parts, axis=0)
    h = h_ref[...].reshape(EC_TILE, D_MODEL)
    xe = jnp.dot(onehot, h, preferred_element_type=F32).astype(BF16)
    xe_ref[...] = xe.reshape(N_EXPERTS, slots, D_MODEL)


def _tile_spec(B, L, width):
    tb, tl, _ = _ec_tiling(B, L)
    groups = B // tb
    return pl.BlockSpec((tb, tl, width), lambda j, *_: (j % groups, j // groups, 0))


def _dispatch(h, rank, gate, B, L):
    T = B * L
    nt = T // EC_TILE
    slots = _ec_tiling(B, L)[2]
    return pl.pallas_call(
        _dispatch_kernel,
        grid=(nt,),
        in_specs=[_tile_spec(B, L, D_MODEL),
                  pl.BlockSpec((N_EXPERTS, EC_TILE), lambda i: (0, i)),
                  pl.BlockSpec((N_EXPERTS, EC_TILE), lambda i: (0, i))],
        out_specs=pl.BlockSpec((N_EXPERTS, None, slots, D_MODEL), lambda i: (0, i, 0, 0)),
        out_shape=jax.ShapeDtypeStruct((N_EXPERTS, nt, slots, D_MODEL), BF16),
        compiler_params=_cparams("parallel"),
        name="ec_dispatch",
    )(h.reshape(B, L, D_MODEL), rank, gate)


def _swiglu(h, wg_ref, wu_ref, wd_ref):
    half = EXPERT_FF // 2
    y = None
    for c in range(2):
        sl = slice(c * half, (c + 1) * half)
        a = jnp.dot(h, wg_ref[:, sl], preferred_element_type=F32)
        b = jnp.dot(h, wu_ref[:, sl], preferred_element_type=F32)
        he = (jax.nn.silu(a) * b).astype(BF16)
        d = jnp.dot(he, wd_ref[sl, :], preferred_element_type=F32)
        y = d if y is None else y + d
    return y


def _expert_rows_kernel(x_ref, wg_ref, wu_ref, wd_ref, o_ref):
    o_ref[0] = _swiglu(x_ref[0], wg_ref.at[0], wu_ref.at[0], wd_ref.at[0]).astype(o_ref.dtype)


def _expert_rows(xe, wg, wu, wd):
    E, R, _ = xe.shape
    rt = math.gcd(R, 1024)
    rows = pl.BlockSpec((1, rt, D_MODEL), lambda e, j: (e, j, 0))
    wspec = lambda r, c: pl.BlockSpec((1, r, c), lambda e, j: (e, 0, 0))
    return pl.pallas_call(
        _expert_rows_kernel,
        grid=(E, R // rt),
        in_specs=[rows, wspec(D_MODEL, EXPERT_FF), wspec(D_MODEL, EXPERT_FF), wspec(EXPERT_FF, D_MODEL)],
        out_specs=rows,
        out_shape=jax.ShapeDtypeStruct((E, R, D_MODEL), BF16),
        compiler_params=_cparams("parallel", "parallel"),
        name="expert_rows",
    )(xe, wg, wu, wd)


def _combine_kernel(flag_ref, tflag_ref, ye_ref, rank_ref, gate_ref, ex_ref, sl_ref, x_ref, h_ref,
                    gn_ref, wg_hbm, wu_hbm, wd_hbm, o_ref, wbuf, sem):
    i = pl.program_id(0)
    slots = ye_ref.shape[1]
    blk = o_ref.shape
    ye = ye_ref[...].reshape(N_EXPERTS * slots, D_MODEL)
    rank = rank_ref[...]
    gate = gate_ref[...]
    r = jnp.dot(rank.astype(BF16), ex_ref[...], preferred_element_type=F32)
    g = jnp.dot(gate.astype(BF16), ex_ref[...], preferred_element_type=F32)
    q = jnp.where(r == sl_ref[...], g, 0.0).astype(BF16)
    o_ref[...] = x_ref[...] + jnp.dot(q, ye, preferred_element_type=F32).reshape(blk)

    @pl.when(tflag_ref[i] > 0)
    def _():
        def expert(e, carry):
            @pl.when(flag_ref[i * N_EXPERTS + e] > 0)
            def _():
                copies = [pltpu.make_async_copy(w.at[e], wbuf.at[k], sem.at[k])
                          for k, w in enumerate((wg_hbm, wu_hbm, wd_hbm))]
                for cp in copies:
                    cp.start()
                for cp in copies:
                    cp.wait()
                lane = lax.broadcasted_iota(I32, rank.shape, 1)
                keep = (lane == e) & (rank >= float(slots))
                ge = jnp.sum(jnp.where(keep, gate, 0.0), axis=1, keepdims=True)
                h = h_ref[...].reshape(EC_TILE, D_MODEL)
                o_ref[...] += (_swiglu(h, wbuf.at[0], wbuf.at[1], wbuf.at[2]) * ge).reshape(blk)
            return carry
        lax.fori_loop(0, N_EXPERTS, expert, 0)

    o_ref[...] = _rms(o_ref[...], gn_ref[...])


def _combine(ye, rank_t, gate_t, x2, h, flags, tflags, wg, wu, wd, gn, B, L):
    T = B * L
    nt = T // EC_TILE
    slots = ye.shape[2]
    es = N_EXPERTS * slots
    lane = jnp.arange(es, dtype=I32)
    expand = (lane[None, :] // slots == jnp.arange(N_EXPERTS, dtype=I32)[:, None]).astype(BF16)
    slot = (lane % slots).astype(F32).reshape(1, es)
    tok = lambda c: pl.BlockSpec((EC_TILE, c), lambda i, f, t: (i, 0))
    full = lambda r, c: pl.BlockSpec((r, c), lambda i, f, t: (0, 0))
    tile = _tile_spec(B, L, D_MODEL)
    hbm = pl.BlockSpec(memory_space=pl.ANY)
    return pl.pallas_call(
        _combine_kernel,
        grid_spec=pltpu.PrefetchScalarGridSpec(
            num_scalar_prefetch=2,
            grid=(nt,),
            in_specs=[pl.BlockSpec((N_EXPERTS, None, slots, D_MODEL), lambda i, f, t: (0, i, 0, 0)),
                      tok(N_EXPERTS), tok(N_EXPERTS), full(N_EXPERTS, es), full(1, es),
                      tile, tile, full(1, D_MODEL), hbm, hbm, hbm],
            out_specs=tile,
            scratch_shapes=[pltpu.VMEM((3, D_MODEL, EXPERT_FF), BF16), pltpu.SemaphoreType.DMA((3,))]),
        out_shape=jax.ShapeDtypeStruct((B, L, D_MODEL), F32),
        compiler_params=_cparams("arbitrary"),
        name="ec_combine",
    )(flags, tflags, ye, rank_t, gate_t, expand, slot, x2.reshape(B, L, D_MODEL),
      h.reshape(B, L, D_MODEL), gn.reshape(1, D_MODEL), wg, wu, wd)


def _expert_mixture(h, gate, x2, wg, wu, wd, gn, B, L):
    nt = B * L // EC_TILE
    slots = _ec_tiling(B, L)[2]
    gate = _to_tile_order(gate, B, L)
    rank, cnt = _tile_ranks(gate)
    xe = _dispatch(h, rank, gate, B, L)
    ye = _expert_rows(xe.reshape(N_EXPERTS, nt * slots, D_MODEL), wg, wu, wd)
    over = cnt.T > slots
    flags = over.reshape(-1).astype(I32)
    tflags = over.any(axis=1).astype(I32)
    return _combine(ye.reshape(N_EXPERTS, nt, slots, D_MODEL), rank.T, gate.T, x2, h, flags, tflags,
                    wg, wu, wd, gn, B, L)


def _run_group(x, mem, W):
    B, L, _ = x.shape
    T = B * L
    xt = x.reshape(T, D_MODEL)
    proj = _inproj(xt, W['norm_mix'], W['w_in'], L)
    a = _diff_attention(proj, W['lam'], W['subln'], B, L)
    zc, x0, zb = _hyena_prep(proj, W['hy_conv_w'], W['hy_conv_b'], W['filt_bias'], B, L)
    tables = _dft_tables(L)
    kf = _filter_spectrum(L, W, tables)
    yc = _long_conv(zc, kf, tables, nb=1 if L >= 4096 else 2)
    kv = _mem_kv(mem.reshape(B * N_MEM, D_MODEL), W['norm_mem'], W['w_xkv'])
    x2, hf, aff = _merge_cross_attn(xt, a, yc, x0, zb, proj, W['w_br_attn'], W['w_br_hyena'],
                                    W['w_out'], kv, W['norm_x'], W['w_xq'], W['w_xo'], W['norm_ffn'],
                                    W['w_router_t'], B, L)
    gate = _select(aff, EC_CAPACITY_FACTOR * T // N_EXPERTS)
    return _expert_mixture(hf, gate, x2, W['w_exp_gate'], W['w_exp_up'], W['w_exp_down'],
                           W['norm_final'], B, L)


def kernel(x_prompt, x_sample, mem_prompt, mem_sample, norm_mix, w_in, lambda_q1, lambda_k1, lambda_q2, lambda_k2, subln, w_br_attn, hy_conv_w, hy_conv_b, filt_w1, filt_b1, filt_w2, filt_b2, filt_w3, filt_freq, filt_bias, w_br_hyena, w_out, norm_x, norm_mem, w_xq, w_xkv, w_xo, norm_ffn, w_router, w_exp_gate, w_exp_up, w_exp_down, norm_final):
    l = 0
    lam = (jnp.exp(jnp.sum(lambda_q1[l] * lambda_k1[l])) - jnp.exp(jnp.sum(lambda_q2[l] * lambda_k2[l]))
           + LAM_INIT).astype(F32)
    W = dict(
        norm_mix=norm_mix[l], w_in=w_in[l].astype(BF16), lam=lam, subln=subln[l],
        w_br_attn=w_br_attn[l].astype(BF16), hy_conv_w=hy_conv_w[l], hy_conv_b=hy_conv_b[l],
        filt_w1=filt_w1[l], filt_b1=filt_b1[l], filt_w2=filt_w2[l], filt_b2=filt_b2[l],
        filt_w3=filt_w3[l], filt_freq=filt_freq[l], filt_bias=filt_bias[l],
        w_br_hyena=w_br_hyena[l].astype(BF16), w_out=w_out[l].astype(BF16),
        norm_x=norm_x[l], norm_mem=norm_mem[l], w_xq=w_xq[l].astype(BF16),
        w_xkv=w_xkv[l].astype(BF16), w_xo=w_xo[l].astype(BF16), norm_ffn=norm_ffn[l],
        w_router_t=w_router[l].T.astype(BF16), w_exp_gate=w_exp_gate[l].astype(BF16),
        w_exp_up=w_exp_up[l].astype(BF16), w_exp_down=w_exp_down[l].astype(BF16),
        norm_final=norm_final)
    return (_run_group(x_prompt, mem_prompt, W), _run_group(x_sample, mem_sample, W))
```

```python
import functools
import math

import jax
import jax.numpy as jnp
from jax import lax
from jax.experimental import pallas as pl
from jax.experimental.pallas import tpu as pltpu

F32 = jnp.float32
BF16 = jnp.bfloat16
I32 = jnp.int32

D_MODEL = 1024
N_ATT_HEADS = 4
ATT_HEAD_DIM = 64
ATT_V_DIM = 128
ROT_DIM = 16
ROPE_THETA = 500000.0
ATT_QK_W = 512
ATT_V_W = 512
HY_WIDTH = 512
FILTER_EMB = 33
FILTER_HIDDEN = 64
DECAY_TARGET = 1e-2
FAST_DECAY_PCT = 0.3
SLOW_DECAY_PCT = 1.5
IN_W = 5120
N_MEM = 256
X_HEADS = 4
X_HEAD_DIM = 256
N_EXPERTS = 16
EC_CAPACITY_FACTOR = 2
EXPERT_FF = 1024
EPS = 1e-6
LAM_INIT = 0.8 - 0.6 * math.exp(-0.3 * 0)

LANES = 128
ATT_ROW_GROUPS = 16
ATT_Q_TILE = 2048
MIX_ROW_GROUPS = 1
FFT_UNROLL = 8
VMEM_LIMIT_BYTES = 56 * 1024 * 1024


def _cparams(*sem):
    return pltpu.CompilerParams(dimension_semantics=sem, vmem_limit_bytes=VMEM_LIMIT_BYTES)


def _rms(x, g):
    return x * lax.rsqrt(jnp.mean(x * x, axis=-1, keepdims=True) + EPS) * g


def _inproj_kernel(x_ref, g_ref, w_ref, cos_ref, sa_ref, sb_ref, o_ref):
    u = _rms(x_ref[...], g_ref[...]).astype(BF16)
    n_col = IN_W // 512
    for j in range(n_col):
        acc = jnp.dot(u, w_ref[:, j * 512:(j + 1) * 512], preferred_element_type=F32)
        if j < 2:
            parts = []
            for h in range(4):
                t = acc[:, h * LANES:(h + 1) * LANES]
                r = (t * cos_ref[...] + pltpu.roll(t, 8, 1) * sa_ref[...]
                     + pltpu.roll(t, LANES - 8, 1) * sb_ref[...])
                parts.append(r)
            acc = jnp.concatenate(parts, axis=1)
        if j == 0:
            acc = acc * (ATT_HEAD_DIM ** -0.5 * math.log2(math.e))
        o_ref[:, j * 512:(j + 1) * 512] = acc.astype(BF16)


def _rope_tables(L):
    inv = ROPE_THETA ** (-jnp.arange(0, ROT_DIM, 2, dtype=F32) / ROT_DIM)
    pos = jnp.arange(L, dtype=F32)
    ang = pos[:, None] * inv[None, :]
    cos, sin = jnp.cos(ang), jnp.sin(ang)
    one = jnp.ones((L, 48), F32)
    zero8 = jnp.zeros((L, 8), F32)
    zero48 = jnp.zeros((L, 48), F32)
    c64 = jnp.concatenate([cos, cos, one], axis=1)
    sa64 = jnp.concatenate([zero8, sin, zero48], axis=1)
    sb64 = jnp.concatenate([-sin, zero8, zero48], axis=1)
    tile2 = lambda a: jnp.concatenate([a, a], axis=1)
    return tile2(c64), tile2(sa64), tile2(sb64)


def _inproj(xt, g, w_in_bf, L, tm=512):
    T = xt.shape[0]
    cos, sa, sb = _rope_tables(L)
    nl = L // tm
    tab = pl.BlockSpec((tm, LANES), lambda i: (i % nl, 0))
    return pl.pallas_call(
        _inproj_kernel,
        grid=(T // tm,),
        in_specs=[pl.BlockSpec((tm, D_MODEL), lambda i: (i, 0)),
                  pl.BlockSpec((1, D_MODEL), lambda i: (0, 0)),
                  pl.BlockSpec((D_MODEL, IN_W), lambda i: (0, 0)),
                  tab, tab, tab],
        out_specs=pl.BlockSpec((tm, IN_W), lambda i: (i, 0)),
        out_shape=jax.ShapeDtypeStruct((T, IN_W), BF16),
        compiler_params=_cparams("parallel"),
        name="inproj",
    )(xt, g.reshape(1, D_MODEL), w_in_bf, cos, sa, sb)


def _attn_kernel(lam_ref, q_ref, k_ref, v_ref, sg_ref, o_ref, vx_ref, *, seq, kc):
    tq = q_ref.shape[0]

    @pl.when(pl.program_id(2) == 0)
    def _():
        lane = lax.broadcasted_iota(I32, (seq, LANES), 1)
        vx_ref[:, :LANES] = v_ref[...]
        vx_ref[:, LANES:] = jnp.where(lane == 0, 1.0, 0.0).astype(BF16)

    q = q_ref[...]
    lane = lax.broadcasted_iota(I32, q.shape, 1)
    zero = jnp.zeros_like(q)
    qq = jnp.concatenate([jnp.where(lane < ATT_HEAD_DIM, q, zero),
                          jnp.where(lane >= ATT_HEAD_DIM, q, zero)], axis=0)

    rows = 2 * tq // ATT_ROW_GROUPS
    groups = [qq[g * rows:(g + 1) * rows] for g in range(ATT_ROW_GROUPS)]

    def scores(qg, c):
        start = pl.multiple_of(c * kc, kc)
        s = lax.dot_general(qg, k_ref[pl.ds(start, kc), :], (((1,), (1,)), ((), ())),
                            preferred_element_type=F32)
        return s, vx_ref[pl.ds(start, kc), :]

    if seq == kc:
        accs = []
        for qg in groups:
            s, vx = scores(qg, 0)
            p = jnp.exp2(s - jnp.max(s, axis=1, keepdims=True))
            accs.append(jnp.dot(p.astype(BF16), vx, preferred_element_type=F32))
    else:
        def body(c, carry):
            out = []
            for qg, (m, acc) in zip(groups, carry):
                s, vx = scores(qg, c)
                m_new = jnp.maximum(m, jnp.max(s, axis=1, keepdims=True))
                p = jnp.exp2(s - m_new)
                pv = jnp.dot(p.astype(BF16), vx, preferred_element_type=F32)
                out.append((m_new, jnp.exp2(m - m_new) * acc + pv))
            return tuple(out)

        init = tuple((jnp.full((rows, 1), -jnp.inf, F32), jnp.zeros((rows, 2 * LANES), F32))
                     for _ in groups)
        accs = [a for _, a in lax.fori_loop(0, seq // kc, body, init)]
    acc = jnp.concatenate(accs, axis=0)
    o = acc[:, :ATT_V_DIM] / acc[:, ATT_V_DIM:ATT_V_DIM + 1]
    a = o[:tq] - lam_ref[0] * o[tq:]
    a = _rms(a, sg_ref[...]) * (1.0 - LAM_INIT)
    o_ref[...] = a.astype(BF16)


def _diff_attention(proj, lam, subln, B, L):
    T = B * L
    tq = min(ATT_Q_TILE, L)
    nq = L // tq
    kc = min(2048, L)
    return pl.pallas_call(
        functools.partial(_attn_kernel, seq=L, kc=kc),
        grid=(B, N_ATT_HEADS, nq),
        in_specs=[pl.BlockSpec(memory_space=pltpu.SMEM),
                  pl.BlockSpec((tq, LANES), lambda b, h, i: (b * nq + i, h)),
                  pl.BlockSpec((L, LANES), lambda b, h, i: (b, 4 + h)),
                  pl.BlockSpec((L, LANES), lambda b, h, i: (b, 8 + h)),
                  pl.BlockSpec((1, ATT_V_DIM), lambda b, h, i: (0, 0))],
        out_specs=pl.BlockSpec((tq, ATT_V_DIM), lambda b, h, i: (b * nq + i, h)),
        out_shape=jax.ShapeDtypeStruct((T, ATT_V_W), BF16),
        scratch_shapes=[pltpu.VMEM((L, 2 * LANES), BF16)],
        compiler_params=_cparams("parallel", "parallel", "arbitrary"),
        name="diff_attn",
    )(lam.reshape(1), proj, proj, proj, subln.reshape(1, ATT_V_DIM))


def _hyprep_kernel(hy_ref, prev_ref, next_ref, w_ref, b_ref, zc_ref, x0_ref):
    i = pl.program_id(1)
    n = pl.num_programs(1)
    hy = hy_ref[...].astype(F32)
    tl = hy.shape[0]
    prev_row = jnp.where(i == 0, 0.0, prev_ref[15:16, :].astype(F32))
    next_row = jnp.where(i == n - 1, 0.0, next_ref[0:1, :].astype(F32))
    row = lax.broadcasted_iota(I32, hy.shape, 0)
    up = jnp.where(row == 0, prev_row, pltpu.roll(hy, 1, 0))
    dn = jnp.where(row == tl - 1, next_row, pltpu.roll(hy, tl - 1, 0))
    conv = up * w_ref[0:1, :] + hy * w_ref[1:2, :] + dn * w_ref[2:3, :] + b_ref[...]
    x0 = conv[:, :HY_WIDTH]
    x1 = conv[:, HY_WIDTH:2 * HY_WIDTH]
    hv = conv[:, 2 * HY_WIDTH:]
    z = hv * x1
    for c in range(HY_WIDTH // LANES):
        zc_ref[0, c] = z[:, c * LANES:(c + 1) * LANES].astype(zc_ref.dtype)
    x0_ref[...] = x0.astype(x0_ref.dtype)


def _hyena_prep(proj, conv_w, conv_b, B, L, tl=512):
    T = B * L
    nl = L // tl
    hw = 3 * HY_WIDTH
    rb = tl // 16
    nrb = T // 16
    return pl.pallas_call(
        _hyprep_kernel,
        grid=(B, nl),
        in_specs=[pl.BlockSpec((tl, hw), lambda b, i: (b * nl + i, 1)),
                  pl.BlockSpec((16, hw), lambda b, i: (jnp.maximum((b * nl + i) * rb - 1, 0), 1)),
                  pl.BlockSpec((16, hw), lambda b, i: (jnp.minimum((b * nl + i + 1) * rb, nrb - 1), 1)),
                  pl.BlockSpec((3, hw), lambda b, i: (0, 0)),
                  pl.BlockSpec((1, hw), lambda b, i: (0, 0))],
        out_specs=[pl.BlockSpec((1, HY_WIDTH // LANES, tl, LANES), lambda b, i: (b, 0, i, 0)),
                   pl.BlockSpec((tl, HY_WIDTH), lambda b, i: (b * nl + i, 0))],
        out_shape=[jax.ShapeDtypeStruct((B, HY_WIDTH // LANES, L, LANES), BF16),
                   jax.ShapeDtypeStruct((T, HY_WIDTH), BF16)],
        compiler_params=_cparams("parallel", "parallel"),
        name="hyena_prep",
    )(proj, proj, proj, conv_w, conv_b.reshape(1, hw))


def _fft_dims(L):
    n = 2 * L
    n1 = 1 << ((n.bit_length() - 1 + 1) // 2)
    n2 = n // n1
    A = n1 // 2
    K1 = n1 // 2 + 1
    K1p = -(-K1 // 8) * 8
    KP = -(-2 * K1p // LANES) * LANES
    assert n1 * n2 == n and A * n2 == L and n2 % 8 == 0 and A % 16 == 0
    return n, n1, n2, A, K1, K1p, KP


def _dft_tables(L):
    n, n1, n2, A, K1, K1p, KP = _fft_dims(L)
    b = jnp.arange(n2, dtype=I32)
    a = jnp.arange(A, dtype=I32)
    k1 = jnp.arange(K1p, dtype=I32)
    t = a[None, :] * n2 + b[:, None]
    m = (k1[None, :, None] * t[:, None, :]) % n
    th = m.astype(F32) * (2.0 * math.pi / n)
    valid = (k1 < K1).astype(F32)[None, :, None]
    c, s = jnp.cos(th) * valid, jnp.sin(th) * valid
    g_fwd = jnp.concatenate([c, -s], axis=1).astype(BF16)
    w = jnp.where((k1 == 0) | (k1 == n1 // 2), 1.0, 2.0)[None, :, None] / n
    ci = jnp.transpose(c * w, (0, 2, 1))
    si = jnp.transpose(-s * w, (0, 2, 1))
    pad = jnp.zeros((n2, A, KP - 2 * K1p), F32)
    g_inv = jnp.concatenate([ci, si, pad], axis=2).astype(BF16)
    k2 = jnp.arange(n2, dtype=I32)
    ph = ((k2[:, None] * b[None, :]) % n2).astype(F32) * (2.0 * math.pi / n2)
    c2, s2 = jnp.cos(ph), jnp.sin(ph)
    f2 = jnp.concatenate([jnp.concatenate([c2, s2], 1), jnp.concatenate([-s2, c2], 1)], 0).astype(BF16)
    f2i = jnp.concatenate([jnp.concatenate([c2, -s2], 1), jnp.concatenate([s2, c2], 1)], 0).astype(BF16)
    return g_fwd, g_inv, f2, f2i


def _fft_rows_fwd(x_ref, g_ref, yf_ref, *, nb, n2, K1p):
    def body(b, _):
        off = pl.multiple_of(b * LANES, LANES)
        xb = jnp.concatenate([x_ref[i, :, pl.ds(off, LANES)] for i in range(nb)], axis=1)
        y = jnp.dot(g_ref[b], xb, preferred_element_type=F32)
        for i in range(nb):
            for r in range(2 * K1p // 8):
                row = pl.multiple_of((r * n2 + b) * 8, 8)
                yf_ref[i, pl.ds(row, 8), :] = y[r * 8:(r + 1) * 8, i * LANES:(i + 1) * LANES]
        return 0
    lax.fori_loop(0, n2, body, 0, unroll=2 * FFT_UNROLL)


def _fft_cols_fwd(yf_ref, f2_ref, k1, *, nb, n2, K1p):
    kt, ks = k1 // 8, k1 % 8
    parts = []
    for i in range(nb):
        re = yf_ref[i, pl.ds(kt * (n2 * 8) + ks, n2, stride=8), :]
        im = yf_ref[i, pl.ds((K1p // 8 + kt) * (n2 * 8) + ks, n2, stride=8), :]
        parts.append(jnp.concatenate([re, im], axis=0))
    yk = jnp.concatenate(parts, axis=1).astype(BF16)
    return jnp.dot(f2_ref[...], yk, preferred_element_type=F32)


def _kspec_kernel(fb_ref, l1_ref, g_ref, f2_ref, kf_ref, yf_ref, *, n2, K1p):
    _fft_rows_fwd(fb_ref, g_ref, yf_ref, nb=2, n2=n2, K1p=K1p)
    inv = 1.0 / l1_ref[...]

    def body(k1, _):
        z = _fft_cols_fwd(yf_ref, f2_ref, k1, nb=2, n2=n2, K1p=K1p)
        zf, zb = z[:, :LANES], z[:, LANES:]
        kr = (zf[:n2] + zb[:n2]) * inv
        ki = (zf[n2:] - zb[n2:]) * inv
        kf_ref[k1] = jnp.concatenate([kr, ki], axis=0).astype(kf_ref.dtype)
        return 0
    lax.fori_loop(0, K1p, body, 0, unroll=FFT_UNROLL)


def _lconv_kernel(x_ref, g_ref, f2_ref, f2i_ref, kf_ref, gi_ref, o_ref, yf_ref, wf_ref, *,
                  nb, n2, K1p, KP):
    _fft_rows_fwd(x_ref, g_ref, yf_ref, nb=nb, n2=n2, K1p=K1p)

    def kbody(k1, _):
        z = _fft_cols_fwd(yf_ref, f2_ref, k1, nb=nb, n2=n2, K1p=K1p)
        kf = kf_ref[k1].astype(F32)
        kr = jnp.concatenate([kf[:n2]] * nb, axis=1)
        ki = jnp.concatenate([kf[n2:]] * nb, axis=1)
        zr, zi = z[:n2], z[n2:]
        p = jnp.concatenate([zr * kr - zi * ki, zr * ki + zi * kr], axis=0).astype(BF16)
        v = jnp.dot(f2i_ref[...], p, preferred_element_type=F32)
        for i in range(nb):
            for r in range(2 * n2 // 8):
                row = pl.multiple_of((r * K1p + k1) * 8, 8)
                wf_ref[i, pl.ds(row, 8), :] = v[r * 8:(r + 1) * 8, i * LANES:(i + 1) * LANES]
        return 0
    lax.fori_loop(0, K1p, kbody, 0, unroll=FFT_UNROLL)

    def bbody(b, _):
        bt, bs = b // 8, b % 8
        parts = []
        for i in range(nb):
            re = wf_ref[i, pl.ds(bt * (K1p * 8) + bs, K1p, stride=8), :]
            im = wf_ref[i, pl.ds((n2 // 8 + bt) * (K1p * 8) + bs, K1p, stride=8), :]
            parts.append(jnp.concatenate([re, im, jnp.zeros((KP - 2 * K1p, LANES), F32)], axis=0))
        vb = jnp.concatenate(parts, axis=1).astype(BF16)
        ob = jnp.dot(gi_ref[b], vb, preferred_element_type=F32)
        off = pl.multiple_of(b * LANES, LANES)
        for i in range(nb):
            o_ref[i, :, pl.ds(off, LANES)] = ob[:, i * LANES:(i + 1) * LANES].astype(o_ref.dtype)
        return 0
    lax.fori_loop(0, n2, bbody, 0, unroll=2 * FFT_UNROLL)


def _filter_kernel(w1_ref, b1_ref, w2_ref, b2_ref, w3_ref, fr_ref, fb_ref, l1_ref, *, seq):
    i = pl.program_id(0)
    tl = fb_ref.shape[2]
    bands = (FILTER_EMB - 1) // 2
    pos = (i * tl + lax.broadcasted_iota(I32, (tl, 1), 0)).astype(F32)
    t = pos * (1.0 / (seq - 1))
    wpos = pos * (2.0 * math.pi / seq)
    lane = lax.broadcasted_iota(I32, (1, LANES), 1)
    band = jnp.where(lane <= bands, lane - 1, lane - 1 - bands).astype(F32)
    fvec = 1e-4 + band * ((bands - 1 - 1e-4) / (bands - 1))
    arg = wpos * fvec
    z = jnp.where(lane == 0, t, jnp.where(lane <= bands, jnp.cos(arg),
                                          jnp.where(lane <= 2 * bands, -jnp.sin(arg), 0.0)))
    fr = fr_ref[...]
    h = jnp.sin(fr * (jnp.dot(z, w1_ref[...], preferred_element_type=F32) + b1_ref[...]))
    h = jnp.sin(fr * (jnp.dot(h, w2_ref[...], preferred_element_type=F32) + b2_ref[...]))
    h = jnp.dot(h, w3_ref[...], preferred_element_type=F32)
    max_decay = math.log(DECAY_TARGET) / FAST_DECAY_PCT
    min_decay = math.log(DECAY_TARGET) / SLOW_DECAY_PCT
    ch = lax.broadcasted_iota(I32, (1, HY_WIDTH), 1).astype(F32)
    deltas = jnp.abs(min_decay + ch * ((max_decay - min_decay) / (HY_WIDTH - 1)))
    decay = jnp.exp(-t * deltas)
    fwd = h[:, :HY_WIDTH] * decay
    bwd = jnp.where(pos == 0.0, 0.0, h[:, HY_WIDTH:] * decay)
    for c in range(HY_WIDTH // LANES):
        fb_ref[0, c] = fwd[:, c * LANES:(c + 1) * LANES].astype(fb_ref.dtype)
        fb_ref[1, c] = bwd[:, c * LANES:(c + 1) * LANES].astype(fb_ref.dtype)
    part = jnp.sum(jnp.abs(fwd) + jnp.abs(bwd), axis=0, keepdims=True)

    @pl.when(i == 0)
    def _():
        l1_ref[...] = jnp.zeros_like(l1_ref)
    l1_ref[...] += part


def _filter_spectrum(L, W, tables, tl=512):
    n, n1, n2, A, K1, K1p, KP = _fft_dims(L)
    g_fwd, _, f2, _ = tables
    nc = HY_WIDTH // LANES
    w1 = jnp.pad(W['filt_w1'], ((0, LANES - FILTER_EMB), (0, 0)))
    full = lambda r, c: pl.BlockSpec((r, c), lambda i: (0, 0))
    fb, l1 = pl.pallas_call(
        functools.partial(_filter_kernel, seq=L),
        grid=(L // tl,),
        in_specs=[full(LANES, FILTER_HIDDEN), full(1, FILTER_HIDDEN),
                  full(FILTER_HIDDEN, FILTER_HIDDEN), full(1, FILTER_HIDDEN),
                  full(FILTER_HIDDEN, 2 * HY_WIDTH), full(1, FILTER_HIDDEN)],
        out_specs=[pl.BlockSpec((2, nc, tl, LANES), lambda i: (0, 0, i, 0)),
                   pl.BlockSpec((1, HY_WIDTH), lambda i: (0, 0))],
        out_shape=[jax.ShapeDtypeStruct((2, nc, L, LANES), BF16),
                   jax.ShapeDtypeStruct((1, HY_WIDTH), F32)],
        compiler_params=_cparams("arbitrary"),
        name="hyena_filter",
    )(w1, W['filt_b1'].reshape(1, -1), W['filt_w2'], W['filt_b2'].reshape(1, -1), W['filt_w3'],
      W['filt_freq'].reshape(1, -1))
    fbr = fb.reshape(2, nc, A, n2 * LANES)
    rows = (2 * K1p // 8) * n2 * 8
    return pl.pallas_call(
        functools.partial(_kspec_kernel, n2=n2, K1p=K1p),
        grid=(nc,),
        in_specs=[pl.BlockSpec((2, None, A, n2 * LANES), lambda c: (0, c, 0, 0)),
                  pl.BlockSpec((1, LANES), lambda c: (0, c)),
                  pl.BlockSpec((n2, 2 * K1p, A), lambda c: (0, 0, 0)),
                  pl.BlockSpec((2 * n2, 2 * n2), lambda c: (0, 0))],
        out_specs=pl.BlockSpec((None, K1p, 2 * n2, LANES), lambda c: (c, 0, 0, 0)),
        out_shape=jax.ShapeDtypeStruct((nc, K1p, 2 * n2, LANES), BF16),
        scratch_shapes=[pltpu.VMEM((2, rows, LANES), F32)],
        compiler_params=_cparams("arbitrary"),
        name="filter_spectrum",
    )(fbr, l1, g_fwd, f2)


def _long_conv(zc, kf, tables, nb):
    B, nc, L, _ = zc.shape
    n, n1, n2, A, K1, K1p, KP = _fft_dims(L)
    g_fwd, g_inv, f2, f2i = tables
    x = zc.reshape(B, nc, A, n2 * LANES)
    rows_y = (2 * K1p // 8) * n2 * 8
    rows_w = (2 * n2 // 8) * K1p * 8
    const = lambda shape: pl.BlockSpec(shape, lambda c, j: (0,) * len(shape),
                                       pipeline_mode=pl.Buffered(1))
    y = pl.pallas_call(
        functools.partial(_lconv_kernel, nb=nb, n2=n2, K1p=K1p, KP=KP),
        grid=(nc, B // nb),
        in_specs=[pl.BlockSpec((nb, None, A, n2 * LANES), lambda c, j: (j, c, 0, 0)),
                  const((n2, 2 * K1p, A)), const((2 * n2, 2 * n2)), const((2 * n2, 2 * n2)),
                  pl.BlockSpec((None, K1p, 2 * n2, LANES), lambda c, j: (c, 0, 0, 0)),
                  const((n2, A, KP))],
        out_specs=pl.BlockSpec((nb, None, A, n2 * LANES), lambda c, j: (j, c, 0, 0)),
        out_shape=jax.ShapeDtypeStruct((B, nc, A, n2 * LANES), BF16),
        scratch_shapes=[pltpu.VMEM((nb, rows_y, LANES), F32), pltpu.VMEM((nb, rows_w, LANES), F32)],
        compiler_params=_cparams("arbitrary", "arbitrary"),
        name="long_conv",
    )(x, g_fwd, f2, f2i, kf, g_inv)
    return y.reshape(B, nc, L, LANES)


def _merge_rows(rows, x_ref, a_ref, y_ref, x0_ref, z_ref, fb_ref, g0_ref, g1_ref, wa_ref, wh_ref, wo_ref):
    a = jnp.dot(a_ref[rows, :], wa_ref[...], preferred_element_type=F32)
    x0 = x0_ref[rows, :].astype(F32)
    chunks = range(HY_WIDTH // LANES)
    y = jnp.concatenate([y_ref[0, c, rows, :].astype(F32) for c in chunks], axis=1)
    z = jnp.concatenate([z_ref[0, c, rows, :].astype(F32) for c in chunks], axis=1)
    hz = ((y + z * fb_ref[...]) * x0).astype(BF16)
    hzp = jnp.dot(hz, wh_ref[...], preferred_element_type=F32)
    g0 = jax.nn.sigmoid(g0_ref[rows, :].astype(F32))
    g1 = jax.nn.sigmoid(g1_ref[rows, :].astype(F32))
    m = (g0 * a + g1 * hzp).astype(BF16)
    return x_ref[rows, :] + jnp.dot(m, wo_ref[...], preferred_element_type=F32)


def _memkv_kernel(m_ref, g_ref, w_ref, o_ref):
    u = _rms(m_ref[...], g_ref[...]).astype(BF16)
    o_ref[...] = jnp.dot(u, w_ref[...], preferred_element_type=F32).astype(BF16)


def _mem_kv(mem_t, g, w_xkv_bf):
    R = mem_t.shape[0]
    return pl.pallas_call(
        _memkv_kernel,
        grid=(R // N_MEM,),
        in_specs=[pl.BlockSpec((N_MEM, D_MODEL), lambda i: (i, 0)),
                  pl.BlockSpec((1, D_MODEL), lambda i: (0, 0)),
                  pl.BlockSpec((D_MODEL, 2 * D_MODEL), lambda i: (0, 0))],
        out_specs=pl.BlockSpec((N_MEM, 2 * D_MODEL), lambda i: (i, 0)),
        out_shape=jax.ShapeDtypeStruct((R, 2 * D_MODEL), BF16),
        compiler_params=_cparams("parallel"),
        name="mem_kv",
    )(mem_t, g.reshape(1, D_MODEL), w_xkv_bf)


def _mix_xattn_kernel(x_ref, a_ref, y_ref, x0_ref, z_ref, fb_ref, g0_ref, g1_ref, wa_ref, wh_ref, wmo_ref,
                      kv_ref, gx_ref, wq_ref, wo_ref, gf_ref, wr_ref, x2_ref, h_ref, aff_ref):
    step = x_ref.shape[0] // MIX_ROW_GROUPS
    for r in range(MIX_ROW_GROUPS):
        rows = slice(r * step, (r + 1) * step)
        x = _merge_rows(rows, x_ref, a_ref, y_ref, x0_ref, z_ref, fb_ref, g0_ref, g1_ref, wa_ref, wh_ref,
                        wmo_ref)
        u = _rms(x, gx_ref[...]).astype(BF16)
        q = jnp.dot(u, wq_ref[...], preferred_element_type=F32)
        q = (q * (X_HEAD_DIM ** -0.5)).astype(BF16)
        outs = []
        for h in range(X_HEADS):
            qh = q[:, h * X_HEAD_DIM:(h + 1) * X_HEAD_DIM]
            kh = kv_ref[:, h * X_HEAD_DIM:(h + 1) * X_HEAD_DIM]
            vh = kv_ref[:, D_MODEL + h * X_HEAD_DIM:D_MODEL + (h + 1) * X_HEAD_DIM]
            s = lax.dot_general(qh, kh, (((1,), (1,)), ((), ())), preferred_element_type=F32)
            s = s - jnp.max(s, axis=1, keepdims=True)
            p = jnp.exp(s)
            p = p / jnp.sum(p, axis=1, keepdims=True)
            outs.append(jnp.dot(p.astype(BF16), vh, preferred_element_type=F32))
        o = jnp.concatenate(outs, axis=1).astype(BF16)
        x2 = x + jnp.dot(o, wo_ref[...], preferred_element_type=F32)
        x2_ref[rows, :] = x2
        hf = _rms(x2, gf_ref[...]).astype(BF16)
        h_ref[rows, :] = hf
        lg = lax.dot_general(wr_ref[...], hf, (((1,), (1,)), ((), ())), preferred_element_type=F32)
        lg = lg - jnp.max(lg, axis=0, keepdims=True)
        e = jnp.exp(lg)
        aff_ref[:, rows] = e / jnp.sum(e, axis=0, keepdims=True)


def _merge_cross_attn(xt, a, yc, x0, zc, fbias, proj, wa, wh, wmo, kv, gx, wq, wo, gf, wr_t, B, L,
                      tm=512):
    T = B * L
    nl = L // tm
    full = lambda r, c: pl.BlockSpec((r, c), lambda b, i: (0, 0))
    tok = lambda c, cb=0: pl.BlockSpec((tm, c), lambda b, i, cb=cb: (b * nl + i, cb))
    chunked = pl.BlockSpec((1, HY_WIDTH // LANES, tm, LANES), lambda b, i: (b, 0, i, 0))
    return pl.pallas_call(
        _mix_xattn_kernel,
        grid=(B, nl),
        in_specs=[tok(D_MODEL), tok(ATT_V_W), chunked,
                  tok(HY_WIDTH), chunked, full(1, HY_WIDTH), tok(D_MODEL, 3), tok(D_MODEL, 4),
                  full(ATT_V_W, D_MODEL), full(HY_WIDTH, D_MODEL), full(D_MODEL, D_MODEL),
                  pl.BlockSpec((N_MEM, 2 * D_MODEL), lambda b, i: (b, 0)),
                  full(1, D_MODEL), full(D_MODEL, D_MODEL), full(D_MODEL, D_MODEL),
                  full(1, D_MODEL), full(N_EXPERTS, D_MODEL)],
        out_specs=[tok(D_MODEL), tok(D_MODEL),
                   pl.BlockSpec((N_EXPERTS, tm), lambda b, i: (0, b * nl + i))],
        out_shape=[jax.ShapeDtypeStruct((T, D_MODEL), F32),
                   jax.ShapeDtypeStruct((T, D_MODEL), BF16),
                   jax.ShapeDtypeStruct((N_EXPERTS, T), F32)],
        compiler_params=_cparams("parallel", "parallel"),
        name="merge_cross_attn",
    )(xt, a, yc, x0, zc, fbias.reshape(1, HY_WIDTH), proj, proj, wa, wh, wmo, kv,
      gx.reshape(1, D_MODEL), wq, wo, gf.reshape(1, D_MODEL), wr_t)


def _select_kernel(aff_ref, gate_ref, *, cap):
    aff = aff_ref[...]
    bits = pltpu.bitcast(aff, I32)
    T = aff.shape[1]

    def count(mask):
        return jnp.sum(mask.astype(I32), axis=1, keepdims=True)

    def vbody(k, thr):
        cand = thr | (jnp.int32(1) << (30 - k))
        return jnp.where(count(bits >= cand) >= cap, cand, thr)

    thr = lax.fori_loop(0, 31, vbody, jnp.zeros((N_EXPERTS, 1), I32))
    gt = bits > thr
    eq = bits == thr
    need = cap - count(gt)
    tok = lax.broadcasted_iota(I32, aff.shape, 1)
    nbit = int(T).bit_length()

    def jbody(k, j):
        cand = j + (jnp.int32(1) << (nbit - 1 - k))
        return jnp.where(count(eq & (tok < cand)) <= need, cand, j)

    j = lax.fori_loop(0, nbit, jbody, jnp.zeros((N_EXPERTS, 1), I32))
    sel = gt | (eq & (tok < j))
    gate_ref[...] = jnp.where(sel, aff, 0.0)


def _select(aff_t, cap):
    E, T = aff_t.shape
    return pl.pallas_call(
        functools.partial(_select_kernel, cap=cap),
        grid=(1,),
        in_specs=[pl.BlockSpec((E, T), lambda i: (0, 0))],
        out_specs=pl.BlockSpec((E, T), lambda i: (0, 0)),
        out_shape=jax.ShapeDtypeStruct((E, T), F32),
        compiler_params=_cparams("arbitrary"),
        name="ec_select",
    )(aff_t)


EC_TILE = 256
EC_TILE_SEQS = 16


def _ec_tiling(B, L):
    tb = min(B, EC_TILE_SEQS)
    tl = EC_TILE // tb
    assert B % tb == 0 and L % tl == 0 and tl % 16 == 0
    return tb, tl, (48 if tb >= EC_TILE_SEQS else 64)


def _to_tile_order(a, B, L):
    tb, tl, _ = _ec_tiling(B, L)
    E = a.shape[0]
    return a.reshape(E, B // tb, tb, L // tl, tl).transpose(0, 3, 1, 2, 4).reshape(E, B * L)


def _rank_kernel(g_ref, rank_ref, cnt_ref):
    n = g_ref.shape[1]
    sel = jnp.where(g_ref[...] > 0.0, 1.0, 0.0).astype(BF16)
    r = lax.broadcasted_iota(I32, (n, n), 0)
    c = lax.broadcasted_iota(I32, (n, n), 1)
    before = jnp.where(r < c, 1.0, 0.0).astype(BF16)
    rank_ref[...] = jnp.dot(sel, before, preferred_element_type=F32)
    cnt_ref[...] = jnp.dot(sel, jnp.ones((n, LANES), BF16), preferred_element_type=F32)


def _tile_ranks(gate):
    E, T = gate.shape
    rows = E * T // EC_TILE
    rb = min(512, rows)
    rank, cnt = pl.pallas_call(
        _rank_kernel,
        grid=(rows // rb,),
        in_specs=[pl.BlockSpec((rb, EC_TILE), lambda i: (i, 0))],
        out_specs=[pl.BlockSpec((rb, EC_TILE), lambda i: (i, 0)),
                   pl.BlockSpec((rb, LANES), lambda i: (i, 0))],
        out_shape=[jax.ShapeDtypeStruct((rows, EC_TILE), F32),
                   jax.ShapeDtypeStruct((rows, LANES), F32)],
        compiler_params=_cparams("parallel"),
        name="ec_rank",
    )(gate.reshape(rows, EC_TILE))
    return rank.reshape(E, T), cnt[:, 0].reshape(E, T // EC_TILE)


def _dispatch_kernel(h_ref, rank_ref, gate_ref, xe_ref):
    slots = xe_ref.shape[1]
    slot = lax.broadcasted_iota(I32, (slots, EC_TILE), 0).astype(F32)
    parts = []
    for e in range(N_EXPERTS):
        hit = (rank_ref[e:e + 1, :] == slot) & (gate_ref[e:e + 1, :] > 0.0)
        parts.append(jnp.where(hit, 1.0, 0.0).astype(BF16))
    onehot = jnp.concatenate(parts, axis=0)
    h = h_ref[...].reshape(EC_TILE, D_MODEL)
    xe = jnp.dot(onehot, h, preferred_element_type=F32).astype(BF16)
    xe_ref[...] = xe.reshape(N_EXPERTS, slots, D_MODEL)


def _tile_spec(B, L, width):
    tb, tl, _ = _ec_tiling(B, L)
    groups = B // tb
    return pl.BlockSpec((tb, tl, width), lambda j, *_: (j % groups, j // groups, 0))


def _dispatch(h, rank, gate, B, L):
    T = B * L
    nt = T // EC_TILE
    slots = _ec_tiling(B, L)[2]
    return pl.pallas_call(
        _dispatch_kernel,
        grid=(nt,),
        in_specs=[_tile_spec(B, L, D_MODEL),
                  pl.BlockSpec((N_EXPERTS, EC_TILE), lambda i: (0, i)),
                  pl.BlockSpec((N_EXPERTS, EC_TILE), lambda i: (0, i))],
        out_specs=pl.BlockSpec((N_EXPERTS, None, slots, D_MODEL), lambda i: (0, i, 0, 0)),
        out_shape=jax.ShapeDtypeStruct((N_EXPERTS, nt, slots, D_MODEL), BF16),
        compiler_params=_cparams("parallel"),
        name="ec_dispatch",
    )(h.reshape(B, L, D_MODEL), rank, gate)


def _swiglu(h, wg_ref, wu_ref, wd_ref):
    half = EXPERT_FF // 2
    y = None
    for c in range(2):
        sl = slice(c * half, (c + 1) * half)
        a = jnp.dot(h, wg_ref[:, sl], preferred_element_type=F32)
        b = jnp.dot(h, wu_ref[:, sl], preferred_element_type=F32)
        he = (jax.nn.silu(a) * b).astype(BF16)
        d = jnp.dot(he, wd_ref[sl, :], preferred_element_type=F32)
        y = d if y is None else y + d
    return y


def _expert_rows_kernel(x_ref, wg_ref, wu_ref, wd_ref, o_ref):
    o_ref[0] = _swiglu(x_ref[0], wg_ref.at[0], wu_ref.at[0], wd_ref.at[0]).astype(o_ref.dtype)


def _expert_rows(xe, wg, wu, wd):
    E, R, _ = xe.shape
    rt = math.gcd(R, 1024)
    rows = pl.BlockSpec((1, rt, D_MODEL), lambda e, j: (e, j, 0))
    wspec = lambda r, c: pl.BlockSpec((1, r, c), lambda e, j: (e, 0, 0))
    return pl.pallas_call(
        _expert_rows_kernel,
        grid=(E, R // rt),
        in_specs=[rows, wspec(D_MODEL, EXPERT_FF), wspec(D_MODEL, EXPERT_FF), wspec(EXPERT_FF, D_MODEL)],
        out_specs=rows,
        out_shape=jax.ShapeDtypeStruct((E, R, D_MODEL), BF16),
        compiler_params=_cparams("parallel", "parallel"),
        name="expert_rows",
    )(xe, wg, wu, wd)


def _combine_kernel(flag_ref, tflag_ref, ye_ref, rank_ref, gate_ref, ex_ref, sl_ref, x_ref, gn_ref,
                    h_hbm, wg_hbm, wu_hbm, wd_hbm, o_ref, wbuf, hbuf, sem, *, groups):
    i = pl.program_id(0)
    slots = ye_ref.shape[1]
    blk = o_ref.shape
    ye = ye_ref[...].reshape(N_EXPERTS * slots, D_MODEL)
    rank = rank_ref[...]
    gate = gate_ref[...]
    r = jnp.dot(rank.astype(BF16), ex_ref[...], preferred_element_type=F32)
    g = jnp.dot(gate.astype(BF16), ex_ref[...], preferred_element_type=F32)
    q = jnp.where(r == sl_ref[...], g, 0.0).astype(BF16)
    o_ref[...] = x_ref[...] + jnp.dot(q, ye, preferred_element_type=F32).reshape(blk)

    @pl.when(tflag_ref[i] > 0)
    def _():
        tb, tl = blk[0], blk[1]
        rows = pl.ds(pl.multiple_of((i // groups) * tl, tl), tl)
        hcp = pltpu.make_async_copy(h_hbm.at[pl.ds((i % groups) * tb, tb), rows, :], hbuf, sem.at[3])
        hcp.start()
        hcp.wait()

        def expert(e, carry):
            @pl.when(flag_ref[i * N_EXPERTS + e] > 0)
            def _():
                copies = [pltpu.make_async_copy(w.at[e], wbuf.at[k], sem.at[k])
                          for k, w in enumerate((wg_hbm, wu_hbm, wd_hbm))]
                for cp in copies:
                    cp.start()
                for cp in copies:
                    cp.wait()
                lane = lax.broadcasted_iota(I32, rank.shape, 1)
                keep = (lane == e) & (rank >= float(slots))
                ge = jnp.sum(jnp.where(keep, gate, 0.0), axis=1, keepdims=True)
                h = hbuf[...].reshape(EC_TILE, D_MODEL)
                o_ref[...] += (_swiglu(h, wbuf.at[0], wbuf.at[1], wbuf.at[2]) * ge).reshape(blk)
            return carry
        lax.fori_loop(0, N_EXPERTS, expert, 0)

    o_ref[...] = _rms(o_ref[...], gn_ref[...])


def _combine(ye, rank_t, gate_t, x2, h, flags, tflags, wg, wu, wd, gn, B, L):
    T = B * L
    nt = T // EC_TILE
    slots = ye.shape[2]
    es = N_EXPERTS * slots
    lane = jnp.arange(es, dtype=I32)
    expand = (lane[None, :] // slots == jnp.arange(N_EXPERTS, dtype=I32)[:, None]).astype(BF16)
    slot = (lane % slots).astype(F32).reshape(1, es)
    tok = lambda c: pl.BlockSpec((EC_TILE, c), lambda i, f, t: (i, 0))
    full = lambda r, c: pl.BlockSpec((r, c), lambda i, f, t: (0, 0))
    tile = _tile_spec(B, L, D_MODEL)
    tb, tl, _ = _ec_tiling(B, L)
    hbm = pl.BlockSpec(memory_space=pl.ANY)
    return pl.pallas_call(
        functools.partial(_combine_kernel, groups=B // tb),
        grid_spec=pltpu.PrefetchScalarGridSpec(
            num_scalar_prefetch=2,
            grid=(nt,),
            in_specs=[pl.BlockSpec((N_EXPERTS, None, slots, D_MODEL), lambda i, f, t: (0, i, 0, 0)),
                      tok(N_EXPERTS), tok(N_EXPERTS), full(N_EXPERTS, es), full(1, es),
                      tile, full(1, D_MODEL), hbm, hbm, hbm, hbm],
            out_specs=tile,
            scratch_shapes=[pltpu.VMEM((3, D_MODEL, EXPERT_FF), BF16),
                            pltpu.VMEM((tb, tl, D_MODEL), BF16),
                            pltpu.SemaphoreType.DMA((4,))]),
        out_shape=jax.ShapeDtypeStruct((B, L, D_MODEL), F32),
        compiler_params=_cparams("arbitrary"),
        name="ec_combine",
    )(flags, tflags, ye, rank_t, gate_t, expand, slot, x2.reshape(B, L, D_MODEL),
      gn.reshape(1, D_MODEL), h.reshape(B, L, D_MODEL), wg, wu, wd)


def _expert_mixture(h, gate, x2, wg, wu, wd, gn, B, L):
    nt = B * L // EC_TILE
    slots = _ec_tiling(B, L)[2]
    gate = _to_tile_order(gate, B, L)
    rank, cnt = _tile_ranks(gate)
    xe = _dispatch(h, rank, gate, B, L)
    ye = _expert_rows(xe.reshape(N_EXPERTS, nt * slots, D_MODEL), wg, wu, wd)
    over = cnt.T > slots
    flags = over.reshape(-1).astype(I32)
    tflags = over.any(axis=1).astype(I32)
    return _combine(ye.reshape(N_EXPERTS, nt, slots, D_MODEL), rank.T, gate.T, x2, h, flags, tflags,
                    wg, wu, wd, gn, B, L)


def _run_group(x, mem, W):
    B, L, _ = x.shape
    T = B * L
    xt = x.reshape(T, D_MODEL)
    proj = _inproj(xt, W['norm_mix'], W['w_in'], L)
    a = _diff_attention(proj, W['lam'], W['subln'], B, L)
    zc, x0 = _hyena_prep(proj, W['hy_conv_w'], W['hy_conv_b'], B, L)
    tables = _dft_tables(L)
    kf = _filter_spectrum(L, W, tables)
    yc = _long_conv(zc, kf, tables, nb=1 if L >= 4096 else 2)
    kv = _mem_kv(mem.reshape(B * N_MEM, D_MODEL), W['norm_mem'], W['w_xkv'])
    x2, hf, aff = _merge_cross_attn(xt, a, yc, x0, zc, W['filt_bias'], proj, W['w_br_attn'], W['w_br_hyena'],
                                    W['w_out'], kv, W['norm_x'], W['w_xq'], W['w_xo'], W['norm_ffn'],
                                    W['w_router_t'], B, L)
    gate = _select(aff, EC_CAPACITY_FACTOR * T // N_EXPERTS)
    return _expert_mixture(hf, gate, x2, W['w_exp_gate'], W['w_exp_up'], W['w_exp_down'],
                           W['norm_final'], B, L)


def kernel(x_prompt, x_sample, mem_prompt, mem_sample, norm_mix, w_in, lambda_q1, lambda_k1, lambda_q2, lambda_k2, subln, w_br_attn, hy_conv_w, hy_conv_b, filt_w1, filt_b1, filt_w2, filt_b2, filt_w3, filt_freq, filt_bias, w_br_hyena, w_out, norm_x, norm_mem, w_xq, w_xkv, w_xo, norm_ffn, w_router, w_exp_gate, w_exp_up, w_exp_down, norm_final):
    l = 0
    lam = (jnp.exp(jnp.sum(lambda_q1[l] * lambda_k1[l])) - jnp.exp(jnp.sum(lambda_q2[l] * lambda_k2[l]))
           + LAM_INIT).astype(F32)
    W = dict(
        norm_mix=norm_mix[l], w_in=w_in[l].astype(BF16), lam=lam, subln=subln[l],
        w_br_attn=w_br_attn[l].astype(BF16), hy_conv_w=hy_conv_w[l], hy_conv_b=hy_conv_b[l],
        filt_w1=filt_w1[l], filt_b1=filt_b1[l], filt_w2=filt_w2[l], filt_b2=filt_b2[l],
        filt_w3=filt_w3[l], filt_freq=filt_freq[l], filt_bias=filt_bias[l],
        w_br_hyena=w_br_hyena[l].astype(BF16), w_out=w_out[l].astype(BF16),
        norm_x=norm_x[l], norm_mem=norm_mem[l], w_xq=w_xq[l].astype(BF16),
        w_xkv=w_xkv[l].astype(BF16), w_xo=w_xo[l].astype(BF16), norm_ffn=norm_ffn[l],
        w_router_t=w_router[l].T.astype(BF16), w_exp_gate=w_exp_gate[l].astype(BF16),
        w_exp_up=w_exp_up[l].astype(BF16), w_exp_down=w_exp_down[l].astype(BF16),
        norm_final=norm_final)
    return (_run_group(x_prompt, mem_prompt, W), _run_group(x_sample, mem_sample, W))
```

```python
import functools
import math

import jax
import jax.numpy as jnp
from jax import lax
from jax.experimental import pallas as pl
from jax.experimental.pallas import tpu as pltpu

F32 = jnp.float32
BF16 = jnp.bfloat16
I32 = jnp.int32

D_MODEL = 1024
N_ATT_HEADS = 4
ATT_HEAD_DIM = 64
ATT_V_DIM = 128
ROT_DIM = 16
ROPE_THETA = 500000.0
ATT_QK_W = 512
ATT_V_W = 512
HY_WIDTH = 512
FILTER_EMB = 33
FILTER_HIDDEN = 64
DECAY_TARGET = 1e-2
FAST_DECAY_PCT = 0.3
SLOW_DECAY_PCT = 1.5
IN_W = 5120
N_MEM = 256
X_HEADS = 4
X_HEAD_DIM = 256
N_EXPERTS = 16
EC_CAPACITY_FACTOR = 2
EXPERT_FF = 1024
EPS = 1e-6
LAM_INIT = 0.8 - 0.6 * math.exp(-0.3 * 0)

LANES = 128
ATT_ROW_GROUPS = 16
ATT_Q_TILE = 2048
MIX_ROW_GROUPS = 1
FFT_UNROLL = 8
VMEM_LIMIT_BYTES = 56 * 1024 * 1024


def _cparams(*sem):
    return pltpu.CompilerParams(dimension_semantics=sem, vmem_limit_bytes=VMEM_LIMIT_BYTES)


def _rms(x, g):
    return x * lax.rsqrt(jnp.mean(x * x, axis=-1, keepdims=True) + EPS) * g


def _inproj_kernel(x_ref, g_ref, w_ref, cos_ref, sa_ref, sb_ref, o_ref):
    u = _rms(x_ref[...], g_ref[...]).astype(BF16)
    n_col = IN_W // 512
    for j in range(n_col):
        acc = jnp.dot(u, w_ref[:, j * 512:(j + 1) * 512], preferred_element_type=F32)
        if j < 2:
            parts = []
            for h in range(4):
                t = acc[:, h * LANES:(h + 1) * LANES]
                r = (t * cos_ref[...] + pltpu.roll(t, 8, 1) * sa_ref[...]
                     + pltpu.roll(t, LANES - 8, 1) * sb_ref[...])
                parts.append(r)
            acc = jnp.concatenate(parts, axis=1)
        if j == 0:
            acc = acc * (ATT_HEAD_DIM ** -0.5 * math.log2(math.e))
        o_ref[:, j * 512:(j + 1) * 512] = acc.astype(BF16)


def _rope_tables(L):
    inv = ROPE_THETA ** (-jnp.arange(0, ROT_DIM, 2, dtype=F32) / ROT_DIM)
    pos = jnp.arange(L, dtype=F32)
    ang = pos[:, None] * inv[None, :]
    cos, sin = jnp.cos(ang), jnp.sin(ang)
    one = jnp.ones((L, 48), F32)
    zero8 = jnp.zeros((L, 8), F32)
    zero48 = jnp.zeros((L, 48), F32)
    c64 = jnp.concatenate([cos, cos, one], axis=1)
    sa64 = jnp.concatenate([zero8, sin, zero48], axis=1)
    sb64 = jnp.concatenate([-sin, zero8, zero48], axis=1)
    tile2 = lambda a: jnp.concatenate([a, a], axis=1)
    return tile2(c64), tile2(sa64), tile2(sb64)


def _inproj(xt, g, w_in_bf, L, tm=512):
    T = xt.shape[0]
    cos, sa, sb = _rope_tables(L)
    nl = L // tm
    tab = pl.BlockSpec((tm, LANES), lambda i: (i % nl, 0))
    return pl.pallas_call(
        _inproj_kernel,
        grid=(T // tm,),
        in_specs=[pl.BlockSpec((tm, D_MODEL), lambda i: (i, 0)),
                  pl.BlockSpec((1, D_MODEL), lambda i: (0, 0)),
                  pl.BlockSpec((D_MODEL, IN_W), lambda i: (0, 0)),
                  tab, tab, tab],
        out_specs=pl.BlockSpec((tm, IN_W), lambda i: (i, 0)),
        out_shape=jax.ShapeDtypeStruct((T, IN_W), BF16),
        compiler_params=_cparams("parallel"),
        name="inproj",
    )(xt, g.reshape(1, D_MODEL), w_in_bf, cos, sa, sb)


def _attn_kernel(lam_ref, q_ref, k_ref, v_ref, sg_ref, o_ref, vx_ref, *, seq, kc):
    tq = q_ref.shape[0]

    @pl.when(pl.program_id(2) == 0)
    def _():
        lane = lax.broadcasted_iota(I32, (seq, LANES), 1)
        vx_ref[:, :LANES] = v_ref[...]
        vx_ref[:, LANES:] = jnp.where(lane == 0, 1.0, 0.0).astype(BF16)

    q = q_ref[...]
    lane = lax.broadcasted_iota(I32, q.shape, 1)
    zero = jnp.zeros_like(q)
    qq = jnp.concatenate([jnp.where(lane < ATT_HEAD_DIM, q, zero),
                          jnp.where(lane >= ATT_HEAD_DIM, q, zero)], axis=0)

    rows = 2 * tq // ATT_ROW_GROUPS
    groups = [qq[g * rows:(g + 1) * rows] for g in range(ATT_ROW_GROUPS)]

    def scores(qg, c):
        start = pl.multiple_of(c * kc, kc)
        s = lax.dot_general(qg, k_ref[pl.ds(start, kc), :], (((1,), (1,)), ((), ())),
                            preferred_element_type=F32)
        return s, vx_ref[pl.ds(start, kc), :]

    if seq == kc:
        accs = []
        for qg in groups:
            s, vx = scores(qg, 0)
            p = jnp.exp2(s - jnp.max(s, axis=1, keepdims=True))
            accs.append(jnp.dot(p.astype(BF16), vx, preferred_element_type=F32))
    else:
        def body(c, carry):
            out = []
            for qg, (m, acc) in zip(groups, carry):
                s, vx = scores(qg, c)
                m_new = jnp.maximum(m, jnp.max(s, axis=1, keepdims=True))
                p = jnp.exp2(s - m_new)
                pv = jnp.dot(p.astype(BF16), vx, preferred_element_type=F32)
                out.append((m_new, jnp.exp2(m - m_new) * acc + pv))
            return tuple(out)

        init = tuple((jnp.full((rows, 1), -jnp.inf, F32), jnp.zeros((rows, 2 * LANES), F32))
                     for _ in groups)
        accs = [a for _, a in lax.fori_loop(0, seq // kc, body, init)]
    acc = jnp.concatenate(accs, axis=0)
    o = acc[:, :ATT_V_DIM] / acc[:, ATT_V_DIM:ATT_V_DIM + 1]
    a = o[:tq] - lam_ref[0] * o[tq:]
    a = _rms(a, sg_ref[...]) * (1.0 - LAM_INIT)
    o_ref[...] = a.astype(BF16)


def _diff_attention(proj, lam, subln, B, L):
    T = B * L
    tq = min(ATT_Q_TILE, L)
    nq = L // tq
    kc = min(2048, L)
    return pl.pallas_call(
        functools.partial(_attn_kernel, seq=L, kc=kc),
        grid=(B, N_ATT_HEADS, nq),
        in_specs=[pl.BlockSpec(memory_space=pltpu.SMEM),
                  pl.BlockSpec((tq, LANES), lambda b, h, i: (b * nq + i, h)),
                  pl.BlockSpec((L, LANES), lambda b, h, i: (b, 4 + h)),
                  pl.BlockSpec((L, LANES), lambda b, h, i: (b, 8 + h)),
                  pl.BlockSpec((1, ATT_V_DIM), lambda b, h, i: (0, 0))],
        out_specs=pl.BlockSpec((tq, ATT_V_DIM), lambda b, h, i: (b * nq + i, h)),
        out_shape=jax.ShapeDtypeStruct((T, ATT_V_W), BF16),
        scratch_shapes=[pltpu.VMEM((L, 2 * LANES), BF16)],
        compiler_params=_cparams("parallel", "parallel", "arbitrary"),
        name="diff_attn",
    )(lam.reshape(1), proj, proj, proj, subln.reshape(1, ATT_V_DIM))


def _hyprep_kernel(hy_ref, prev_ref, next_ref, w_ref, b_ref, zc_ref, x0_ref):
    i = pl.program_id(1)
    n = pl.num_programs(1)
    hy = hy_ref[...].astype(F32)
    tl = hy.shape[0]
    prev_row = jnp.where(i == 0, 0.0, prev_ref[15:16, :].astype(F32))
    next_row = jnp.where(i == n - 1, 0.0, next_ref[0:1, :].astype(F32))
    row = lax.broadcasted_iota(I32, hy.shape, 0)
    up = jnp.where(row == 0, prev_row, pltpu.roll(hy, 1, 0))
    dn = jnp.where(row == tl - 1, next_row, pltpu.roll(hy, tl - 1, 0))
    conv = up * w_ref[0:1, :] + hy * w_ref[1:2, :] + dn * w_ref[2:3, :] + b_ref[...]
    x0 = conv[:, :HY_WIDTH]
    x1 = conv[:, HY_WIDTH:2 * HY_WIDTH]
    hv = conv[:, 2 * HY_WIDTH:]
    z = hv * x1
    for c in range(HY_WIDTH // LANES):
        zc_ref[0, c] = z[:, c * LANES:(c + 1) * LANES].astype(zc_ref.dtype)
    x0_ref[...] = x0.astype(x0_ref.dtype)


def _hyena_prep(proj, conv_w, conv_b, B, L, tl=512):
    T = B * L
    nl = L // tl
    hw = 3 * HY_WIDTH
    rb = tl // 16
    nrb = T // 16
    return pl.pallas_call(
        _hyprep_kernel,
        grid=(B, nl),
        in_specs=[pl.BlockSpec((tl, hw), lambda b, i: (b * nl + i, 1)),
                  pl.BlockSpec((16, hw), lambda b, i: (jnp.maximum((b * nl + i) * rb - 1, 0), 1)),
                  pl.BlockSpec((16, hw), lambda b, i: (jnp.minimum((b * nl + i + 1) * rb, nrb - 1), 1)),
                  pl.BlockSpec((3, hw), lambda b, i: (0, 0)),
                  pl.BlockSpec((1, hw), lambda b, i: (0, 0))],
        out_specs=[pl.BlockSpec((1, HY_WIDTH // LANES, tl, LANES), lambda b, i: (b, 0, i, 0)),
                   pl.BlockSpec((tl, HY_WIDTH), lambda b, i: (b * nl + i, 0))],
        out_shape=[jax.ShapeDtypeStruct((B, HY_WIDTH // LANES, L, LANES), BF16),
                   jax.ShapeDtypeStruct((T, HY_WIDTH), BF16)],
        compiler_params=_cparams("parallel", "parallel"),
        name="hyena_prep",
    )(proj, proj, proj, conv_w, conv_b.reshape(1, hw))


def _fft_dims(L):
    n = 2 * L
    n1 = 1 << ((n.bit_length() - 1 + 1) // 2)
    n2 = n // n1
    A = n1 // 2
    K1 = n1 // 2 + 1
    K1p = -(-K1 // 8) * 8
    KP = -(-2 * K1p // LANES) * LANES
    assert n1 * n2 == n and A * n2 == L and n2 % 8 == 0 and A % 16 == 0
    return n, n1, n2, A, K1, K1p, KP


def _dft_tables(L):
    n, n1, n2, A, K1, K1p, KP = _fft_dims(L)
    b = jnp.arange(n2, dtype=I32)
    a = jnp.arange(A, dtype=I32)
    k1 = jnp.arange(K1p, dtype=I32)
    t = a[None, :] * n2 + b[:, None]
    m = (k1[None, :, None] * t[:, None, :]) % n
    th = m.astype(F32) * (2.0 * math.pi / n)
    valid = (k1 < K1).astype(F32)[None, :, None]
    c, s = jnp.cos(th) * valid, jnp.sin(th) * valid
    g_fwd = jnp.concatenate([c, -s], axis=1).astype(BF16)
    w = jnp.where((k1 == 0) | (k1 == n1 // 2), 1.0, 2.0)[None, :, None] / n
    ci = jnp.transpose(c * w, (0, 2, 1))
    si = jnp.transpose(-s * w, (0, 2, 1))
    pad = jnp.zeros((n2, A, KP - 2 * K1p), F32)
    g_inv = jnp.concatenate([ci, si, pad], axis=2).astype(BF16)
    k2 = jnp.arange(n2, dtype=I32)
    ph = ((k2[:, None] * b[None, :]) % n2).astype(F32) * (2.0 * math.pi / n2)
    c2, s2 = jnp.cos(ph), jnp.sin(ph)
    f2 = jnp.concatenate([jnp.concatenate([c2, s2], 1), jnp.concatenate([-s2, c2], 1)], 0).astype(BF16)
    f2i = jnp.concatenate([jnp.concatenate([c2, -s2], 1), jnp.concatenate([s2, c2], 1)], 0).astype(BF16)
    return g_fwd, g_inv, f2, f2i


def _fft_rows_fwd(x_ref, g_ref, yf_ref, *, nb, n2, K1p):
    def body(b, _):
        off = pl.multiple_of(b * LANES, LANES)
        xb = jnp.concatenate([x_ref[i, :, pl.ds(off, LANES)] for i in range(nb)], axis=1)
        y = jnp.dot(g_ref[b], xb, preferred_element_type=F32)
        for i in range(nb):
            for r in range(2 * K1p // 8):
                row = pl.multiple_of((r * n2 + b) * 8, 8)
                yf_ref[i, pl.ds(row, 8), :] = y[r * 8:(r + 1) * 8, i * LANES:(i + 1) * LANES]
        return 0
    lax.fori_loop(0, n2, body, 0, unroll=2 * FFT_UNROLL)


def _fft_cols_fwd(yf_ref, f2_ref, k1, *, nb, n2, K1p):
    kt, ks = k1 // 8, k1 % 8
    parts = []
    for i in range(nb):
        re = yf_ref[i, pl.ds(kt * (n2 * 8) + ks, n2, stride=8), :]
        im = yf_ref[i, pl.ds((K1p // 8 + kt) * (n2 * 8) + ks, n2, stride=8), :]
        parts.append(jnp.concatenate([re, im], axis=0))
    yk = jnp.concatenate(parts, axis=1).astype(BF16)
    return jnp.dot(f2_ref[...], yk, preferred_element_type=F32)


def _kspec_kernel(fb_ref, l1_ref, g_ref, f2_ref, kf_ref, yf_ref, *, n2, K1p):
    _fft_rows_fwd(fb_ref, g_ref, yf_ref, nb=2, n2=n2, K1p=K1p)
    inv = 1.0 / l1_ref[...]

    def body(k1, _):
        z = _fft_cols_fwd(yf_ref, f2_ref, k1, nb=2, n2=n2, K1p=K1p)
        zf, zb = z[:, :LANES], z[:, LANES:]
        kr = (zf[:n2] + zb[:n2]) * inv
        ki = (zf[n2:] - zb[n2:]) * inv
        kf_ref[k1] = jnp.concatenate([kr, ki], axis=0).astype(kf_ref.dtype)
        return 0
    lax.fori_loop(0, K1p, body, 0, unroll=FFT_UNROLL)


def _lconv_kernel(x_ref, g_ref, f2_ref, f2i_ref, kf_ref, gi_ref, o_ref, yf_ref, wf_ref, *,
                  nb, n2, K1p, KP):
    _fft_rows_fwd(x_ref, g_ref, yf_ref, nb=nb, n2=n2, K1p=K1p)

    def kbody(k1, _):
        z = _fft_cols_fwd(yf_ref, f2_ref, k1, nb=nb, n2=n2, K1p=K1p)
        kf = kf_ref[k1].astype(F32)
        kr = jnp.concatenate([kf[:n2]] * nb, axis=1)
        ki = jnp.concatenate([kf[n2:]] * nb, axis=1)
        zr, zi = z[:n2], z[n2:]
        p = jnp.concatenate([zr * kr - zi * ki, zr * ki + zi * kr], axis=0).astype(BF16)
        v = jnp.dot(f2i_ref[...], p, preferred_element_type=F32)
        for i in range(nb):
            for r in range(2 * n2 // 8):
                row = pl.multiple_of((r * K1p + k1) * 8, 8)
                wf_ref[i, pl.ds(row, 8), :] = v[r * 8:(r + 1) * 8, i * LANES:(i + 1) * LANES]
        return 0
    lax.fori_loop(0, K1p, kbody, 0, unroll=FFT_UNROLL)

    def bbody(b, _):
        bt, bs = b // 8, b % 8
        parts = []
        for i in range(nb):
            re = wf_ref[i, pl.ds(bt * (K1p * 8) + bs, K1p, stride=8), :]
            im = wf_ref[i, pl.ds((n2 // 8 + bt) * (K1p * 8) + bs, K1p, stride=8), :]
            parts.append(jnp.concatenate([re, im, jnp.zeros((KP - 2 * K1p, LANES), F32)], axis=0))
        vb = jnp.concatenate(parts, axis=1).astype(BF16)
        ob = jnp.dot(gi_ref[b], vb, preferred_element_type=F32)
        off = pl.multiple_of(b * LANES, LANES)
        for i in range(nb):
            o_ref[i, :, pl.ds(off, LANES)] = ob[:, i * LANES:(i + 1) * LANES].astype(o_ref.dtype)
        return 0
    lax.fori_loop(0, n2, bbody, 0, unroll=2 * FFT_UNROLL)


def _filter_kernel(w1_ref, b1_ref, w2_ref, b2_ref, w3_ref, fr_ref, fb_ref, l1_ref, *, seq):
    i = pl.program_id(0)
    tl = fb_ref.shape[2]
    bands = (FILTER_EMB - 1) // 2
    pos = (i * tl + lax.broadcasted_iota(I32, (tl, 1), 0)).astype(F32)
    t = pos * (1.0 / (seq - 1))
    wpos = pos * (2.0 * math.pi / seq)
    lane = lax.broadcasted_iota(I32, (1, LANES), 1)
    band = jnp.where(lane <= bands, lane - 1, lane - 1 - bands).astype(F32)
    fvec = 1e-4 + band * ((bands - 1 - 1e-4) / (bands - 1))
    arg = wpos * fvec
    z = jnp.where(lane == 0, t, jnp.where(lane <= bands, jnp.cos(arg),
                                          jnp.where(lane <= 2 * bands, -jnp.sin(arg), 0.0)))
    fr = fr_ref[...]
    h = jnp.sin(fr * (jnp.dot(z, w1_ref[...], preferred_element_type=F32) + b1_ref[...]))
    h = jnp.sin(fr * (jnp.dot(h, w2_ref[...], preferred_element_type=F32) + b2_ref[...]))
    h = jnp.dot(h, w3_ref[...], preferred_element_type=F32)
    max_decay = math.log(DECAY_TARGET) / FAST_DECAY_PCT
    min_decay = math.log(DECAY_TARGET) / SLOW_DECAY_PCT
    ch = lax.broadcasted_iota(I32, (1, HY_WIDTH), 1).astype(F32)
    deltas = jnp.abs(min_decay + ch * ((max_decay - min_decay) / (HY_WIDTH - 1)))
    decay = jnp.exp(-t * deltas)
    fwd = h[:, :HY_WIDTH] * decay
    bwd = jnp.where(pos == 0.0, 0.0, h[:, HY_WIDTH:] * decay)
    for c in range(HY_WIDTH // LANES):
        fb_ref[0, c] = fwd[:, c * LANES:(c + 1) * LANES].astype(fb_ref.dtype)
        fb_ref[1, c] = bwd[:, c * LANES:(c + 1) * LANES].astype(fb_ref.dtype)
    part = jnp.sum(jnp.abs(fwd) + jnp.abs(bwd), axis=0, keepdims=True)

    @pl.when(i == 0)
    def _():
        l1_ref[...] = jnp.zeros_like(l1_ref)
    l1_ref[...] += part


def _filter_spectrum(L, W, tables, tl=512):
    n, n1, n2, A, K1, K1p, KP = _fft_dims(L)
    g_fwd, _, f2, _ = tables
    nc = HY_WIDTH // LANES
    w1 = jnp.pad(W['filt_w1'], ((0, LANES - FILTER_EMB), (0, 0)))
    full = lambda r, c: pl.BlockSpec((r, c), lambda i: (0, 0))
    fb, l1 = pl.pallas_call(
        functools.partial(_filter_kernel, seq=L),
        grid=(L // tl,),
        in_specs=[full(LANES, FILTER_HIDDEN), full(1, FILTER_HIDDEN),
                  full(FILTER_HIDDEN, FILTER_HIDDEN), full(1, FILTER_HIDDEN),
                  full(FILTER_HIDDEN, 2 * HY_WIDTH), full(1, FILTER_HIDDEN)],
        out_specs=[pl.BlockSpec((2, nc, tl, LANES), lambda i: (0, 0, i, 0)),
                   pl.BlockSpec((1, HY_WIDTH), lambda i: (0, 0))],
        out_shape=[jax.ShapeDtypeStruct((2, nc, L, LANES), BF16),
                   jax.ShapeDtypeStruct((1, HY_WIDTH), F32)],
        compiler_params=_cparams("arbitrary"),
        name="hyena_filter",
    )(w1, W['filt_b1'].reshape(1, -1), W['filt_w2'], W['filt_b2'].reshape(1, -1), W['filt_w3'],
      W['filt_freq'].reshape(1, -1))
    fbr = fb.reshape(2, nc, A, n2 * LANES)
    rows = (2 * K1p // 8) * n2 * 8
    return pl.pallas_call(
        functools.partial(_kspec_kernel, n2=n2, K1p=K1p),
        grid=(nc,),
        in_specs=[pl.BlockSpec((2, None, A, n2 * LANES), lambda c: (0, c, 0, 0)),
                  pl.BlockSpec((1, LANES), lambda c: (0, c)),
                  pl.BlockSpec((n2, 2 * K1p, A), lambda c: (0, 0, 0)),
                  pl.BlockSpec((2 * n2, 2 * n2), lambda c: (0, 0))],
        out_specs=pl.BlockSpec((None, K1p, 2 * n2, LANES), lambda c: (c, 0, 0, 0)),
        out_shape=jax.ShapeDtypeStruct((nc, K1p, 2 * n2, LANES), BF16),
        scratch_shapes=[pltpu.VMEM((2, rows, LANES), F32)],
        compiler_params=_cparams("arbitrary"),
        name="filter_spectrum",
    )(fbr, l1, g_fwd, f2)


def _long_conv(zc, kf, tables, nb):
    B, nc, L, _ = zc.shape
    n, n1, n2, A, K1, K1p, KP = _fft_dims(L)
    g_fwd, g_inv, f2, f2i = tables
    x = zc.reshape(B, nc, A, n2 * LANES)
    rows_y = (2 * K1p // 8) * n2 * 8
    rows_w = (2 * n2 // 8) * K1p * 8
    const = lambda shape: pl.BlockSpec(shape, lambda c, j: (0,) * len(shape),
                                       pipeline_mode=pl.Buffered(1))
    y = pl.pallas_call(
        functools.partial(_lconv_kernel, nb=nb, n2=n2, K1p=K1p, KP=KP),
        grid=(nc, B // nb),
        in_specs=[pl.BlockSpec((nb, None, A, n2 * LANES), lambda c, j: (j, c, 0, 0)),
                  const((n2, 2 * K1p, A)), const((2 * n2, 2 * n2)), const((2 * n2, 2 * n2)),
                  pl.BlockSpec((None, K1p, 2 * n2, LANES), lambda c, j: (c, 0, 0, 0)),
                  const((n2, A, KP))],
        out_specs=pl.BlockSpec((nb, None, A, n2 * LANES), lambda c, j: (j, c, 0, 0)),
        out_shape=jax.ShapeDtypeStruct((B, nc, A, n2 * LANES), BF16),
        scratch_shapes=[pltpu.VMEM((nb, rows_y, LANES), F32), pltpu.VMEM((nb, rows_w, LANES), F32)],
        compiler_params=_cparams("arbitrary", "arbitrary"),
        name="long_conv",
    )(x, g_fwd, f2, f2i, kf, g_inv)
    return y.reshape(B, nc, L, LANES)


def _merge_rows(rows, x_ref, a_ref, y_ref, x0_ref, z_ref, fb_ref, g0_ref, g1_ref, wa_ref, wh_ref, wo_ref):
    a = jnp.dot(a_ref[rows, :], wa_ref[...], preferred_element_type=F32)
    x0 = x0_ref[rows, :].astype(F32)
    chunks = range(HY_WIDTH // LANES)
    y = jnp.concatenate([y_ref[0, c, rows, :].astype(F32) for c in chunks], axis=1)
    z = jnp.concatenate([z_ref[0, c, rows, :].astype(F32) for c in chunks], axis=1)
    hz = ((y + z * fb_ref[...]) * x0).astype(BF16)
    hzp = jnp.dot(hz, wh_ref[...], preferred_element_type=F32)
    g0 = jax.nn.sigmoid(g0_ref[rows, :].astype(F32))
    g1 = jax.nn.sigmoid(g1_ref[rows, :].astype(F32))
    m = (g0 * a + g1 * hzp).astype(BF16)
    return x_ref[rows, :] + jnp.dot(m, wo_ref[...], preferred_element_type=F32)


def _memkv_kernel(m_ref, g_ref, w_ref, o_ref):
    u = _rms(m_ref[...], g_ref[...]).astype(BF16)
    o_ref[...] = jnp.dot(u, w_ref[...], preferred_element_type=F32).astype(BF16)


def _mem_kv(mem_t, g, w_xkv_bf):
    R = mem_t.shape[0]
    return pl.pallas_call(
        _memkv_kernel,
        grid=(R // N_MEM,),
        in_specs=[pl.BlockSpec((N_MEM, D_MODEL), lambda i: (i, 0)),
                  pl.BlockSpec((1, D_MODEL), lambda i: (0, 0)),
                  pl.BlockSpec((D_MODEL, 2 * D_MODEL), lambda i: (0, 0))],
        out_specs=pl.BlockSpec((N_MEM, 2 * D_MODEL), lambda i: (i, 0)),
        out_shape=jax.ShapeDtypeStruct((R, 2 * D_MODEL), BF16),
        compiler_params=_cparams("parallel"),
        name="mem_kv",
    )(mem_t, g.reshape(1, D_MODEL), w_xkv_bf)


def _mix_xattn_kernel(x_ref, a_ref, y_ref, x0_ref, z_ref, fb_ref, g0_ref, g1_ref, wa_ref, wh_ref, wmo_ref,
                      kv_ref, gx_ref, wq_ref, wo_ref, gf_ref, wr_ref, x2_ref, h_ref, aff_ref):
    step = x_ref.shape[0] // MIX_ROW_GROUPS
    for r in range(MIX_ROW_GROUPS):
        rows = slice(r * step, (r + 1) * step)
        x = _merge_rows(rows, x_ref, a_ref, y_ref, x0_ref, z_ref, fb_ref, g0_ref, g1_ref, wa_ref, wh_ref,
                        wmo_ref)
        u = _rms(x, gx_ref[...]).astype(BF16)
        q = jnp.dot(u, wq_ref[...], preferred_element_type=F32)
        q = (q * (X_HEAD_DIM ** -0.5)).astype(BF16)
        outs = []
        for h in range(X_HEADS):
            qh = q[:, h * X_HEAD_DIM:(h + 1) * X_HEAD_DIM]
            kh = kv_ref[:, h * X_HEAD_DIM:(h + 1) * X_HEAD_DIM]
            vh = kv_ref[:, D_MODEL + h * X_HEAD_DIM:D_MODEL + (h + 1) * X_HEAD_DIM]
            s = lax.dot_general(qh, kh, (((1,), (1,)), ((), ())), preferred_element_type=F32)
            s = s - jnp.max(s, axis=1, keepdims=True)
            p = jnp.exp(s)
            p = p / jnp.sum(p, axis=1, keepdims=True)
            outs.append(jnp.dot(p.astype(BF16), vh, preferred_element_type=F32))
        o = jnp.concatenate(outs, axis=1).astype(BF16)
        x2 = x + jnp.dot(o, wo_ref[...], preferred_element_type=F32)
        x2_ref[rows, :] = x2
        hf = _rms(x2, gf_ref[...]).astype(BF16)
        h_ref[rows, :] = hf
        lg = lax.dot_general(wr_ref[...], hf, (((1,), (1,)), ((), ())), preferred_element_type=F32)
        lg = lg - jnp.max(lg, axis=0, keepdims=True)
        e = jnp.exp(lg)
        aff_ref[:, rows] = e / jnp.sum(e, axis=0, keepdims=True)


def _merge_cross_attn(xt, a, yc, x0, zc, fbias, proj, wa, wh, wmo, kv, gx, wq, wo, gf, wr_t, B, L,
                      tm=512):
    T = B * L
    nl = L // tm
    full = lambda r, c: pl.BlockSpec((r, c), lambda b, i: (0, 0))
    tok = lambda c, cb=0: pl.BlockSpec((tm, c), lambda b, i, cb=cb: (b * nl + i, cb))
    chunked = pl.BlockSpec((1, HY_WIDTH // LANES, tm, LANES), lambda b, i: (b, 0, i, 0))
    return pl.pallas_call(
        _mix_xattn_kernel,
        grid=(B, nl),
        in_specs=[tok(D_MODEL), tok(ATT_V_W), chunked,
                  tok(HY_WIDTH), chunked, full(1, HY_WIDTH), tok(D_MODEL, 3), tok(D_MODEL, 4),
                  full(ATT_V_W, D_MODEL), full(HY_WIDTH, D_MODEL), full(D_MODEL, D_MODEL),
                  pl.BlockSpec((N_MEM, 2 * D_MODEL), lambda b, i: (b, 0)),
                  full(1, D_MODEL), full(D_MODEL, D_MODEL), full(D_MODEL, D_MODEL),
                  full(1, D_MODEL), full(N_EXPERTS, D_MODEL)],
        out_specs=[tok(D_MODEL), tok(D_MODEL),
                   pl.BlockSpec((N_EXPERTS, tm), lambda b, i: (0, b * nl + i))],
        out_shape=[jax.ShapeDtypeStruct((T, D_MODEL), F32),
                   jax.ShapeDtypeStruct((T, D_MODEL), BF16),
                   jax.ShapeDtypeStruct((N_EXPERTS, T), F32)],
        compiler_params=_cparams("parallel", "parallel"),
        name="merge_cross_attn",
    )(xt, a, yc, x0, zc, fbias.reshape(1, HY_WIDTH), proj, proj, wa, wh, wmo, kv,
      gx.reshape(1, D_MODEL), wq, wo, gf.reshape(1, D_MODEL), wr_t)


def _select_kernel(aff_ref, gate_ref, *, cap):
    aff = aff_ref[...]
    bits = pltpu.bitcast(aff, I32)
    T = aff.shape[1]

    def count(mask):
        return jnp.sum(mask.astype(I32), axis=1, keepdims=True)

    def vbody(k, thr):
        cand = thr | (jnp.int32(1) << (30 - k))
        return jnp.where(count(bits >= cand) >= cap, cand, thr)

    thr = lax.fori_loop(0, 31, vbody, jnp.zeros((N_EXPERTS, 1), I32))
    gt = bits > thr
    eq = bits == thr
    need = cap - count(gt)
    tok = lax.broadcasted_iota(I32, aff.shape, 1)
    nbit = int(T).bit_length()

    def jbody(k, j):
        cand = j + (jnp.int32(1) << (nbit - 1 - k))
        return jnp.where(count(eq & (tok < cand)) <= need, cand, j)

    j = lax.fori_loop(0, nbit, jbody, jnp.zeros((N_EXPERTS, 1), I32))
    sel = gt | (eq & (tok < j))
    gate_ref[...] = jnp.where(sel, aff, 0.0)


def _select(aff_t, cap):
    E, T = aff_t.shape
    return pl.pallas_call(
        functools.partial(_select_kernel, cap=cap),
        grid=(1,),
        in_specs=[pl.BlockSpec((E, T), lambda i: (0, 0))],
        out_specs=pl.BlockSpec((E, T), lambda i: (0, 0)),
        out_shape=jax.ShapeDtypeStruct((E, T), F32),
        compiler_params=_cparams("arbitrary"),
        name="ec_select",
    )(aff_t)


EC_TILE = 256
EC_TILE_SEQS = 16


def _ec_tiling(B, L):
    tb = min(B, EC_TILE_SEQS)
    tl = EC_TILE // tb
    assert B % tb == 0 and L % tl == 0 and tl % 16 == 0
    return tb, tl, 48


def _to_tile_order(a, B, L):
    tb, tl, _ = _ec_tiling(B, L)
    E = a.shape[0]
    return a.reshape(E, B // tb, tb, L // tl, tl).transpose(0, 3, 1, 2, 4).reshape(E, B * L)


def _rank_kernel(g_ref, rank_ref, cnt_ref):
    n = g_ref.shape[1]
    sel = jnp.where(g_ref[...] > 0.0, 1.0, 0.0).astype(BF16)
    r = lax.broadcasted_iota(I32, (n, n), 0)
    c = lax.broadcasted_iota(I32, (n, n), 1)
    before = jnp.where(r < c, 1.0, 0.0).astype(BF16)
    rank_ref[...] = jnp.dot(sel, before, preferred_element_type=F32)
    cnt_ref[...] = jnp.dot(sel, jnp.ones((n, LANES), BF16), preferred_element_type=F32)


def _tile_ranks(gate):
    E, T = gate.shape
    rows = E * T // EC_TILE
    rb = min(512, rows)
    rank, cnt = pl.pallas_call(
        _rank_kernel,
        grid=(rows // rb,),
        in_specs=[pl.BlockSpec((rb, EC_TILE), lambda i: (i, 0))],
        out_specs=[pl.BlockSpec((rb, EC_TILE), lambda i: (i, 0)),
                   pl.BlockSpec((rb, LANES), lambda i: (i, 0))],
        out_shape=[jax.ShapeDtypeStruct((rows, EC_TILE), F32),
                   jax.ShapeDtypeStruct((rows, LANES), F32)],
        compiler_params=_cparams("parallel"),
        name="ec_rank",
    )(gate.reshape(rows, EC_TILE))
    return rank.reshape(E, T), cnt[:, 0].reshape(E, T // EC_TILE)


def _dispatch_kernel(h_ref, rank_ref, gate_ref, xe_ref):
    slots = xe_ref.shape[1]
    slot = lax.broadcasted_iota(I32, (slots, EC_TILE), 0).astype(F32)
    parts = []
    for e in range(N_EXPERTS):
        hit = (rank_ref[e:e + 1, :] == slot) & (gate_ref[e:e + 1, :] > 0.0)
        parts.append(jnp.where(hit, 1.0, 0.0).astype(BF16))
    onehot = jnp.concatenate(parts, axis=0)
    h = h_ref[...].reshape(EC_TILE, D_MODEL)
    xe = jnp.dot(onehot, h, preferred_element_type=F32).astype(BF16)
    xe_ref[...] = xe.reshape(N_EXPERTS, slots, D_MODEL)


def _tile_spec(B, L, width):
    tb, tl, _ = _ec_tiling(B, L)
    groups = B // tb
    return pl.BlockSpec((tb, tl, width), lambda j, *_: (j % groups, j // groups, 0))


def _dispatch(h, rank, gate, B, L):
    T = B * L
    nt = T // EC_TILE
    slots = _ec_tiling(B, L)[2]
    return pl.pallas_call(
        _dispatch_kernel,
        grid=(nt,),
        in_specs=[_tile_spec(B, L, D_MODEL),
                  pl.BlockSpec((N_EXPERTS, EC_TILE), lambda i: (0, i)),
                  pl.BlockSpec((N_EXPERTS, EC_TILE), lambda i: (0, i))],
        out_specs=pl.BlockSpec((N_EXPERTS, None, slots, D_MODEL), lambda i: (0, i, 0, 0)),
        out_shape=jax.ShapeDtypeStruct((N_EXPERTS, nt, slots, D_MODEL), BF16),
        compiler_params=_cparams("parallel"),
        name="ec_dispatch",
    )(h.reshape(B, L, D_MODEL), rank, gate)


def _swiglu(h, wg_ref, wu_ref, wd_ref):
    half = EXPERT_FF // 2
    y = None
    for c in range(2):
        sl = slice(c * half, (c + 1) * half)
        a = jnp.dot(h, wg_ref[:, sl], preferred_element_type=F32)
        b = jnp.dot(h, wu_ref[:, sl], preferred_element_type=F32)
        he = (jax.nn.silu(a) * b).astype(BF16)
        d = jnp.dot(he, wd_ref[sl, :], preferred_element_type=F32)
        y = d if y is None else y + d
    return y


def _expert_rows_kernel(x_ref, wg_ref, wu_ref, wd_ref, o_ref):
    o_ref[0] = _swiglu(x_ref[0], wg_ref.at[0], wu_ref.at[0], wd_ref.at[0]).astype(o_ref.dtype)


def _expert_rows(xe, wg, wu, wd):
    E, R, _ = xe.shape
    rt = math.gcd(R, 1024)
    rows = pl.BlockSpec((1, rt, D_MODEL), lambda e, j: (e, j, 0))
    wspec = lambda r, c: pl.BlockSpec((1, r, c), lambda e, j: (e, 0, 0))
    return pl.pallas_call(
        _expert_rows_kernel,
        grid=(E, R // rt),
        in_specs=[rows, wspec(D_MODEL, EXPERT_FF), wspec(D_MODEL, EXPERT_FF), wspec(EXPERT_FF, D_MODEL)],
        out_specs=rows,
        out_shape=jax.ShapeDtypeStruct((E, R, D_MODEL), BF16),
        compiler_params=_cparams("parallel", "parallel"),
        name="expert_rows",
    )(xe, wg, wu, wd)


def _combine_kernel(flag_ref, tflag_ref, ye_ref, rank_ref, gate_ref, ex_ref, sl_ref, x_ref, gn_ref,
                    h_hbm, wg_hbm, wu_hbm, wd_hbm, o_ref, wbuf, hbuf, sem, *, groups):
    i = pl.program_id(0)
    slots = ye_ref.shape[1]
    blk = o_ref.shape
    ye = ye_ref[...].reshape(N_EXPERTS * slots, D_MODEL)
    rank = rank_ref[...]
    gate = gate_ref[...]
    r = jnp.dot(rank.astype(BF16), ex_ref[...], preferred_element_type=F32)
    g = jnp.dot(gate.astype(BF16), ex_ref[...], preferred_element_type=F32)
    q = jnp.where(r == sl_ref[...], g, 0.0).astype(BF16)
    o_ref[...] = x_ref[...] + jnp.dot(q, ye, preferred_element_type=F32).reshape(blk)

    @pl.when(tflag_ref[i] > 0)
    def _():
        tb, tl = blk[0], blk[1]
        rows = pl.ds(pl.multiple_of((i // groups) * tl, tl), tl)
        hcp = pltpu.make_async_copy(h_hbm.at[pl.ds((i % groups) * tb, tb), rows, :], hbuf, sem.at[3])
        hcp.start()
        hcp.wait()

        def expert(e, carry):
            @pl.when(flag_ref[i * N_EXPERTS + e] > 0)
            def _():
                copies = [pltpu.make_async_copy(w.at[e], wbuf.at[k], sem.at[k])
                          for k, w in enumerate((wg_hbm, wu_hbm, wd_hbm))]
                for cp in copies:
                    cp.start()
                for cp in copies:
                    cp.wait()
                lane = lax.broadcasted_iota(I32, rank.shape, 1)
                keep = (lane == e) & (rank >= float(slots))
                ge = jnp.sum(jnp.where(keep, gate, 0.0), axis=1, keepdims=True)
                h = hbuf[...].reshape(EC_TILE, D_MODEL)
                o_ref[...] += (_swiglu(h, wbuf.at[0], wbuf.at[1], wbuf.at[2]) * ge).reshape(blk)
            return carry
        lax.fori_loop(0, N_EXPERTS, expert, 0)

    o_ref[...] = _rms(o_ref[...], gn_ref[...])


def _combine(ye, rank_t, gate_t, x2, h, flags, tflags, wg, wu, wd, gn, B, L):
    T = B * L
    nt = T // EC_TILE
    slots = ye.shape[2]
    es = N_EXPERTS * slots
    lane = jnp.arange(es, dtype=I32)
    expand = (lane[None, :] // slots == jnp.arange(N_EXPERTS, dtype=I32)[:, None]).astype(BF16)
    slot = (lane % slots).astype(F32).reshape(1, es)
    tok = lambda c: pl.BlockSpec((EC_TILE, c), lambda i, f, t: (i, 0))
    full = lambda r, c: pl.BlockSpec((r, c), lambda i, f, t: (0, 0))
    tile = _tile_spec(B, L, D_MODEL)
    tb, tl, _ = _ec_tiling(B, L)
    hbm = pl.BlockSpec(memory_space=pl.ANY)
    return pl.pallas_call(
        functools.partial(_combine_kernel, groups=B // tb),
        grid_spec=pltpu.PrefetchScalarGridSpec(
            num_scalar_prefetch=2,
            grid=(nt,),
            in_specs=[pl.BlockSpec((N_EXPERTS, None, slots, D_MODEL), lambda i, f, t: (0, i, 0, 0)),
                      tok(N_EXPERTS), tok(N_EXPERTS), full(N_EXPERTS, es), full(1, es),
                      tile, full(1, D_MODEL), hbm, hbm, hbm, hbm],
            out_specs=tile,
            scratch_shapes=[pltpu.VMEM((3, D_MODEL, EXPERT_FF), BF16),
                            pltpu.VMEM((tb, tl, D_MODEL), BF16),
                            pltpu.SemaphoreType.DMA((4,))]),
        out_shape=jax.ShapeDtypeStruct((B, L, D_MODEL), F32),
        compiler_params=_cparams("arbitrary"),
        name="ec_combine",
    )(flags, tflags, ye, rank_t, gate_t, expand, slot, x2.reshape(B, L, D_MODEL),
      gn.reshape(1, D_MODEL), h.reshape(B, L, D_MODEL), wg, wu, wd)


def _expert_mixture(h, gate, x2, wg, wu, wd, gn, B, L):
    nt = B * L // EC_TILE
    slots = _ec_tiling(B, L)[2]
    gate = _to_tile_order(gate, B, L)
    rank, cnt = _tile_ranks(gate)
    xe = _dispatch(h, rank, gate, B, L)
    ye = _expert_rows(xe.reshape(N_EXPERTS, nt * slots, D_MODEL), wg, wu, wd)
    over = cnt.T > slots
    flags = over.reshape(-1).astype(I32)
    tflags = over.any(axis=1).astype(I32)
    return _combine(ye.reshape(N_EXPERTS, nt, slots, D_MODEL), rank.T, gate.T, x2, h, flags, tflags,
                    wg, wu, wd, gn, B, L)


def _run_group(x, mem, W):
    B, L, _ = x.shape
    T = B * L
    xt = x.reshape(T, D_MODEL)
    proj = _inproj(xt, W['norm_mix'], W['w_in'], L)
    a = _diff_attention(proj, W['lam'], W['subln'], B, L)
    zc, x0 = _hyena_prep(proj, W['hy_conv_w'], W['hy_conv_b'], B, L)
    tables = _dft_tables(L)
    kf = _filter_spectrum(L, W, tables)
    yc = _long_conv(zc, kf, tables, nb=1 if L >= 4096 else 2)
    kv = _mem_kv(mem.reshape(B * N_MEM, D_MODEL), W['norm_mem'], W['w_xkv'])
    x2, hf, aff = _merge_cross_attn(xt, a, yc, x0, zc, W['filt_bias'], proj, W['w_br_attn'], W['w_br_hyena'],
                                    W['w_out'], kv, W['norm_x'], W['w_xq'], W['w_xo'], W['norm_ffn'],
                                    W['w_router_t'], B, L)
    gate = _select(aff, EC_CAPACITY_FACTOR * T // N_EXPERTS)
    return _expert_mixture(hf, gate, x2, W['w_exp_gate'], W['w_exp_up'], W['w_exp_down'],
                           W['norm_final'], B, L)


def kernel(x_prompt, x_sample, mem_prompt, mem_sample, norm_mix, w_in, lambda_q1, lambda_k1, lambda_q2, lambda_k2, subln, w_br_attn, hy_conv_w, hy_conv_b, filt_w1, filt_b1, filt_w2, filt_b2, filt_w3, filt_freq, filt_bias, w_br_hyena, w_out, norm_x, norm_mem, w_xq, w_xkv, w_xo, norm_ffn, w_router, w_exp_gate, w_exp_up, w_exp_down, norm_final):
    l = 0
    lam = (jnp.exp(jnp.sum(lambda_q1[l] * lambda_k1[l])) - jnp.exp(jnp.sum(lambda_q2[l] * lambda_k2[l]))
           + LAM_INIT).astype(F32)
    W = dict(
        norm_mix=norm_mix[l], w_in=w_in[l].astype(BF16), lam=lam, subln=subln[l],
        w_br_attn=w_br_attn[l].astype(BF16), hy_conv_w=hy_conv_w[l], hy_conv_b=hy_conv_b[l],
        filt_w1=filt_w1[l], filt_b1=filt_b1[l], filt_w2=filt_w2[l], filt_b2=filt_b2[l],
        filt_w3=filt_w3[l], filt_freq=filt_freq[l], filt_bias=filt_bias[l],
        w_br_hyena=w_br_hyena[l].astype(BF16), w_out=w_out[l].astype(BF16),
        norm_x=norm_x[l], norm_mem=norm_mem[l], w_xq=w_xq[l].astype(BF16),
        w_xkv=w_xkv[l].astype(BF16), w_xo=w_xo[l].astype(BF16), norm_ffn=norm_ffn[l],
        w_router_t=w_router[l].T.astype(BF16), w_exp_gate=w_exp_gate[l].astype(BF16),
        w_exp_up=w_exp_up[l].astype(BF16), w_exp_down=w_exp_down[l].astype(BF16),
        norm_final=norm_final)
    return (_run_group(x_prompt, mem_prompt, W), _run_group(x_sample, mem_sample, W))
```
